```python
import jax, jax.numpy as jnp
from jax import lax
import numpy as np

D_MODEL = 1024
BATCH = 2
SEQ = 8192
DEPTH = 1
DEC_BATCH = 32
DEC_SEQ = 8
PAST_LEN = 8192
PAGE_SIZE = 128

HEAD_DIM = 64
MIX_W = D_MODEL
ATT_W = 3 * D_MODEL // 8
RWKV_W = 3 * D_MODEL // 8
MEM_W = D_MODEL // 4
ATT_HEADS = ATT_W // HEAD_DIM
RWKV_HEADS = RWKV_W // HEAD_DIM
MEM_HEADS = MEM_W // HEAD_DIM
DILATED = ((128, 1), (512, 4), (2048, 16))
MAX_WINDOW = 2048
BAND_UNITS = 128
ROPE_DIM = HEAD_DIM // 4
ROPE_THETA = 500000.0
N_MEM = 256
LORA_W = 64
LORA_A = 64
SHIFT_W = 3 * RWKV_W + LORA_W + LORA_A
IN_SPLITS = (ATT_W, 2 * ATT_W, 3 * ATT_W, 3 * ATT_W + SHIFT_W, 3 * ATT_W + SHIFT_W + MEM_W)
IN_W = 3 * ATT_W + SHIFT_W + MEM_W + MIX_W
NORM_EPS = 1e-6
LNX_EPS = 64e-5

kernel_name = "hybrid_dilated_rwkv7_memory_step"


def rms_norm(x, g):
    xf = x.astype(jnp.float32)
    y = xf * lax.rsqrt(jnp.mean(xf * xf, axis=-1, keepdims=True) + NORM_EPS)
    return (y * g.astype(jnp.float32)).astype(x.dtype)


def partial_rope(x, pos):
    half = ROPE_DIM // 2
    inv_freq = ROPE_THETA ** (-jnp.arange(half, dtype=jnp.float32) / half)
    ang = pos.astype(jnp.float32)[:, None] * inv_freq[None, :]
    cos = jnp.cos(ang)[None, :, None, :]
    sin = jnp.sin(ang)[None, :, None, :]
    xr = x[..., :ROPE_DIM].astype(jnp.float32)
    x1, x2 = xr[..., :half], xr[..., half:]
    rot = jnp.concatenate([x1 * cos - x2 * sin, x2 * cos + x1 * sin], axis=-1)
    return jnp.concatenate([rot.astype(x.dtype), x[..., ROPE_DIM:]], axis=-1)


def input_projection(x, norm_g, w_in):
    B, S, _ = x.shape
    z = rms_norm(x, norm_g) @ w_in
    q_a, k_a, v_a, z_b, q_m, gate = jnp.split(z, IN_SPLITS, axis=-1)
    heads = lambda t, n: t.reshape(B, S, n, HEAD_DIM)
    return (heads(q_a, ATT_HEADS), heads(k_a, ATT_HEADS), heads(v_a, ATT_HEADS),
            z_b, heads(q_m, MEM_HEADS), gate)


def dilated_branch_prompt(q, k, v, dil):
    B, S, H, Dh = q.shape
    Qb = BAND_UNITS
    span = dil * Qb
    S_pad = -(-S // span) * span
    L = S_pad // dil
    nb = L // Qb
    pad = ((0, 0), (0, S_pad - S), (0, 0), (0, 0))

    def to_blocks(t):
        t = jnp.pad(t, pad).reshape(B, L, dil, H, Dh)
        return t.transpose(0, 2, 3, 1, 4).reshape(B, dil, H, nb, Qb, Dh)

    def with_prev(t):
        prev = jnp.pad(t[:, :, :, :-1], ((0, 0), (0, 0), (0, 0), (1, 0), (0, 0), (0, 0)))
        return jnp.concatenate([prev, t], axis=4)

    qb = to_blocks(q)
    kb = with_prev(to_blocks(k))
    vb = with_prev(to_blocks(v)).astype(jnp.float32)
    s = jnp.einsum('brhnqd,brhnkd->brhnqk', qb, kb).astype(jnp.float32)
    qi = jnp.arange(Qb)[:, None]
    kj = jnp.arange(2 * Qb)[None, :]
    dist = Qb + qi - kj
    band = (dist >= 0) & (dist <= BAND_UNITS)
    has_prev = (jnp.arange(nb)[:, None, None] > 0) | (kj >= Qb)[None]
    mask = band[None] & has_prev
    s = jnp.where(mask, s, -jnp.inf)
    m = jnp.max(s, axis=-1, keepdims=True)
    p = jnp.exp(s - m)
    l = jnp.sum(p, axis=-1, keepdims=True)
    o = jnp.einsum('brhnqk,brhnkd->brhnqd', p, vb) / l
    lse = (m + jnp.log(l))[..., 0]
    o = o.reshape(B, dil, H, L, Dh).transpose(0, 3, 1, 2, 4).reshape(B, S_pad, H, Dh)[:, :S]
    lse = lse.reshape(B, dil, H, L).transpose(0, 3, 1, 2).reshape(B, S_pad, H)[:, :S]
    return o, lse


def dilated_branch_sample(q, k_all, v_all, dil, n_past):
    T = q.shape[1]
    j = jnp.arange(BAND_UNITS + 1)
    idx = n_past + jnp.arange(T)[:, None] - dil * j[None, :]
    valid = idx >= 0
    idx_c = jnp.maximum(idx, 0)
    kg = k_all[:, idx_c]
    vg = v_all[:, idx_c].astype(jnp.float32)
    s = jnp.einsum('bthd,btjhd->bthj', q, kg).astype(jnp.float32)
    s = jnp.where(valid[None, :, None, :], s, -jnp.inf)
    m = jnp.max(s, axis=-1, keepdims=True)
    p = jnp.exp(s - m)
    l = jnp.sum(p, axis=-1, keepdims=True)
    o = jnp.einsum('bthj,btjhd->bthd', p, vg) / l
    return o, (m + jnp.log(l))[..., 0]


def combine_dilations(outs, lses, dtype):
    w = jax.nn.softmax(jnp.stack(lses, axis=-1), axis=-1)
    o = sum(w[..., i, None] * outs[i] for i in range(len(outs)))
    return o.astype(dtype)


def rwkv7_branch(z, shift0, state0, mu, w0, w2, a0, a2, k_k, k_a, r_k, lnx_g, lnx_b):
    B, S, _ = z.shape
    H, N = RWKV_HEADS, HEAD_DIM
    f32 = jnp.float32
    zf = z.astype(f32)
    z_prev = jnp.concatenate([shift0[:, None].astype(f32), zf[:, :-1]], axis=1)
    zs = zf + (z_prev - zf) * mu.astype(f32)
    r, k, v, wl, al = jnp.split(zs, (RWKV_W, 2 * RWKV_W, 3 * RWKV_W, 3 * RWKV_W + LORA_W), axis=-1)
    w = -jax.nn.softplus(-(w0.astype(f32) + jnp.tanh(wl) @ w2.astype(f32))) - 0.5
    decay = jnp.exp(-jnp.exp(w))
    a = jax.nn.sigmoid(a0.astype(f32) + al @ a2.astype(f32))
    heads = lambda t: t.reshape(B, S, H, N)
    kk = heads(k * k_k.astype(f32))
    kk = kk / jnp.maximum(jnp.sqrt(jnp.sum(kk * kk, axis=-1, keepdims=True)), 1e-12)
    k = k * (1.0 + (a - 1.0) * k_a.astype(f32))
    r, k, v, decay, a = heads(r), heads(k), heads(v), heads(decay), heads(a)
    b = kk * a

    def step(st, inp):
        r_t, k_t, v_t, w_t, kk_t, b_t = inp
        sa = jnp.einsum('bhvk,bhk->bhv', st, -kk_t)
        st = st * w_t[:, :, None, :] + sa[..., None] * b_t[:, :, None, :] + v_t[..., None] * k_t[:, :, None, :]
        return st, jnp.einsum('bhvk,bhk->bhv', st, r_t)

    tm = lambda t: jnp.swapaxes(t, 0, 1)
    state, y = lax.scan(step, state0.astype(f32), (tm(r), tm(k), tm(v), tm(decay), tm(kk), tm(b)))
    y = tm(y)
    mean = jnp.mean(y, axis=-1, keepdims=True)
    var = jnp.mean((y - mean) ** 2, axis=-1, keepdims=True)
    y = ((y - mean) * lax.rsqrt(var + LNX_EPS)).reshape(B, S, RWKV_W)
    y = y * lnx_g.astype(f32) + lnx_b.astype(f32)
    bonus = jnp.sum(r * k * r_k.astype(f32), axis=-1, keepdims=True) * v
    y = y + bonus.reshape(B, S, RWKV_W)
    return y.astype(z.dtype), state.astype(state0.dtype), z[:, -1]


def memory_attention(q, mem_k, mem_v):
    s = jnp.einsum('bshd,bmhd->bhsm', q, mem_k).astype(jnp.float32)
    p = jax.nn.softmax(s, axis=-1)
    return jnp.einsum('bhsm,bmhd->bshd', p, mem_v.astype(jnp.float32)).astype(q.dtype)


def memory_kv(mem, norm_g, w_kv):
    B = mem.shape[0]
    mk, mv = jnp.split(rms_norm(mem, norm_g) @ w_kv, 2, axis=-1)
    return mk.reshape(B, N_MEM, MEM_HEADS, HEAD_DIM), mv.reshape(B, N_MEM, MEM_HEADS, HEAD_DIM)


def output_projection(x, o_a, o_b, o_m, gate, w_out):
    B, S, _ = x.shape
    o = jnp.concatenate([o_a.reshape(B, S, ATT_W), o_b, o_m.reshape(B, S, MEM_W)], axis=-1)
    return x + (o * jax.nn.silu(gate)) @ w_out


def setup_inputs(seed: int = 0) -> dict:
    key = jax.random.key(seed)
    ks = jax.random.split(key, 32)
    f32 = jnp.float32
    nrm = lambda k, shape, scale: jax.random.normal(k, shape, f32) * scale
    Lr = DEPTH
    win_past = min(MAX_WINDOW, PAST_LEN)
    return {
        "x_prompt": nrm(ks[0], (BATCH, SEQ, D_MODEL), 1.0),
        "x_sample": nrm(ks[1], (DEC_BATCH, DEC_SEQ, D_MODEL), 1.0),
        "cache_win_k": nrm(ks[2], (Lr, DEC_BATCH, win_past, ATT_HEADS, HEAD_DIM), 1.0),
        "cache_win_v": nrm(ks[3], (Lr, DEC_BATCH, win_past, ATT_HEADS, HEAD_DIM), 1.0),
        "state_rwkv": nrm(ks[4], (Lr, DEC_BATCH, RWKV_HEADS, HEAD_DIM, HEAD_DIM), 1.0),
        "state_rwkv_shift": nrm(ks[5], (Lr, DEC_BATCH, SHIFT_W), 1.0),
        "cache_mem_k": nrm(ks[6], (Lr, DEC_BATCH, N_MEM, MEM_HEADS, HEAD_DIM), 1.0),
        "cache_mem_v": nrm(ks[7], (Lr, DEC_BATCH, N_MEM, MEM_HEADS, HEAD_DIM), 1.0),
        "mem_prompt": nrm(ks[8], (BATCH, N_MEM, D_MODEL), 1.0),
        "norm_in": 1.0 + nrm(ks[9], (Lr, D_MODEL), 0.02),
        "w_in": nrm(ks[10], (Lr, D_MODEL, IN_W), D_MODEL ** -0.5),
        "rwkv_mu": jax.random.uniform(ks[11], (Lr, SHIFT_W), f32),
        "rwkv_w0": jax.random.uniform(ks[12], (Lr, RWKV_W), f32, -6.0, -1.0),
        "rwkv_w2": nrm(ks[13], (Lr, LORA_W, RWKV_W), 0.5 * LORA_W ** -0.5),
        "rwkv_a0": nrm(ks[14], (Lr, RWKV_W), 0.1),
        "rwkv_a2": nrm(ks[15], (Lr, LORA_A, RWKV_W), LORA_A ** -0.5),
        "rwkv_k_k": 0.85 + nrm(ks[16], (Lr, RWKV_W), 0.05),
        "rwkv_k_a": 1.0 + nrm(ks[17], (Lr, RWKV_W), 0.05),
        "rwkv_r_k": nrm(ks[18], (Lr, RWKV_HEADS, HEAD_DIM), 0.1),
        "rwkv_lnx_g": 1.0 + nrm(ks[19], (Lr, RWKV_W), 0.02),
        "rwkv_lnx_b": nrm(ks[20], (Lr, RWKV_W), 0.02),
        "norm_mem": 1.0 + nrm(ks[21], (Lr, D_MODEL), 0.02),
        "w_mem_kv": nrm(ks[22], (Lr, D_MODEL, 2 * MEM_W), D_MODEL ** -0.5),
        "w_out": nrm(ks[23], (Lr, MIX_W, D_MODEL), MIX_W ** -0.5),
        "norm_final": 1.0 + nrm(ks[24], (D_MODEL,), 0.02),
    }


def reference(x_prompt, x_sample, cache_win_k, cache_win_v, state_rwkv, state_rwkv_shift,
              cache_mem_k, cache_mem_v, mem_prompt, norm_in, w_in, rwkv_mu, rwkv_w0, rwkv_w2,
              rwkv_a0, rwkv_a2, rwkv_k_k, rwkv_k_a, rwkv_r_k, rwkv_lnx_g, rwkv_lnx_b,
              norm_mem, w_mem_kv, w_out, norm_final):
    B, S, _ = x_prompt.shape
    DB, T, _ = x_sample.shape
    scale = HEAD_DIM ** -0.5
    pos_p = jnp.arange(S)
    pos_s = PAST_LEN + jnp.arange(T)
    win_p = min(MAX_WINDOW, S)
    n_past = cache_win_k.shape[2]
    xp, xs = x_prompt, x_sample
    wk_p, wv_p, st_p, sh_p, mk_p, mv_p = [], [], [], [], [], []
    wk_s, wv_s, st_s, sh_s = [], [], [], []
    for l in range(DEPTH):
        rw = (rwkv_mu[l], rwkv_w0[l], rwkv_w2[l], rwkv_a0[l], rwkv_a2[l], rwkv_k_k[l],
              rwkv_k_a[l], rwkv_r_k[l], rwkv_lnx_g[l], rwkv_lnx_b[l])
        q_a, k_a, v_a, z_b, q_m, gate = input_projection(xp, norm_in[l], w_in[l])
        q_a = partial_rope(q_a, pos_p) * scale
        k_a = partial_rope(k_a, pos_p)
        outs, lses = [], []
        for _, dil in DILATED:
            o_i, lse_i = dilated_branch_prompt(q_a, k_a, v_a, dil)
            outs.append(o_i)
            lses.append(lse_i)
        o_a = combine_dilations(outs, lses, xp.dtype)
        o_b, st_b, sh_b = rwkv7_branch(z_b, jnp.zeros((B, SHIFT_W), xp.dtype),
                                       jnp.zeros((B, RWKV_HEADS, HEAD_DIM, HEAD_DIM), xp.dtype), *rw)
        mk, mv = memory_kv(mem_prompt, norm_mem[l], w_mem_kv[l])
        o_m = memory_attention(q_m * scale, mk, mv)
        xp = output_projection(xp, o_a, o_b, o_m, gate, w_out[l])
        wk_p.append(k_a[:, S - win_p:])
        wv_p.append(v_a[:, S - win_p:])
        st_p.append(st_b)
        sh_p.append(sh_b)
        mk_p.append(mk)
        mv_p.append(mv)
        q_s, k_s, v_s, z_s, qm_s, gate_s = input_projection(xs, norm_in[l], w_in[l])
        q_s = partial_rope(q_s, pos_s) * scale
        k_s = partial_rope(k_s, pos_s)
        k_all = jnp.concatenate([cache_win_k[l], k_s], axis=1)
        v_all = jnp.concatenate([cache_win_v[l], v_s], axis=1)
        outs, lses = [], []
        for _, dil in DILATED:
            o_i, lse_i = dilated_branch_sample(q_s, k_all, v_all, dil, n_past)
            outs.append(o_i)
            lses.append(lse_i)
        oa_s = combine_dilations(outs, lses, xs.dtype)
        ob_s, stb_s, shb_s = rwkv7_branch(z_s, state_rwkv_shift[l], state_rwkv[l], *rw)
        om_s = memory_attention(qm_s * scale, cache_mem_k[l], cache_mem_v[l])
        xs = output_projection(xs, oa_s, ob_s, om_s, gate_s, w_out[l])
        wk_s.append(k_s)
        wv_s.append(v_s)
        st_s.append(stb_s)
        sh_s.append(shb_s)
    y_prompt = rms_norm(xp, norm_final)
    y_sample = rms_norm(xs, norm_final)
    return (y_prompt, y_sample,
            jnp.stack(wk_p), jnp.stack(wv_p), jnp.stack(st_p), jnp.stack(sh_p),
            jnp.stack(mk_p), jnp.stack(mv_p),
            jnp.stack(wk_s), jnp.stack(wv_s), jnp.stack(st_s), jnp.stack(sh_s))
```

```python
import functools

import jax
import jax.numpy as jnp
from jax import lax
from jax.experimental import pallas as pl
from jax.experimental.pallas import tpu as pltpu

F32 = jnp.float32
BF16 = jnp.bfloat16

D_MODEL = 1024
HEAD_DIM = 64
ATT_W = 384
RWKV_W = 384
MEM_W = 256
MIX_W = 1024
LORA_W = 64
SHIFT_W = 3 * RWKV_W + 2 * LORA_W
N_MEM = 256
ROPE_DIM = 16
ROPE_THETA = 500000.0
NORM_EPS = 1e-6
LNX_EPS = 64e-5
DILATIONS = (1, 4, 16)
BAND = 128
ATT_TILE = BAND * max(DILATIONS)
LANES = 128
NEG = -1e30
VMEM_LIMIT = 56 * 1024 * 1024


def _cparams(sem):
    return pltpu.CompilerParams(dimension_semantics=sem, vmem_limit_bytes=VMEM_LIMIT)


def _dot(a, b):
    return jnp.dot(a.astype(BF16), b.astype(BF16), preferred_element_type=F32)


def _dot_nt(a, b):
    return lax.dot_general(a.astype(BF16), b.astype(BF16), (((1,), (1,)), ((), ())),
                           preferred_element_type=F32)


def _dot_tn(a, b):
    return lax.dot_general(a.astype(BF16), b.astype(BF16), (((0,), (0,)), ((), ())),
                           preferred_element_type=F32)


def _dot_f32(a, b):
    return jnp.dot(a, b, preferred_element_type=F32, precision=lax.Precision.HIGHEST)


def _half0(shape):
    return (lax.broadcasted_iota(jnp.int32, shape, len(shape) - 1) & 64) == 0


def _proj_kernel(x_ref, pos_ref, g_ref, invf_ref, w_ref, q_ref, k_ref, v_ref, zb_ref, qm_ref, gate_ref,
                 *, split_pairs):
    x = x_ref[...]
    ms = jnp.mean(x * x, axis=-1, keepdims=True)
    h = ((x * lax.rsqrt(ms + NORM_EPS)) * g_ref[...]).astype(BF16)

    ang = pos_ref[...] * invf_ref[...]
    cos = jnp.cos(ang)
    sin = jnp.sin(ang)
    lane = lax.broadcasted_iota(jnp.int32, ang.shape, 1)
    second = (lane & 8) != 0
    sin_up = jnp.where(second, sin, 0.0)
    sin_dn = jnp.where(second, 0.0, -sin)

    def rope(t):
        return t * cos + pltpu.roll(t, 8, 1) * sin_up + pltpu.roll(t, LANES - 8, 1) * sin_dn

    def put(ref, p, val):
        if split_pairs:
            ref[p] = val
        else:
            ref[:, p * LANES:(p + 1) * LANES] = val

    for p in range(ATT_W // LANES):
        c0 = p * LANES
        put(q_ref, p, rope(jnp.dot(h, w_ref[:, c0:c0 + LANES], preferred_element_type=F32)) * 0.125)
        c0 = ATT_W + p * LANES
        put(k_ref, p, rope(jnp.dot(h, w_ref[:, c0:c0 + LANES], preferred_element_type=F32)))
        c0 = 2 * ATT_W + p * LANES
        put(v_ref, p, jnp.dot(h, w_ref[:, c0:c0 + LANES], preferred_element_type=F32))
    c0 = 3 * ATT_W
    zb_ref[...] = jnp.dot(h, w_ref[:, c0:c0 + SHIFT_W], preferred_element_type=F32)
    c0 += SHIFT_W
    qm_ref[...] = jnp.dot(h, w_ref[:, c0:c0 + MEM_W], preferred_element_type=F32) * 0.125
    c0 += MEM_W
    gate_ref[...] = jnp.dot(h, w_ref[:, c0:c0 + MIX_W], preferred_element_type=F32)


def _proj(x2d, pos, g, invf, w_bf16, tm, split_pairs):
    rows = x2d.shape[0]
    in_w = w_bf16.shape[1]
    if split_pairs:
        qkv_shape = jax.ShapeDtypeStruct((3, rows, LANES), F32)
        qkv_spec = pl.BlockSpec((3, tm, LANES), lambda i: (0, i, 0))
    else:
        qkv_shape = jax.ShapeDtypeStruct((rows, ATT_W), F32)
        qkv_spec = pl.BlockSpec((tm, ATT_W), lambda i: (i, 0))
    row = lambda w: pl.BlockSpec((tm, w), lambda i: (i, 0))
    full = lambda a, b: pl.BlockSpec((a, b), lambda i: (0, 0))
    return pl.pallas_call(
        functools.partial(_proj_kernel, split_pairs=split_pairs),
        grid=(rows // tm,),
        in_specs=[row(D_MODEL), row(1), full(1, D_MODEL), full(1, LANES), full(D_MODEL, in_w)],
        out_specs=[qkv_spec, qkv_spec, qkv_spec, row(SHIFT_W), row(MEM_W), row(MIX_W)],
        out_shape=[qkv_shape, qkv_shape, qkv_shape,
                   jax.ShapeDtypeStruct((rows, SHIFT_W), F32),
                   jax.ShapeDtypeStruct((rows, MEM_W), F32),
                   jax.ShapeDtypeStruct((rows, MIX_W), F32)],
        compiler_params=_cparams(("arbitrary",)),
        name="proj",
    )(x2d, pos, g, invf, w_bf16)


def _attn_unit(p, d, q_ref, kp_ref, kc_ref, vp_ref, vc_ref, qs, ps, bias, m_sc, l_sc, acc_sc):
    ld = lambda ref, st: ref[p, pl.ds(st, BAND, stride=d), :]
    q = ld(q_ref, qs)
    kcat = jnp.concatenate([ld(kp_ref, ps), ld(kc_ref, qs)], axis=0).astype(BF16)
    vcat = jnp.concatenate([ld(vp_ref, ps), ld(vc_ref, qs)], axis=0)
    h0 = _half0((BAND, LANES))
    h0k = _half0((2 * BAND, LANES))
    res, alpha = [], []
    for hh in range(2):
        mine = h0 if hh == 0 else jnp.logical_not(h0)
        mine_k = h0k if hh == 0 else jnp.logical_not(h0k)
        s = _dot_nt(jnp.where(mine, q, 0.0), kcat) + bias
        row = pl.ds(qs, BAND, stride=d)
        m_run = m_sc[2 * p + hh, row, :]
        m_new = jnp.maximum(m_run, jnp.max(s, axis=-1, keepdims=True))
        pexp = jnp.exp(s - jnp.concatenate([m_new, m_new], axis=1))
        res.append(_dot(pexp, jnp.where(mine_k, vcat, 1.0)))
        alpha.append(jnp.exp(m_run - m_new))
        m_sc[2 * p + hh, row, :] = m_new
    row = pl.ds(qs, BAND, stride=d)
    acc_sc[p, row, :] = acc_sc[p, row, :] * jnp.where(h0, alpha[0], alpha[1]) + jnp.where(h0, res[0], res[1])
    l_sc[p, row, :] = l_sc[p, row, :] * jnp.where(h0, alpha[1], alpha[0]) + jnp.where(h0, res[1], res[0])


def _attn_kernel(q_ref, kc_ref, kp_ref, vc_ref, vp_ref, o_ref, m_sc, l_sc, acc_sc):
    has_prev = pl.program_id(1) > 0
    m_sc[...] = jnp.full(m_sc.shape, NEG, F32)
    l_sc[...] = jnp.zeros(l_sc.shape, F32)
    acc_sc[...] = jnp.zeros(acc_sc.shape, F32)

    qi = lax.broadcasted_iota(jnp.int32, (BAND, 2 * BAND), 0)
    kj = lax.broadcasted_iota(jnp.int32, (BAND, 2 * BAND), 1)
    in_cur = (kj >= BAND) & (kj - BAND <= qi)
    in_prev = (kj < BAND) & (kj >= qi)
    bias_full = jnp.where(in_cur | in_prev, 0.0, NEG).astype(F32)
    bias_first = jnp.where(in_cur | (in_prev & has_prev), 0.0, NEG).astype(F32)

    npairs = q_ref.shape[0]
    for d in DILATIONS:
        span = BAND * d
        shift = d.bit_length() - 1

        def first_span(r, carry, d=d, span=span):
            for p in range(npairs):
                _attn_unit(p, d, q_ref, kp_ref, kc_ref, vp_ref, vc_ref, r, ATT_TILE - span + r,
                           bias_first, m_sc, l_sc, acc_sc)
            return carry

        def later_span(u, carry, d=d, span=span, shift=shift):
            qs = (u >> shift) * span + (u & (d - 1))
            for p in range(npairs):
                _attn_unit(p, d, q_ref, kc_ref, kc_ref, vc_ref, vc_ref, qs, qs - span,
                           bias_full, m_sc, l_sc, acc_sc)
            return carry

        lax.fori_loop(0, d, first_span, 0)
        if d < ATT_TILE // BAND:
            lax.fori_loop(d, ATT_TILE // BAND, later_span, 0)

    def finish(i, carry):
        row = pl.ds(pl.multiple_of(i * BAND, BAND), BAND)
        for p in range(npairs):
            denom = pltpu.roll(l_sc[p, row, :], HEAD_DIM, 1)
            o_ref[p, row, :] = acc_sc[p, row, :] / denom
        return carry

    lax.fori_loop(0, ATT_TILE // BAND, finish, 0)


def _attn_prompt(q3, k3, v3, batch, seq):
    nt = seq // ATT_TILE
    cur = pl.BlockSpec((3, ATT_TILE, LANES), lambda b, i: (0, b * nt + i, 0))
    prev = pl.BlockSpec((3, ATT_TILE, LANES), lambda b, i: (0, b * nt + jnp.maximum(i - 1, 0), 0))
    return pl.pallas_call(
        _attn_kernel,
        grid=(batch, nt),
        in_specs=[cur, cur, prev, cur, prev],
        out_specs=cur,
        out_shape=jax.ShapeDtypeStruct(q3.shape, F32),
        scratch_shapes=[pltpu.VMEM((6, ATT_TILE, LANES), F32),
                        pltpu.VMEM((3, ATT_TILE, LANES), F32),
                        pltpu.VMEM((3, ATT_TILE, LANES), F32)],
        compiler_params=_cparams(("arbitrary", "arbitrary")),
        name="attn_prompt",
    )(q3, k3, k3, v3, v3)


def _attn_sample_kernel(q_ref, kn_ref, vn_ref, kc_ref, vc_ref, o_ref, *, n_past, t_new):
    nh = ATT_W // HEAD_DIM
    q = q_ref[0]
    lane = lax.broadcasted_iota(jnp.int32, q.shape, 1)
    head_of_lane = lane >> 6
    qst = jnp.concatenate([jnp.where(head_of_lane == h, q, 0.0) for h in range(nh)], axis=0)

    def count(delta):
        c = jnp.zeros(delta.shape, F32)
        for d in DILATIONS:
            ok = (delta >= 0) & (delta <= BAND * d) & ((delta & (d - 1)) == 0)
            c = c + jnp.where(ok, 1.0, 0.0)
        return c

    rows = nh * t_new
    t_past = lax.broadcasted_iota(jnp.int32, (rows, n_past), 0) & (t_new - 1)
    cnt_past = count(n_past + t_past - lax.broadcasted_iota(jnp.int32, (rows, n_past), 1))
    t_n = lax.broadcasted_iota(jnp.int32, (rows, t_new), 0) & (t_new - 1)
    cnt_new = count(t_n - lax.broadcasted_iota(jnp.int32, (rows, t_new), 1))

    s_past = jnp.where(cnt_past > 0, _dot_nt(qst, kc_ref[0]), NEG)
    s_new = jnp.where(cnt_new > 0, _dot_nt(qst, kn_ref[0]), NEG)
    m = jnp.maximum(jnp.max(s_past, axis=-1, keepdims=True), jnp.max(s_new, axis=-1, keepdims=True))
    p_past = cnt_past * jnp.exp(s_past - m)
    p_new = cnt_new * jnp.exp(s_new - m)
    l = jnp.sum(p_past, axis=-1, keepdims=True) + jnp.sum(p_new, axis=-1, keepdims=True)
    o = (_dot(p_past, vc_ref[0]) + _dot(p_new, vn_ref[0])) / l
    out = jnp.zeros(q.shape, F32)
    for h in range(nh):
        out = jnp.where(head_of_lane == h, o[h * t_new:(h + 1) * t_new, :], out)
    o_ref[0] = out


def _attn_sample(q, k_new, v_new, k_cache, v_cache):
    db, t_new, _ = q.shape
    n_past = k_cache.shape[1]
    new = pl.BlockSpec((1, t_new, ATT_W), lambda b: (b, 0, 0))
    cache = pl.BlockSpec((1, n_past, ATT_W), lambda b: (b, 0, 0))
    return pl.pallas_call(
        functools.partial(_attn_sample_kernel, n_past=n_past, t_new=t_new),
        grid=(db,),
        in_specs=[new, new, new, cache, cache],
        out_specs=new,
        out_shape=jax.ShapeDtypeStruct(q.shape, F32),
        compiler_params=_cparams(("arbitrary",)),
        name="attn_sample",
    )(q, k_new, v_new, k_cache, v_cache)


def _seg_sum(x, h0):
    s0 = jnp.sum(jnp.where(h0, x, 0.0), axis=-1, keepdims=True)
    s1 = jnp.sum(jnp.where(h0, 0.0, x), axis=-1, keepdims=True)
    return jnp.where(h0, s0, s1)


def _rwkv_kernel(z_ref, sh0_ref, st0_ref, mu_ref, w0_ref, a0_ref, w2a2_ref, kk_ref, ka_ref, rk_ref,
                 lng_ref, lnb_ref, o_ref, st_ref, st_sc, prev_sc, *, chunk, n_valid):
    c = pl.program_id(1)
    C = chunk
    npairs = RWKV_W // LANES

    @pl.when(c == 0)
    def _():
        st_sc[...] = st0_ref[0]
        prev_sc[...] = sh0_ref[0]

    z = z_ref[0]
    rowid = lax.broadcasted_iota(jnp.int32, (C, 1), 0)
    z_prev = jnp.where(rowid == 0, prev_sc[...], pltpu.roll(z, 1, 0))
    prev_sc[...] = z_ref[0, pl.ds(C - 1, 1), :]
    zs = z + (z_prev - z) * mu_ref[...]
    if n_valid < C:
        zs = jnp.where(rowid < n_valid, zs, 0.0)

    lat = zs[:, 3 * RWKV_W:]
    lat = jnp.where(_half0(lat.shape), jnp.tanh(lat), lat)
    lora = _dot_f32(lat, w2a2_ref[...])
    w = -jax.nn.softplus(-(w0_ref[...] + lora[:, :RWKV_W])) - 0.5
    lw = -jnp.exp(w)
    if n_valid < C:
        lw = jnp.where(rowid < n_valid, lw, 0.0)
    a = jax.nn.sigmoid(a0_ref[...] + lora[:, RWKV_W:])

    ti = lax.broadcasted_iota(jnp.int32, (C, C), 0)
    tj = lax.broadcasted_iota(jnp.int32, (C, C), 1)
    cum = _dot_f32(jnp.where(tj <= ti, 1.0, 0.0).astype(F32), lw)

    si = lax.broadcasted_iota(jnp.int32, (2 * C, 2 * C), 0)
    sj = lax.broadcasted_iota(jnp.int32, (2 * C, 2 * C), 1)
    same_head = (si >= C) == (sj >= C)
    ti2 = si & (C - 1)
    tj2 = sj & (C - 1)
    incl = same_head & (tj2 <= ti2)
    strict = same_head & (tj2 < ti2)
    eye = jnp.where(si == sj, 1.0, 0.0).astype(F32)

    for p in range(npairs):
        sl = slice(p * LANES, (p + 1) * LANES)
        h0 = _half0((C, LANES))
        r = zs[:, sl]
        k = zs[:, RWKV_W + p * LANES:RWKV_W + (p + 1) * LANES]
        v = zs[:, 2 * RWKV_W + p * LANES:2 * RWKV_W + (p + 1) * LANES]
        ap = a[:, sl]
        kk = k * kk_ref[:, sl]
        kk = kk / jnp.maximum(jnp.sqrt(_seg_sum(kk * kk, h0)), 1e-12)
        k = k * (1.0 + (ap - 1.0) * ka_ref[:, sl])
        b = kk * ap

        L = cum[:, sl]
        e_in = jnp.exp(L)
        e_out = jnp.exp(-L)
        e_end = jnp.exp(L[C - 1:C, :] - L)
        rt = r * e_in
        kkh = kk * jnp.exp(L - lw[:, sl])
        kt = k * e_out
        nbt = -(b * e_out)
        kg = k * e_end
        nbg = -(b * e_end)

        def stack(x):
            return jnp.concatenate([jnp.where(h0, x, 0.0), jnp.where(h0, 0.0, x)], axis=0)

        left = jnp.concatenate([stack(rt), stack(kkh)], axis=0)
        right = jnp.concatenate([stack(kt), stack(nbt)], axis=0)
        aa = _dot_nt(left, right)
        a_top = jnp.concatenate([jnp.where(incl, aa[:2 * C, :2 * C], 0.0),
                                 jnp.where(incl, aa[:2 * C, 2 * C:], 0.0)], axis=1)
        a_kk = jnp.where(strict, aa[2 * C:, :2 * C], 0.0)
        n_kb = jnp.where(strict, aa[2 * C:, 2 * C:], 0.0)

        t_inv = eye + jnp.where((ti2 >> 1) == (tj2 >> 1), n_kb, 0.0)
        s = 2
        while s < C:
            sh = s.bit_length() - 1
            off = ((ti2 >> (sh + 1)) == (tj2 >> (sh + 1))) & (((ti2 >> sh) & 1) == 1) & (((tj2 >> sh) & 1) == 0)
            t_inv = t_inv + _dot(_dot(t_inv, jnp.where(off, n_kb, 0.0)), t_inv)
            s *= 2

        st = st_sc[p]
        ls = _dot_nt(left, st)
        v2 = stack(v)
        u2 = _dot(t_inv, ls[2 * C:] + _dot(a_kk, v2))
        vu = jnp.concatenate([v2, u2], axis=0)
        y2 = ls[:2 * C] + _dot(a_top, vu)
        y = y2[:C] + y2[C:]
        st_sc[p] = st * jnp.exp(L[C - 1:C, :]) + _dot_tn(vu, jnp.concatenate([stack(kg), stack(nbg)], axis=0))

        mean = _seg_sum(y, h0) * (1.0 / HEAD_DIM)
        yc = y - mean
        var = _seg_sum(yc * yc, h0) * (1.0 / HEAD_DIM)
        yn = yc * lax.rsqrt(var + LNX_EPS) * lng_ref[:, sl] + lnb_ref[:, sl]
        bonus = _seg_sum(r * k * rk_ref[:, sl], h0) * v
        o_ref[0, :, sl] = yn + bonus

    @pl.when(c == pl.num_programs(1) - 1)
    def _():
        st_ref[0] = st_sc[...]


def _rwkv(z, shift0, st0, mu, w0, a0, w2a2, k_k, k_a, r_k, ln_g, ln_b, chunk):
    b, n_valid, _ = z.shape
    if n_valid % chunk:
        assert n_valid < chunk
        z = jnp.pad(z, ((0, 0), (0, chunk - n_valid), (0, 0)))
    s = z.shape[1]
    n_valid = min(n_valid, chunk)
    vec = lambda w: pl.BlockSpec((1, w), lambda i, j: (0, 0))
    st_spec = pl.BlockSpec((1, 3, LANES, LANES), lambda i, j: (i, 0, 0, 0))
    return pl.pallas_call(
        functools.partial(_rwkv_kernel, chunk=chunk, n_valid=n_valid),
        grid=(b, s // chunk),
        in_specs=[pl.BlockSpec((1, chunk, SHIFT_W), lambda i, j: (i, j, 0)),
                  pl.BlockSpec((1, 1, SHIFT_W), lambda i, j: (i, 0, 0)),
                  st_spec,
                  vec(SHIFT_W), vec(RWKV_W), vec(RWKV_W),
                  pl.BlockSpec((LANES, 2 * RWKV_W), lambda i, j: (0, 0)),
                  vec(RWKV_W), vec(RWKV_W), vec(RWKV_W), vec(RWKV_W), vec(RWKV_W)],
        out_specs=[pl.BlockSpec((1, chunk, RWKV_W), lambda i, j: (i, j, 0)), st_spec],
        out_shape=[jax.ShapeDtypeStruct((b, s, RWKV_W), F32),
                   jax.ShapeDtypeStruct((b, 3, LANES, LANES), F32)],
        scratch_shapes=[pltpu.VMEM((3, LANES, LANES), F32), pltpu.VMEM((1, SHIFT_W), F32)],
        compiler_params=_cparams(("arbitrary", "arbitrary")),
        name="rwkv",
    )(z, shift0, st0, mu, w0, a0, w2a2, k_k, k_a, r_k, ln_g, ln_b)


def _state_to_pairs(st):
    b = st.shape[0]
    st = st.reshape(b, 3, 2, HEAD_DIM, HEAD_DIM)
    zero = jnp.zeros_like(st[:, :, 0])
    top = jnp.concatenate([st[:, :, 0], zero], axis=-1)
    bot = jnp.concatenate([zero, st[:, :, 1]], axis=-1)
    return jnp.concatenate([top, bot], axis=-2)


def _pairs_to_state(sp):
    b = sp.shape[0]
    return jnp.stack([sp[:, :, :HEAD_DIM, :HEAD_DIM], sp[:, :, HEAD_DIM:, HEAD_DIM:]], axis=2).reshape(
        b, 6, HEAD_DIM, HEAD_DIM)


def _memkv_kernel(x_ref, g_ref, w_ref, k_ref, v_ref):
    x = x_ref[...]
    ms = jnp.mean(x * x, axis=-1, keepdims=True)
    h = ((x * lax.rsqrt(ms + NORM_EPS)) * g_ref[...]).astype(BF16)
    k_ref[...] = jnp.dot(h, w_ref[:, :MEM_W], preferred_element_type=F32)
    v_ref[...] = jnp.dot(h, w_ref[:, MEM_W:], preferred_element_type=F32)


def _memkv(mem2d, g, w_bf16):
    rows = mem2d.shape[0]
    out = jax.ShapeDtypeStruct((rows, MEM_W), F32)
    return pl.pallas_call(
        _memkv_kernel,
        out_shape=[out, out],
        compiler_params=pltpu.CompilerParams(vmem_limit_bytes=VMEM_LIMIT),
        name="memkv",
    )(mem2d, g, w_bf16)


def _memattn_kernel(q_ref, mk_ref, mv_ref, o_ref):
    q = q_ref[0]
    mk = mk_ref[0].astype(BF16)
    mv = mv_ref[0].astype(BF16)
    head_of_lane = lax.broadcasted_iota(jnp.int32, q.shape, 1) >> 6
    out = jnp.zeros(q.shape, F32)
    for h in range(MEM_W // HEAD_DIM):
        mine = head_of_lane == h
        s = _dot_nt(jnp.where(mine, q, 0.0), mk)
        pexp = jnp.exp(s - jnp.max(s, axis=-1, keepdims=True))
        l = jnp.sum(pexp, axis=-1, keepdims=True)
        out = jnp.where(mine, _dot(pexp, mv) / l, out)
    o_ref[0] = out


def _memattn(q, mk, mv, tm):
    b, s, _ = q.shape
    qspec = pl.BlockSpec((1, tm, MEM_W), lambda i, j: (i, j, 0))
    mspec = pl.BlockSpec((1, N_MEM, MEM_W), lambda i, j: (i, 0, 0))
    return pl.pallas_call(
        _memattn_kernel,
        grid=(b, s // tm),
        in_specs=[qspec, mspec, mspec],
        out_specs=qspec,
        out_shape=jax.ShapeDtypeStruct(q.shape, F32),
        compiler_params=_cparams(("arbitrary", "arbitrary")),
        name="memattn",
    )(q, mk, mv)


def _out_kernel(oa_ref, ob_ref, om_ref, gate_ref, x_ref, w_ref, g_ref, y_ref, *, split_pairs):
    acc = x_ref[...]

    def add(acc, o, c0):
        width = o.shape[1]
        gate = gate_ref[:, c0:c0 + width]
        return acc + _dot(o * (gate * jax.nn.sigmoid(gate)), w_ref[c0:c0 + width, :])

    if split_pairs:
        for p in range(ATT_W // LANES):
            acc = add(acc, oa_ref[p], p * LANES)
    else:
        acc = add(acc, oa_ref[...], 0)
    acc = add(acc, ob_ref[...], ATT_W)
    acc = add(acc, om_ref[...], ATT_W + RWKV_W)
    ms = jnp.mean(acc * acc, axis=-1, keepdims=True)
    y_ref[...] = (acc * lax.rsqrt(ms + NORM_EPS)) * g_ref[...]


def _out(oa, ob, om, gate, x2d, w_bf16, g, tm, split_pairs):
    rows = x2d.shape[0]
    row = lambda w: pl.BlockSpec((tm, w), lambda i: (i, 0))
    oa_spec = pl.BlockSpec((3, tm, LANES), lambda i: (0, i, 0)) if split_pairs else row(ATT_W)
    return pl.pallas_call(
        functools.partial(_out_kernel, split_pairs=split_pairs),
        grid=(rows // tm,),
        in_specs=[oa_spec, row(RWKV_W), row(MEM_W), row(MIX_W), row(D_MODEL),
                  pl.BlockSpec((MIX_W, D_MODEL), lambda i: (0, 0)),
                  pl.BlockSpec((1, D_MODEL), lambda i: (0, 0))],
        out_specs=row(D_MODEL),
        out_shape=jax.ShapeDtypeStruct((rows, D_MODEL), F32),
        compiler_params=_cparams(("arbitrary",)),
        name="outproj",
    )(oa, ob, om, gate, x2d, w_bf16, g)


def kernel(x_prompt, x_sample, cache_win_k, cache_win_v, state_rwkv, state_rwkv_shift, cache_mem_k, cache_mem_v, mem_prompt, norm_in, w_in, rwkv_mu, rwkv_w0, rwkv_w2, rwkv_a0, rwkv_a2, rwkv_k_k, rwkv_k_a, rwkv_r_k, rwkv_lnx_g, rwkv_lnx_b, norm_mem, w_mem_kv, w_out, norm_final):
    B, S, _ = x_prompt.shape
    DB, T, _ = x_sample.shape
    depth = w_in.shape[0]
    assert depth == 1 and S % ATT_TILE == 0 and cache_win_k.shape[2] == max(DILATIONS) * BAND
    l = 0
    past_len = S

    half = ROPE_DIM // 2
    inv_freq = ROPE_THETA ** (-jnp.arange(half, dtype=F32) / half)
    lane = jnp.arange(LANES)
    invf = jnp.where((lane % HEAD_DIM) < ROPE_DIM, inv_freq[lane % half], 0.0).reshape(1, LANES)

    row = lambda t: t.reshape(1, -1)
    w_in_b = w_in[l].astype(BF16)
    w_out_b = w_out[l].astype(BF16)
    w_kv_b = w_mem_kv[l].astype(BF16)
    zero = jnp.zeros((LORA_W, RWKV_W), F32)
    w2a2 = jnp.concatenate([jnp.concatenate([rwkv_w2[l], zero], axis=1),
                            jnp.concatenate([zero, rwkv_a2[l]], axis=1)], axis=0)
    rw = (row(rwkv_mu[l]), row(rwkv_w0[l]), row(rwkv_a0[l]), w2a2, row(rwkv_k_k[l]), row(rwkv_k_a[l]),
          row(rwkv_r_k[l]), row(rwkv_lnx_g[l]), row(rwkv_lnx_b[l]))

    xp = x_prompt.reshape(B * S, D_MODEL)
    pos_p = jnp.tile(jnp.arange(S, dtype=F32), B).reshape(B * S, 1)
    q3, k3, v3, zb, qm, gate = _proj(xp, pos_p, row(norm_in[l]), invf, w_in_b, 512, True)
    oa3 = _attn_prompt(q3, k3, v3, B, S)
    ob, st_p = _rwkv(zb.reshape(B, S, SHIFT_W), jnp.zeros((B, 1, SHIFT_W), F32),
                     jnp.zeros((B, 3, LANES, LANES), F32), *rw, chunk=64)
    mk, mv = _memkv(mem_prompt.reshape(B * N_MEM, D_MODEL), row(norm_mem[l]), w_kv_b)
    mk = mk.reshape(B, N_MEM, MEM_W)
    mv = mv.reshape(B, N_MEM, MEM_W)
    om = _memattn(qm.reshape(B, S, MEM_W), mk, mv, 1024)
    y_p = _out(oa3, ob.reshape(B * S, RWKV_W), om.reshape(B * S, MEM_W), gate, xp, w_out_b,
               row(norm_final), 512, True)

    win = min(max(DILATIONS) * BAND, S)
    tail = lambda t3: jnp.transpose(t3.reshape(3, B, S, LANES)[:, :, S - win:], (1, 2, 0, 3)).reshape(
        1, B, win, ATT_W // HEAD_DIM, HEAD_DIM)
    heads = lambda t, n: t.reshape(1, t.shape[0], t.shape[1], n, HEAD_DIM)

    xs = x_sample.reshape(DB * T, D_MODEL)
    pos_s = jnp.tile(past_len + jnp.arange(T, dtype=F32), DB).reshape(DB * T, 1)
    qs, ks, vs, zbs, qms, gates = _proj(xs, pos_s, row(norm_in[l]), invf, w_in_b, DB * T, False)
    n_past = cache_win_k.shape[2]
    oas = _attn_sample(qs.reshape(DB, T, ATT_W), ks.reshape(DB, T, ATT_W), vs.reshape(DB, T, ATT_W),
                       cache_win_k[l].reshape(DB, n_past, ATT_W), cache_win_v[l].reshape(DB, n_past, ATT_W))
    zbs3 = zbs.reshape(DB, T, SHIFT_W)
    obs, st_s = _rwkv(zbs3, state_rwkv_shift[l].reshape(DB, 1, SHIFT_W), _state_to_pairs(state_rwkv[l]),
                      *rw, chunk=64)
    obs = obs[:, :T]
    oms = _memattn(qms.reshape(DB, T, MEM_W), cache_mem_k[l].reshape(DB, N_MEM, MEM_W),
                   cache_mem_v[l].reshape(DB, N_MEM, MEM_W), T)
    y_s = _out(oas.reshape(DB * T, ATT_W), obs.reshape(DB * T, RWKV_W), oms.reshape(DB * T, MEM_W), gates, xs,
               w_out_b, row(norm_final), DB * T, False)

    return (y_p.reshape(B, S, D_MODEL), y_s.reshape(DB, T, D_MODEL),
            tail(k3), tail(v3),
            _pairs_to_state(st_p)[None], zb.reshape(B, S, SHIFT_W)[:, -1][None],
            heads(mk, MEM_W // HEAD_DIM), heads(mv, MEM_W // HEAD_DIM),
            heads(ks.reshape(DB, T, ATT_W), ATT_W // HEAD_DIM), heads(vs.reshape(DB, T, ATT_W), ATT_W // HEAD_DIM),
            _pairs_to_state(st_s)[None], zbs3[:, -1][None])
```

```python
import functools

import jax
import jax.numpy as jnp
from jax import lax
from jax.experimental import pallas as pl
from jax.experimental.pallas import tpu as pltpu

F32 = jnp.float32
BF16 = jnp.bfloat16

D_MODEL = 1024
HEAD_DIM = 64
ATT_W = 384
RWKV_W = 384
MEM_W = 256
MIX_W = 1024
LORA_W = 64
SHIFT_W = 3 * RWKV_W + 2 * LORA_W
N_MEM = 256
ROPE_DIM = 16
ROPE_THETA = 500000.0
NORM_EPS = 1e-6
LNX_EPS = 64e-5
DILATIONS = (1, 4, 16)
BAND = 128
ATT_TILE = BAND * max(DILATIONS)
LANES = 128
RWKV_CHUNK = 64
NEG = -1e30
VMEM_LIMIT = 56 * 1024 * 1024


def _cparams(sem):
    return pltpu.CompilerParams(dimension_semantics=sem, vmem_limit_bytes=VMEM_LIMIT)


def _dot(a, b):
    return jnp.dot(a.astype(BF16), b.astype(BF16), preferred_element_type=F32)


def _dot_nt(a, b):
    return lax.dot_general(a.astype(BF16), b.astype(BF16), (((1,), (1,)), ((), ())),
                           preferred_element_type=F32)


def _dot_tn(a, b):
    return lax.dot_general(a.astype(BF16), b.astype(BF16), (((0,), (0,)), ((), ())),
                           preferred_element_type=F32)


def _dot_f32(a, b):
    return jnp.dot(a, b, preferred_element_type=F32, precision=lax.Precision.HIGHEST)


def _half0(shape):
    return (lax.broadcasted_iota(jnp.int32, shape, len(shape) - 1) & 64) == 0


def _proj_kernel(x_ref, pos_ref, g_ref, invf_ref, w_ref, q_ref, k_ref, v_ref, zb_ref, qm_ref, gate_ref,
                 *, split_pairs):
    x = x_ref[...]
    ms = jnp.mean(x * x, axis=-1, keepdims=True)
    h = ((x * lax.rsqrt(ms + NORM_EPS)) * g_ref[...]).astype(BF16)

    ang = pos_ref[...] * invf_ref[...]
    cos = jnp.cos(ang)
    sin = jnp.sin(ang)
    lane = lax.broadcasted_iota(jnp.int32, ang.shape, 1)
    second = (lane & 8) != 0
    sin_up = jnp.where(second, sin, 0.0)
    sin_dn = jnp.where(second, 0.0, -sin)

    def rope(t):
        return t * cos + pltpu.roll(t, 8, 1) * sin_up + pltpu.roll(t, LANES - 8, 1) * sin_dn

    def put(ref, p, val):
        if split_pairs:
            ref[p] = val
        else:
            ref[:, p * LANES:(p + 1) * LANES] = val

    for p in range(ATT_W // LANES):
        c0 = p * LANES
        put(q_ref, p, rope(jnp.dot(h, w_ref[:, c0:c0 + LANES], preferred_element_type=F32)) * 0.125)
        c0 = ATT_W + p * LANES
        put(k_ref, p, rope(jnp.dot(h, w_ref[:, c0:c0 + LANES], preferred_element_type=F32)))
        c0 = 2 * ATT_W + p * LANES
        put(v_ref, p, jnp.dot(h, w_ref[:, c0:c0 + LANES], preferred_element_type=F32))
    c0 = 3 * ATT_W
    zb_ref[...] = jnp.dot(h, w_ref[:, c0:c0 + SHIFT_W], preferred_element_type=F32)
    c0 += SHIFT_W
    qm_ref[...] = jnp.dot(h, w_ref[:, c0:c0 + MEM_W], preferred_element_type=F32) * 0.125
    c0 += MEM_W
    gate_ref[...] = jnp.dot(h, w_ref[:, c0:c0 + MIX_W], preferred_element_type=F32)


def _proj(x2d, pos, g, invf, w_bf16, tm, split_pairs):
    rows = x2d.shape[0]
    in_w = w_bf16.shape[1]
    if split_pairs:
        qkv_shape = jax.ShapeDtypeStruct((3, rows, LANES), F32)
        qkv_spec = pl.BlockSpec((3, tm, LANES), lambda i: (0, i, 0))
    else:
        qkv_shape = jax.ShapeDtypeStruct((rows, ATT_W), F32)
        qkv_spec = pl.BlockSpec((tm, ATT_W), lambda i: (i, 0))
    row = lambda w: pl.BlockSpec((tm, w), lambda i: (i, 0))
    full = lambda a, b: pl.BlockSpec((a, b), lambda i: (0, 0))
    return pl.pallas_call(
        functools.partial(_proj_kernel, split_pairs=split_pairs),
        grid=(rows // tm,),
        in_specs=[row(D_MODEL), row(1), full(1, D_MODEL), full(1, LANES), full(D_MODEL, in_w)],
        out_specs=[qkv_spec, qkv_spec, qkv_spec, row(SHIFT_W), row(MEM_W), row(MIX_W)],
        out_shape=[qkv_shape, qkv_shape, qkv_shape,
                   jax.ShapeDtypeStruct((rows, SHIFT_W), F32),
                   jax.ShapeDtypeStruct((rows, MEM_W), F32),
                   jax.ShapeDtypeStruct((rows, MIX_W), F32)],
        compiler_params=_cparams(("arbitrary",)),
        name="proj",
    )(x2d, pos, g, invf, w_bf16)


def _attn_unit(npairs, d, q_ref, kp_ref, kc_ref, vp_ref, vc_ref, qs, ps, bias, m_sc, l_sc, acc_sc):
    row = pl.ds(qs, BAND, stride=d)
    prow = pl.ds(ps, BAND, stride=d)
    h0 = _half0((BAND, LANES))
    h0k = _half0((2 * BAND, LANES))
    heads = [(p, hh) for p in range(npairs) for hh in range(2)]
    q = [q_ref[p, row, :] for p in range(npairs)]
    kcat = [jnp.concatenate([kp_ref[p, prow, :], kc_ref[p, row, :]], axis=0).astype(BF16) for p in range(npairs)]
    vcat = [jnp.concatenate([vp_ref[p, prow, :], vc_ref[p, row, :]], axis=0) for p in range(npairs)]
    m_run = [m_sc[2 * p + hh, row, :] for p, hh in heads]
    acc_run = [acc_sc[p, row, :] for p in range(npairs)]
    l_run = [l_sc[p, row, :] for p in range(npairs)]
    s = [_dot_nt(jnp.where(h0 if hh == 0 else jnp.logical_not(h0), q[p], 0.0), kcat[p]) + bias
         for p, hh in heads]
    m_new = [jnp.maximum(m_run[i], jnp.max(s[i], axis=-1, keepdims=True)) for i in range(len(heads))]
    pexp = [jnp.exp(s[i] - jnp.concatenate([m_new[i], m_new[i]], axis=1)).astype(BF16) for i in range(len(heads))]
    alpha = [jnp.exp(m_run[i] - m_new[i]) for i in range(len(heads))]
    res = [jnp.dot(pexp[i], jnp.where(h0k if hh == 0 else jnp.logical_not(h0k), vcat[p], 1.0).astype(BF16),
                   preferred_element_type=F32) for i, (p, hh) in enumerate(heads)]
    for i, (p, hh) in enumerate(heads):
        m_sc[2 * p + hh, row, :] = m_new[i]
    for p in range(npairs):
        a0, a1, r0, r1 = alpha[2 * p], alpha[2 * p + 1], res[2 * p], res[2 * p + 1]
        acc_sc[p, row, :] = acc_run[p] * jnp.where(h0, a0, a1) + jnp.where(h0, r0, r1)
        l_sc[p, row, :] = l_run[p] * jnp.where(h0, a1, a0) + jnp.where(h0, r1, r0)


def _attn_kernel(q_ref, kc_ref, kp_ref, vc_ref, vp_ref, o_ref, m_sc, l_sc, acc_sc):
    has_prev = pl.program_id(1) > 0
    m_sc[...] = jnp.full(m_sc.shape, NEG, F32)
    l_sc[...] = jnp.zeros(l_sc.shape, F32)
    acc_sc[...] = jnp.zeros(acc_sc.shape, F32)

    qi = lax.broadcasted_iota(jnp.int32, (BAND, 2 * BAND), 0)
    kj = lax.broadcasted_iota(jnp.int32, (BAND, 2 * BAND), 1)
    in_cur = (kj >= BAND) & (kj - BAND <= qi)
    in_prev = (kj < BAND) & (kj >= qi)
    bias_full = jnp.where(in_cur | in_prev, 0.0, NEG).astype(F32)
    bias_first = jnp.where(in_cur | (in_prev & has_prev), 0.0, NEG).astype(F32)

    npairs = q_ref.shape[0]
    for d in DILATIONS:
        span = BAND * d
        shift = d.bit_length() - 1

        def first_span(r, carry, d=d, span=span):
            _attn_unit(npairs, d, q_ref, kp_ref, kc_ref, vp_ref, vc_ref, r, ATT_TILE - span + r,
                       bias_first, m_sc, l_sc, acc_sc)
            return carry

        def later_span(u, carry, d=d, span=span, shift=shift):
            qs = (u >> shift) * span + (u & (d - 1))
            _attn_unit(npairs, d, q_ref, kc_ref, kc_ref, vc_ref, vc_ref, qs, qs - span,
                       bias_full, m_sc, l_sc, acc_sc)
            return carry

        lax.fori_loop(0, d, first_span, 0)
        if d < ATT_TILE // BAND:
            lax.fori_loop(d, ATT_TILE // BAND, later_span, 0)

    def finish(i, carry):
        row = pl.ds(pl.multiple_of(i * BAND, BAND), BAND)
        for p in range(npairs):
            denom = pltpu.roll(l_sc[p, row, :], HEAD_DIM, 1)
            o_ref[p, row, :] = acc_sc[p, row, :] / denom
        return carry

    lax.fori_loop(0, ATT_TILE // BAND, finish, 0)


def _attn_prompt(q3, k3, v3, batch, seq):
    nt = seq // ATT_TILE
    cur = pl.BlockSpec((3, ATT_TILE, LANES), lambda b, i: (0, b * nt + i, 0))
    prev = pl.BlockSpec((3, ATT_TILE, LANES), lambda b, i: (0, b * nt + jnp.maximum(i - 1, 0), 0))
    return pl.pallas_call(
        _attn_kernel,
        grid=(batch, nt),
        in_specs=[cur, cur, prev, cur, prev],
        out_specs=cur,
        out_shape=jax.ShapeDtypeStruct(q3.shape, F32),
        scratch_shapes=[pltpu.VMEM((6, ATT_TILE, LANES), F32),
                        pltpu.VMEM((3, ATT_TILE, LANES), F32),
                        pltpu.VMEM((3, ATT_TILE, LANES), F32)],
        compiler_params=_cparams(("arbitrary", "arbitrary")),
        name="attn_prompt",
    )(q3, k3, k3, v3, v3)


def _attn_sample_kernel(q_ref, kn_ref, vn_ref, kc_ref, vc_ref, o_ref, *, n_past, t_new):
    nh = ATT_W // HEAD_DIM
    q = q_ref[0]
    lane = lax.broadcasted_iota(jnp.int32, q.shape, 1)
    head_of_lane = lane >> 6
    qst = jnp.concatenate([jnp.where(head_of_lane == h, q, 0.0) for h in range(nh)], axis=0)

    def count(delta):
        c = jnp.zeros(delta.shape, F32)
        for d in DILATIONS:
            ok = (delta >= 0) & (delta <= BAND * d) & ((delta & (d - 1)) == 0)
            c = c + jnp.where(ok, 1.0, 0.0)
        return c

    rows = nh * t_new
    t_past = lax.broadcasted_iota(jnp.int32, (rows, n_past), 0) & (t_new - 1)
    cnt_past = count(n_past + t_past - lax.broadcasted_iota(jnp.int32, (rows, n_past), 1))
    t_n = lax.broadcasted_iota(jnp.int32, (rows, t_new), 0) & (t_new - 1)
    cnt_new = count(t_n - lax.broadcasted_iota(jnp.int32, (rows, t_new), 1))

    s_past = jnp.where(cnt_past > 0, _dot_nt(qst, kc_ref[0]), NEG)
    s_new = jnp.where(cnt_new > 0, _dot_nt(qst, kn_ref[0]), NEG)
    m = jnp.maximum(jnp.max(s_past, axis=-1, keepdims=True), jnp.max(s_new, axis=-1, keepdims=True))
    p_past = cnt_past * jnp.exp(s_past - m)
    p_new = cnt_new * jnp.exp(s_new - m)
    l = jnp.sum(p_past, axis=-1, keepdims=True) + jnp.sum(p_new, axis=-1, keepdims=True)
    o = (_dot(p_past, vc_ref[0]) + _dot(p_new, vn_ref[0])) / l
    out = jnp.zeros(q.shape, F32)
    for h in range(nh):
        out = jnp.where(head_of_lane == h, o[h * t_new:(h + 1) * t_new, :], out)
    o_ref[0] = out


def _attn_sample(q, k_new, v_new, k_cache, v_cache):
    db, t_new, _ = q.shape
    n_past = k_cache.shape[1]
    new = pl.BlockSpec((1, t_new, ATT_W), lambda b: (b, 0, 0))
    cache = pl.BlockSpec((1, n_past, ATT_W), lambda b: (b, 0, 0))
    return pl.pallas_call(
        functools.partial(_attn_sample_kernel, n_past=n_past, t_new=t_new),
        grid=(db,),
        in_specs=[new, new, new, cache, cache],
        out_specs=new,
        out_shape=jax.ShapeDtypeStruct(q.shape, F32),
        compiler_params=_cparams(("arbitrary",)),
        name="attn_sample",
    )(q, k_new, v_new, k_cache, v_cache)


def _seg_sum(x, h0):
    s0 = jnp.sum(jnp.where(h0, x, 0.0), axis=-1, keepdims=True)
    s1 = jnp.sum(jnp.where(h0, 0.0, x), axis=-1, keepdims=True)
    return jnp.where(h0, s0, s1)


def _split_bf16(x):
    hi = x.astype(BF16)
    return hi, (x - hi.astype(F32)).astype(BF16)


def _rwkv_kernel(z_ref, sh0_ref, st0_ref, mu_ref, w0_ref, a0_ref, whi_ref, wlo_ref, kk_ref, ka_ref, rk_ref,
                 lng_ref, lnb_ref, o_ref, st_ref, st_sc, prev_sc, *, nb, groups, n_valid):
    C = RWKV_CHUNK
    R = groups * C
    npairs = RWKV_W // LANES

    @pl.when(pl.program_id(1) == 0)
    def _():
        st_sc[...] = st0_ref[...]
        prev_sc[...] = sh0_ref[...]

    si = lax.broadcasted_iota(jnp.int32, (2 * C, 2 * C), 0)
    sj = lax.broadcasted_iota(jnp.int32, (2 * C, 2 * C), 1)
    same_head = (si >= C) == (sj >= C)
    ti2 = si & (C - 1)
    tj2 = sj & (C - 1)
    incl = same_head & (tj2 <= ti2)
    strict = same_head & (tj2 < ti2)
    eye = jnp.where(si == sj, 1.0, 0.0).astype(F32)
    levels = []
    s = 2
    while s < C:
        sh = s.bit_length() - 1
        levels.append(((ti2 >> (sh + 1)) == (tj2 >> (sh + 1))) & (((ti2 >> sh) & 1) == 1) & (((tj2 >> sh) & 1) == 0))
        s *= 2
    first_level = (ti2 >> 1) == (tj2 >> 1)

    ri = lax.broadcasted_iota(jnp.int32, (R, R), 0)
    rj = lax.broadcasted_iota(jnp.int32, (R, R), 1)
    cs = C.bit_length() - 1
    tri = jnp.where(((ri >> cs) == (rj >> cs)) & (rj <= ri), 1.0, 0.0).astype(BF16)
    rowid = lax.broadcasted_iota(jnp.int32, (R, 1), 0)
    h0 = _half0((C, LANES))
    zero_blk = jnp.zeros((2 * C, LANES), F32)

    def stack(x):
        return jnp.concatenate([jnp.where(h0, x, 0.0), jnp.where(h0, 0.0, x)], axis=0)

    seqs = []
    for b in range(nb):
        z = z_ref[b]
        z_prev = jnp.where(rowid == 0, prev_sc[b], pltpu.roll(z, 1, 0))
        prev_sc[b] = z_ref[b, pl.ds(R - 1, 1), :]
        zs = z + (z_prev - z) * mu_ref[...]
        if n_valid < R:
            zs = jnp.where(rowid < n_valid, zs, 0.0)
        lat = zs[:, 3 * RWKV_W:]
        lat_hi, lat_lo = _split_bf16(jnp.where(_half0(lat.shape), jnp.tanh(lat), lat))
        dot = functools.partial(jnp.dot, preferred_element_type=F32)
        lora = dot(lat_hi, whi_ref[...]) + dot(lat_lo, whi_ref[...]) + dot(lat_hi, wlo_ref[...])
        w = -jax.nn.softplus(-(w0_ref[...] + lora[:, :RWKV_W])) - 0.5
        lw = -jnp.exp(w)
        if n_valid < R:
            lw = jnp.where(rowid < n_valid, lw, 0.0)
        a = jax.nn.sigmoid(a0_ref[...] + lora[:, RWKV_W:])
        lw_hi, lw_lo = _split_bf16(lw)
        cum = dot(tri, lw_hi) + dot(tri, lw_lo)
        seqs.append((zs, lw, a, cum))

    chains = [(b, g, p) for g in range(groups) for b in range(nb) for p in range(npairs)]
    pre = {}
    for (b, g, p) in chains:
        zs, lw, a, cum = seqs[b]
        rows = slice(g * C, (g + 1) * C)
        sl = slice(p * LANES, (p + 1) * LANES)
        r = zs[rows, sl]
        k = zs[rows, RWKV_W + p * LANES:RWKV_W + (p + 1) * LANES]
        v = zs[rows, 2 * RWKV_W + p * LANES:2 * RWKV_W + (p + 1) * LANES]
        ap = a[rows, sl]
        kk = k * kk_ref[:, sl]
        kk = kk / jnp.maximum(jnp.sqrt(_seg_sum(kk * kk, h0)), 1e-12)
        k = k * (1.0 + (ap - 1.0) * ka_ref[:, sl])
        bb = kk * ap
        L = cum[rows, sl]
        l_end = L[C - 1:C, :]
        e_out = jnp.exp(-L)
        e_end = jnp.exp(l_end - L)
        rt = stack(r * jnp.exp(L))
        kkh = stack(kk * jnp.exp(L - lw[rows, sl]))
        pre[(b, g, p)] = dict(
            left=jnp.concatenate([rt, kkh], axis=0).astype(BF16),
            right=jnp.concatenate([stack(k * e_out), stack(-(bb * e_out))], axis=0).astype(BF16),
            ends=jnp.concatenate([stack(k * e_end), stack(-(bb * e_end))], axis=0).astype(BF16),
            rt=rt, kkh=kkh, v2=stack(v), decay=jnp.exp(l_end),
            bonus=_seg_sum(r * k * rk_ref[:, sl], h0) * v)

    for c in chains:
        d = pre[c]
        aa = lax.dot_general(d["left"], d["right"], (((1,), (1,)), ((), ())), preferred_element_type=F32)
        d["a_r"] = jnp.concatenate([jnp.where(incl, aa[:2 * C, :2 * C], 0.0),
                                    jnp.where(incl, aa[:2 * C, 2 * C:], 0.0)], axis=1).astype(BF16)
        d["a_kk"] = jnp.where(strict, aa[2 * C:, :2 * C], 0.0)
        d["n_kb"] = jnp.where(strict, aa[2 * C:, 2 * C:], 0.0)
        d["t"] = eye + jnp.where(first_level, d["n_kb"], 0.0)
    for off in levels:
        for c in chains:
            d = pre[c]
            d["tn"] = _dot(d["t"], jnp.where(off, d["n_kb"], 0.0))
        for c in chains:
            d = pre[c]
            d["t"] = d["t"] + _dot(d["tn"], d["t"])
    for c in chains:
        d = pre[c]
        d["akv"] = _dot(d["a_kk"], d["v2"])
    for c in chains:
        d = pre[c]
        tr = _dot(d["t"], jnp.concatenate([d["kkh"], d["akv"]], axis=1))
        kq, u0 = tr[:, :LANES], tr[:, LANES:]
        d["big"] = jnp.concatenate([jnp.concatenate([d["v2"], zero_blk], axis=1),
                                    jnp.concatenate([u0, kq], axis=1)], axis=0).astype(BF16)
    for c in chains:
        d = pre[c]
        yr = jnp.dot(d["a_r"], d["big"], preferred_element_type=F32)
        d["y0"] = yr[:, :LANES]
        d["rq"] = (d["rt"] + yr[:, LANES:]).astype(BF16)
    for c in chains:
        d = pre[c]
        dg = lax.dot_general(d["big"], d["ends"], (((0,), (0,)), ((), ())), preferred_element_type=F32)
        d["dd"] = dg[:LANES]
        d["gm"] = dg[LANES:].astype(BF16)

    state = {(b, p): st_sc[b, p] for b in range(nb) for p in range(npairs)}
    for (b, g, p) in chains:
        d = pre[(b, g, p)]
        st = state[(b, p)]
        st_b = st.astype(BF16)
        d["y2"] = lax.dot_general(d["rq"], st_b, (((1,), (1,)), ((), ())), preferred_element_type=F32) + d["y0"]
        state[(b, p)] = st * d["decay"] + jnp.dot(st_b, d["gm"], preferred_element_type=F32) + d["dd"]
    for (b, g, p) in chains:
        d = pre[(b, g, p)]
        y2 = d["y2"]
        y = y2[:C] + y2[C:]
        sl = slice(p * LANES, (p + 1) * LANES)
        mean = _seg_sum(y, h0) * (1.0 / HEAD_DIM)
        yc = y - mean
        var = _seg_sum(yc * yc, h0) * (1.0 / HEAD_DIM)
        yn = yc * lax.rsqrt(var + LNX_EPS) * lng_ref[:, sl] + lnb_ref[:, sl]
        o_ref[b, g * C:(g + 1) * C, sl] = yn + d["bonus"]
    for (b, p), st in state.items():
        st_sc[b, p] = st

    @pl.when(pl.program_id(1) == pl.num_programs(1) - 1)
    def _():
        st_ref[...] = st_sc[...]


def _rwkv(z, shift0, st0, mu, w0, a0, w_hi, w_lo, k_k, k_a, r_k, ln_g, ln_b, nb, groups):
    b, n_valid, _ = z.shape
    rows = groups * RWKV_CHUNK
    if n_valid % rows:
        assert n_valid < rows
        z = jnp.pad(z, ((0, 0), (0, rows - n_valid), (0, 0)))
    s = z.shape[1]
    n_valid = min(n_valid, rows)
    vec = lambda w: pl.BlockSpec((1, w), lambda i, j: (0, 0))
    st_spec = pl.BlockSpec((nb, 3, LANES, LANES), lambda i, j: (i, 0, 0, 0))
    wspec = pl.BlockSpec((LANES, 2 * RWKV_W), lambda i, j: (0, 0))
    return pl.pallas_call(
        functools.partial(_rwkv_kernel, nb=nb, groups=groups, n_valid=n_valid),
        grid=(b // nb, s // rows),
        in_specs=[pl.BlockSpec((nb, rows, SHIFT_W), lambda i, j: (i, j, 0)),
                  pl.BlockSpec((nb, 1, SHIFT_W), lambda i, j: (i, 0, 0)),
                  st_spec,
                  vec(SHIFT_W), vec(RWKV_W), vec(RWKV_W), wspec, wspec,
                  vec(RWKV_W), vec(RWKV_W), vec(RWKV_W), vec(RWKV_W), vec(RWKV_W)],
        out_specs=[pl.BlockSpec((nb, rows, RWKV_W), lambda i, j: (i, j, 0)), st_spec],
        out_shape=[jax.ShapeDtypeStruct((b, s, RWKV_W), F32),
                   jax.ShapeDtypeStruct((b, 3, LANES, LANES), F32)],
        scratch_shapes=[pltpu.VMEM((nb, 3, LANES, LANES), F32), pltpu.VMEM((nb, 1, SHIFT_W), F32)],
        compiler_params=_cparams(("arbitrary", "arbitrary")),
        name="rwkv",
    )(z, shift0, st0, mu, w0, a0, w_hi, w_lo, k_k, k_a, r_k, ln_g, ln_b)


def _state_to_pairs(st):
    b = st.shape[0]
    st = st.reshape(b, 3, 2, HEAD_DIM, HEAD_DIM)
    zero = jnp.zeros_like(st[:, :, 0])
    top = jnp.concatenate([st[:, :, 0], zero], axis=-1)
    bot = jnp.concatenate([zero, st[:, :, 1]], axis=-1)
    return jnp.concatenate([top, bot], axis=-2)


def _pairs_to_state(sp):
    b = sp.shape[0]
    return jnp.stack([sp[:, :, :HEAD_DIM, :HEAD_DIM], sp[:, :, HEAD_DIM:, HEAD_DIM:]], axis=2).reshape(
        b, 6, HEAD_DIM, HEAD_DIM)


def _memkv_kernel(x_ref, g_ref, w_ref, k_ref, v_ref):
    x = x_ref[...]
    ms = jnp.mean(x * x, axis=-1, keepdims=True)
    h = ((x * lax.rsqrt(ms + NORM_EPS)) * g_ref[...]).astype(BF16)
    k_ref[...] = jnp.dot(h, w_ref[:, :MEM_W], preferred_element_type=F32)
    v_ref[...] = jnp.dot(h, w_ref[:, MEM_W:], preferred_element_type=F32)


def _memkv(mem2d, g, w_bf16):
    rows = mem2d.shape[0]
    out = jax.ShapeDtypeStruct((rows, MEM_W), F32)
    return pl.pallas_call(
        _memkv_kernel,
        out_shape=[out, out],
        compiler_params=pltpu.CompilerParams(vmem_limit_bytes=VMEM_LIMIT),
        name="memkv",
    )(mem2d, g, w_bf16)


def _memattn_kernel(q_ref, mk_ref, mv_ref, o_ref):
    q = q_ref[0]
    mk = mk_ref[0].astype(BF16)
    mv = mv_ref[0].astype(BF16)
    head_of_lane = lax.broadcasted_iota(jnp.int32, q.shape, 1) >> 6
    out = jnp.zeros(q.shape, F32)
    for h in range(MEM_W // HEAD_DIM):
        mine = head_of_lane == h
        s = _dot_nt(jnp.where(mine, q, 0.0), mk)
        pexp = jnp.exp(s - jnp.max(s, axis=-1, keepdims=True))
        l = jnp.sum(pexp, axis=-1, keepdims=True)
        out = jnp.where(mine, _dot(pexp, mv) / l, out)
    o_ref[0] = out


def _memattn(q, mk, mv, tm):
    b, s, _ = q.shape
    qspec = pl.BlockSpec((1, tm, MEM_W), lambda i, j: (i, j, 0))
    mspec = pl.BlockSpec((1, N_MEM, MEM_W), lambda i, j: (i, 0, 0))
    return pl.pallas_call(
        _memattn_kernel,
        grid=(b, s // tm),
        in_specs=[qspec, mspec, mspec],
        out_specs=qspec,
        out_shape=jax.ShapeDtypeStruct(q.shape, F32),
        compiler_params=_cparams(("arbitrary", "arbitrary")),
        name="memattn",
    )(q, mk, mv)


def _out_kernel(oa_ref, ob_ref, om_ref, gate_ref, x_ref, w_ref, g_ref, y_ref, *, split_pairs):
    acc = x_ref[...]

    def add(acc, o, c0):
        width = o.shape[1]
        gate = gate_ref[:, c0:c0 + width]
        return acc + _dot(o * (gate * jax.nn.sigmoid(gate)), w_ref[c0:c0 + width, :])

    if split_pairs:
        for p in range(ATT_W // LANES):
            acc = add(acc, oa_ref[p], p * LANES)
    else:
        acc = add(acc, oa_ref[...], 0)
    acc = add(acc, ob_ref[...], ATT_W)
    acc = add(acc, om_ref[...], ATT_W + RWKV_W)
    ms = jnp.mean(acc * acc, axis=-1, keepdims=True)
    y_ref[...] = (acc * lax.rsqrt(ms + NORM_EPS)) * g_ref[...]


def _out(oa, ob, om, gate, x2d, w_bf16, g, tm, split_pairs):
    rows = x2d.shape[0]
    row = lambda w: pl.BlockSpec((tm, w), lambda i: (i, 0))
    oa_spec = pl.BlockSpec((3, tm, LANES), lambda i: (0, i, 0)) if split_pairs else row(ATT_W)
    return pl.pallas_call(
        functools.partial(_out_kernel, split_pairs=split_pairs),
        grid=(rows // tm,),
        in_specs=[oa_spec, row(RWKV_W), row(MEM_W), row(MIX_W), row(D_MODEL),
                  pl.BlockSpec((MIX_W, D_MODEL), lambda i: (0, 0)),
                  pl.BlockSpec((1, D_MODEL), lambda i: (0, 0))],
        out_specs=row(D_MODEL),
        out_shape=jax.ShapeDtypeStruct((rows, D_MODEL), F32),
        compiler_params=_cparams(("arbitrary",)),
        name="outproj",
    )(oa, ob, om, gate, x2d, w_bf16, g)


def kernel(x_prompt, x_sample, cache_win_k, cache_win_v, state_rwkv, state_rwkv_shift, cache_mem_k, cache_mem_v, mem_prompt, norm_in, w_in, rwkv_mu, rwkv_w0, rwkv_w2, rwkv_a0, rwkv_a2, rwkv_k_k, rwkv_k_a, rwkv_r_k, rwkv_lnx_g, rwkv_lnx_b, norm_mem, w_mem_kv, w_out, norm_final):
    B, S, _ = x_prompt.shape
    DB, T, _ = x_sample.shape
    depth = w_in.shape[0]
    assert depth == 1 and S % ATT_TILE == 0 and cache_win_k.shape[2] == max(DILATIONS) * BAND
    l = 0
    past_len = S

    half = ROPE_DIM // 2
    inv_freq = ROPE_THETA ** (-jnp.arange(half, dtype=F32) / half)
    lane = jnp.arange(LANES)
    invf = jnp.where((lane % HEAD_DIM) < ROPE_DIM, inv_freq[lane % half], 0.0).reshape(1, LANES)

    row = lambda t: t.reshape(1, -1)
    w_in_b = w_in[l].astype(BF16)
    w_out_b = w_out[l].astype(BF16)
    w_kv_b = w_mem_kv[l].astype(BF16)
    zero = jnp.zeros((LORA_W, RWKV_W), F32)
    w2a2 = jnp.concatenate([jnp.concatenate([rwkv_w2[l], zero], axis=1),
                            jnp.concatenate([zero, rwkv_a2[l]], axis=1)], axis=0)
    w2a2_hi = w2a2.astype(BF16)
    w2a2_lo = (w2a2 - w2a2_hi.astype(F32)).astype(BF16)
    rw = (row(rwkv_mu[l]), row(rwkv_w0[l]), row(rwkv_a0[l]), w2a2_hi, w2a2_lo, row(rwkv_k_k[l]),
          row(rwkv_k_a[l]), row(rwkv_r_k[l]), row(rwkv_lnx_g[l]), row(rwkv_lnx_b[l]))

    xp = x_prompt.reshape(B * S, D_MODEL)
    pos_p = jnp.tile(jnp.arange(S, dtype=F32), B).reshape(B * S, 1)
    q3, k3, v3, zb, qm, gate = _proj(xp, pos_p, row(norm_in[l]), invf, w_in_b, 512, True)
    oa3 = _attn_prompt(q3, k3, v3, B, S)
    ob, st_p = _rwkv(zb.reshape(B, S, SHIFT_W), jnp.zeros((B, 1, SHIFT_W), F32),
                     jnp.zeros((B, 3, LANES, LANES), F32), *rw, nb=B, groups=2)
    mk, mv = _memkv(mem_prompt.reshape(B * N_MEM, D_MODEL), row(norm_mem[l]), w_kv_b)
    mk = mk.reshape(B, N_MEM, MEM_W)
    mv = mv.reshape(B, N_MEM, MEM_W)
    om = _memattn(qm.reshape(B, S, MEM_W), mk, mv, 1024)
    y_p = _out(oa3, ob.reshape(B * S, RWKV_W), om.reshape(B * S, MEM_W), gate, xp, w_out_b,
               row(norm_final), 512, True)

    win = min(max(DILATIONS) * BAND, S)
    tail = lambda t3: jnp.transpose(t3.reshape(3, B, S, LANES)[:, :, S - win:], (1, 2, 0, 3)).reshape(
        1, B, win, ATT_W // HEAD_DIM, HEAD_DIM)
    heads = lambda t, n: t.reshape(1, t.shape[0], t.shape[1], n, HEAD_DIM)

    xs = x_sample.reshape(DB * T, D_MODEL)
    pos_s = jnp.tile(past_len + jnp.arange(T, dtype=F32), DB).reshape(DB * T, 1)
    qs, ks, vs, zbs, qms, gates = _proj(xs, pos_s, row(norm_in[l]), invf, w_in_b, DB * T, False)
    n_past = cache_win_k.shape[2]
    oas = _attn_sample(qs.reshape(DB, T, ATT_W), ks.reshape(DB, T, ATT_W), vs.reshape(DB, T, ATT_W),
                       cache_win_k[l].reshape(DB, n_past, ATT_W), cache_win_v[l].reshape(DB, n_past, ATT_W))
    zbs3 = zbs.reshape(DB, T, SHIFT_W)
    obs, st_s = _rwkv(zbs3, state_rwkv_shift[l].reshape(DB, 1, SHIFT_W), _state_to_pairs(state_rwkv[l]),
                      *rw, nb=4, groups=1)
    obs = obs[:, :T]
    oms = _memattn(qms.reshape(DB, T, MEM_W), cache_mem_k[l].reshape(DB, N_MEM, MEM_W),
                   cache_mem_v[l].reshape(DB, N_MEM, MEM_W), T)
    y_s = _out(oas.reshape(DB * T, ATT_W), obs.reshape(DB * T, RWKV_W), oms.reshape(DB * T, MEM_W), gates, xs,
               w_out_b, row(norm_final), DB * T, False)

    return (y_p.reshape(B, S, D_MODEL), y_s.reshape(DB, T, D_MODEL),
            tail(k3), tail(v3),
            _pairs_to_state(st_p)[None], zb.reshape(B, S, SHIFT_W)[:, -1][None],
            heads(mk, MEM_W // HEAD_DIM), heads(mv, MEM_W // HEAD_DIM),
            heads(ks.reshape(DB, T, ATT_W), ATT_W // HEAD_DIM), heads(vs.reshape(DB, T, ATT_W), ATT_W // HEAD_DIM),
            _pairs_to_state(st_s)[None], zbs3[:, -1][None])
```

```python
import functools

import jax
import jax.numpy as jnp
from jax import lax
from jax.experimental import pallas as pl
from jax.experimental.pallas import tpu as pltpu

F32 = jnp.float32
BF16 = jnp.bfloat16

D_MODEL = 1024
HEAD_DIM = 64
ATT_W = 384
RWKV_W = 384
MEM_W = 256
MIX_W = 1024
LORA_W = 64
SHIFT_W = 3 * RWKV_W + 2 * LORA_W
N_MEM = 256
ROPE_DIM = 16
ROPE_THETA = 500000.0
NORM_EPS = 1e-6
LNX_EPS = 64e-5
DILATIONS = (1, 4, 16)
BAND = 128
ATT_TILE = BAND * max(DILATIONS)
LANES = 128
RWKV_CHUNK = 64
NEG = -1e30
VMEM_LIMIT = 56 * 1024 * 1024


def _cparams(sem):
    return pltpu.CompilerParams(dimension_semantics=sem, vmem_limit_bytes=VMEM_LIMIT)


def _dot(a, b):
    return jnp.dot(a.astype(BF16), b.astype(BF16), preferred_element_type=F32)


def _dot_nt(a, b):
    return lax.dot_general(a.astype(BF16), b.astype(BF16), (((1,), (1,)), ((), ())),
                           preferred_element_type=F32)


def _dot_tn(a, b):
    return lax.dot_general(a.astype(BF16), b.astype(BF16), (((0,), (0,)), ((), ())),
                           preferred_element_type=F32)


def _dot_f32(a, b):
    return jnp.dot(a, b, preferred_element_type=F32, precision=lax.Precision.HIGHEST)


def _half0(shape):
    return (lax.broadcasted_iota(jnp.int32, shape, len(shape) - 1) & 64) == 0


def _proj_kernel(x_ref, pos_ref, g_ref, invf_ref, w_ref, q_ref, k_ref, v_ref, zb_ref, qm_ref, gate_ref,
                 *, split_pairs):
    x = x_ref[...]
    ms = jnp.mean(x * x, axis=-1, keepdims=True)
    h = ((x * lax.rsqrt(ms + NORM_EPS)) * g_ref[...]).astype(BF16)

    ang = pos_ref[...] * invf_ref[...]
    cos = jnp.cos(ang)
    sin = jnp.sin(ang)
    lane = lax.broadcasted_iota(jnp.int32, ang.shape, 1)
    second = (lane & 8) != 0
    sin_up = jnp.where(second, sin, 0.0)
    sin_dn = jnp.where(second, 0.0, -sin)

    def rope(t):
        return t * cos + pltpu.roll(t, 8, 1) * sin_up + pltpu.roll(t, LANES - 8, 1) * sin_dn

    def put(ref, p, val):
        if split_pairs:
            ref[p] = val
        else:
            ref[:, p * LANES:(p + 1) * LANES] = val

    for p in range(ATT_W // LANES):
        c0 = p * LANES
        put(q_ref, p, rope(jnp.dot(h, w_ref[:, c0:c0 + LANES], preferred_element_type=F32)) * 0.125)
        c0 = ATT_W + p * LANES
        put(k_ref, p, rope(jnp.dot(h, w_ref[:, c0:c0 + LANES], preferred_element_type=F32)))
        c0 = 2 * ATT_W + p * LANES
        put(v_ref, p, jnp.dot(h, w_ref[:, c0:c0 + LANES], preferred_element_type=F32))
    c0 = 3 * ATT_W
    zb_ref[...] = jnp.dot(h, w_ref[:, c0:c0 + SHIFT_W], preferred_element_type=F32)
    c0 += SHIFT_W
    qm_ref[...] = jnp.dot(h, w_ref[:, c0:c0 + MEM_W], preferred_element_type=F32) * 0.125
    c0 += MEM_W
    gate_ref[...] = jnp.dot(h, w_ref[:, c0:c0 + MIX_W], preferred_element_type=F32)


def _proj(x2d, pos, g, invf, w_bf16, tm, split_pairs):
    rows = x2d.shape[0]
    in_w = w_bf16.shape[1]
    if split_pairs:
        qkv_shape = jax.ShapeDtypeStruct((3, rows, LANES), F32)
        qkv_spec = pl.BlockSpec((3, tm, LANES), lambda i: (0, i, 0))
    else:
        qkv_shape = jax.ShapeDtypeStruct((rows, ATT_W), F32)
        qkv_spec = pl.BlockSpec((tm, ATT_W), lambda i: (i, 0))
    row = lambda w: pl.BlockSpec((tm, w), lambda i: (i, 0))
    full = lambda a, b: pl.BlockSpec((a, b), lambda i: (0, 0))
    return pl.pallas_call(
        functools.partial(_proj_kernel, split_pairs=split_pairs),
        grid=(rows // tm,),
        in_specs=[row(D_MODEL), row(1), full(1, D_MODEL), full(1, LANES), full(D_MODEL, in_w)],
        out_specs=[qkv_spec, qkv_spec, qkv_spec, row(SHIFT_W), row(MEM_W), row(MIX_W)],
        out_shape=[qkv_shape, qkv_shape, qkv_shape,
                   jax.ShapeDtypeStruct((rows, SHIFT_W), F32),
                   jax.ShapeDtypeStruct((rows, MEM_W), F32),
                   jax.ShapeDtypeStruct((rows, MIX_W), F32)],
        compiler_params=_cparams(("arbitrary",)),
        name="proj",
    )(x2d, pos, g, invf, w_bf16)


def _attn_unit(npairs, d, q_ref, kp_ref, kc_ref, vp_ref, vc_ref, qs, ps, bias, m_sc, l_sc, acc_sc):
    row = pl.ds(qs, BAND, stride=d)
    prow = pl.ds(ps, BAND, stride=d)
    h0 = _half0((BAND, LANES))
    h0k = _half0((2 * BAND, LANES))
    heads = [(p, hh) for p in range(npairs) for hh in range(2)]
    q = [q_ref[p, row, :] for p in range(npairs)]
    kcat = [jnp.concatenate([kp_ref[p, prow, :], kc_ref[p, row, :]], axis=0).astype(BF16) for p in range(npairs)]
    vcat = [jnp.concatenate([vp_ref[p, prow, :], vc_ref[p, row, :]], axis=0) for p in range(npairs)]
    m_run = [m_sc[2 * p + hh, row, :] for p, hh in heads]
    acc_run = [acc_sc[p, row, :] for p in range(npairs)]
    l_run = [l_sc[p, row, :] for p in range(npairs)]
    s = [_dot_nt(jnp.where(h0 if hh == 0 else jnp.logical_not(h0), q[p], 0.0), kcat[p]) + bias
         for p, hh in heads]
    m_new = [jnp.maximum(m_run[i], jnp.max(s[i], axis=-1, keepdims=True)) for i in range(len(heads))]
    pexp = [jnp.exp(s[i] - jnp.concatenate([m_new[i], m_new[i]], axis=1)).astype(BF16) for i in range(len(heads))]
    alpha = [jnp.exp(m_run[i] - m_new[i]) for i in range(len(heads))]
    res = [jnp.dot(pexp[i], jnp.where(h0k if hh == 0 else jnp.logical_not(h0k), vcat[p], 1.0).astype(BF16),
                   preferred_element_type=F32) for i, (p, hh) in enumerate(heads)]
    for i, (p, hh) in enumerate(heads):
        m_sc[2 * p + hh, row, :] = m_new[i]
    for p in range(npairs):
        a0, a1, r0, r1 = alpha[2 * p], alpha[2 * p + 1], res[2 * p], res[2 * p + 1]
        acc_sc[p, row, :] = acc_run[p] * jnp.where(h0, a0, a1) + jnp.where(h0, r0, r1)
        l_sc[p, row, :] = l_run[p] * jnp.where(h0, a1, a0) + jnp.where(h0, r1, r0)


def _attn_kernel(q_ref, kc_ref, kp_ref, vc_ref, vp_ref, o_ref, m_sc, l_sc, acc_sc):
    has_prev = pl.program_id(1) > 0
    m_sc[...] = jnp.full(m_sc.shape, NEG, F32)
    l_sc[...] = jnp.zeros(l_sc.shape, F32)
    acc_sc[...] = jnp.zeros(acc_sc.shape, F32)

    qi = lax.broadcasted_iota(jnp.int32, (BAND, 2 * BAND), 0)
    kj = lax.broadcasted_iota(jnp.int32, (BAND, 2 * BAND), 1)
    in_cur = (kj >= BAND) & (kj - BAND <= qi)
    in_prev = (kj < BAND) & (kj >= qi)
    bias_full = jnp.where(in_cur | in_prev, 0.0, NEG).astype(F32)
    bias_first = jnp.where(in_cur | (in_prev & has_prev), 0.0, NEG).astype(F32)

    npairs = q_ref.shape[0]
    for d in DILATIONS:
        span = BAND * d
        shift = d.bit_length() - 1

        def first_span(r, carry, d=d, span=span):
            _attn_unit(npairs, d, q_ref, kp_ref, kc_ref, vp_ref, vc_ref, r, ATT_TILE - span + r,
                       bias_first, m_sc, l_sc, acc_sc)
            return carry

        def later_span(u, carry, d=d, span=span, shift=shift):
            qs = (u >> shift) * span + (u & (d - 1))
            _attn_unit(npairs, d, q_ref, kc_ref, kc_ref, vc_ref, vc_ref, qs, qs - span,
                       bias_full, m_sc, l_sc, acc_sc)
            return carry

        lax.fori_loop(0, d, first_span, 0)
        if d < ATT_TILE // BAND:
            lax.fori_loop(d, ATT_TILE // BAND, later_span, 0)

    def finish(i, carry):
        row = pl.ds(pl.multiple_of(i * BAND, BAND), BAND)
        for p in range(npairs):
            denom = pltpu.roll(l_sc[p, row, :], HEAD_DIM, 1)
            o_ref[p, row, :] = acc_sc[p, row, :] / denom
        return carry

    lax.fori_loop(0, ATT_TILE // BAND, finish, 0)


def _attn_prompt(q3, k3, v3, batch, seq):
    nt = seq // ATT_TILE
    cur = pl.BlockSpec((3, ATT_TILE, LANES), lambda b, i: (0, b * nt + i, 0))
    prev = pl.BlockSpec((3, ATT_TILE, LANES), lambda b, i: (0, b * nt + jnp.maximum(i - 1, 0), 0))
    return pl.pallas_call(
        _attn_kernel,
        grid=(batch, nt),
        in_specs=[cur, cur, prev, cur, prev],
        out_specs=cur,
        out_shape=jax.ShapeDtypeStruct(q3.shape, F32),
        scratch_shapes=[pltpu.VMEM((6, ATT_TILE, LANES), F32),
                        pltpu.VMEM((3, ATT_TILE, LANES), F32),
                        pltpu.VMEM((3, ATT_TILE, LANES), F32)],
        compiler_params=_cparams(("arbitrary", "arbitrary")),
        name="attn_prompt",
    )(q3, k3, k3, v3, v3)


def _attn_sample_kernel(q_ref, kn_ref, vn_ref, kc_ref, vc_ref, o_ref, *, n_past, t_new):
    nh = ATT_W // HEAD_DIM
    q = q_ref[0]
    lane = lax.broadcasted_iota(jnp.int32, q.shape, 1)
    head_of_lane = lane >> 6
    qst = jnp.concatenate([jnp.where(head_of_lane == h, q, 0.0) for h in range(nh)], axis=0)

    def count(delta):
        c = jnp.zeros(delta.shape, F32)
        for d in DILATIONS:
            ok = (delta >= 0) & (delta <= BAND * d) & ((delta & (d - 1)) == 0)
            c = c + jnp.where(ok, 1.0, 0.0)
        return c

    rows = nh * t_new
    t_past = lax.broadcasted_iota(jnp.int32, (rows, n_past), 0) & (t_new - 1)
    cnt_past = count(n_past + t_past - lax.broadcasted_iota(jnp.int32, (rows, n_past), 1))
    t_n = lax.broadcasted_iota(jnp.int32, (rows, t_new), 0) & (t_new - 1)
    cnt_new = count(t_n - lax.broadcasted_iota(jnp.int32, (rows, t_new), 1))

    s_past = jnp.where(cnt_past > 0, _dot(qst, kc_ref[0]), NEG)
    s_new = jnp.where(cnt_new > 0, _dot_nt(qst, kn_ref[0]), NEG)
    m = jnp.maximum(jnp.max(s_past, axis=-1, keepdims=True), jnp.max(s_new, axis=-1, keepdims=True))
    p_past = cnt_past * jnp.exp(s_past - m)
    p_new = cnt_new * jnp.exp(s_new - m)
    l = jnp.sum(p_past, axis=-1, keepdims=True) + jnp.sum(p_new, axis=-1, keepdims=True)
    o = (_dot_nt(p_past, vc_ref[0]) + _dot(p_new, vn_ref[0])) / l
    out = jnp.zeros(q.shape, F32)
    for h in range(nh):
        out = jnp.where(head_of_lane == h, o[h * t_new:(h + 1) * t_new, :], out)
    o_ref[0] = out


def _attn_sample(q, k_new, v_new, k_cache, v_cache):
    db, t_new, _ = q.shape
    n_past = k_cache.shape[2]
    new = pl.BlockSpec((1, t_new, ATT_W), lambda b: (b, 0, 0))
    cache = pl.BlockSpec((1, ATT_W, n_past), lambda b: (b, 0, 0))
    return pl.pallas_call(
        functools.partial(_attn_sample_kernel, n_past=n_past, t_new=t_new),
        grid=(db,),
        in_specs=[new, new, new, cache, cache],
        out_specs=new,
        out_shape=jax.ShapeDtypeStruct(q.shape, F32),
        compiler_params=_cparams(("arbitrary",)),
        name="attn_sample",
    )(q, k_new, v_new, k_cache, v_cache)


def _seg_sum(x, h0):
    s0 = jnp.sum(jnp.where(h0, x, 0.0), axis=-1, keepdims=True)
    s1 = jnp.sum(jnp.where(h0, 0.0, x), axis=-1, keepdims=True)
    return jnp.where(h0, s0, s1)


def _split_bf16(x):
    hi = x.astype(BF16)
    return hi, (x - hi.astype(F32)).astype(BF16)


def _rwkv_kernel(z_ref, sh0_ref, st0_ref, mu_ref, w0_ref, a0_ref, whi_ref, wlo_ref, kk_ref, ka_ref, rk_ref,
                 lng_ref, lnb_ref, o_ref, st_ref, st_sc, prev_sc, *, nb, groups, n_valid):
    C = RWKV_CHUNK
    R = groups * C
    npairs = RWKV_W // LANES

    @pl.when(pl.program_id(1) == 0)
    def _():
        st_sc[...] = st0_ref[...]
        prev_sc[...] = sh0_ref[...]

    si = lax.broadcasted_iota(jnp.int32, (2 * C, 2 * C), 0)
    sj = lax.broadcasted_iota(jnp.int32, (2 * C, 2 * C), 1)
    same_head = (si >= C) == (sj >= C)
    ti2 = si & (C - 1)
    tj2 = sj & (C - 1)
    incl = same_head & (tj2 <= ti2)
    strict = same_head & (tj2 < ti2)
    eye = jnp.where(si == sj, 1.0, 0.0).astype(F32)
    levels = []
    s = 2
    while s < C:
        sh = s.bit_length() - 1
        levels.append(((ti2 >> (sh + 1)) == (tj2 >> (sh + 1))) & (((ti2 >> sh) & 1) == 1) & (((tj2 >> sh) & 1) == 0))
        s *= 2
    first_level = (ti2 >> 1) == (tj2 >> 1)

    ri = lax.broadcasted_iota(jnp.int32, (R, R), 0)
    rj = lax.broadcasted_iota(jnp.int32, (R, R), 1)
    cs = C.bit_length() - 1
    tri = jnp.where(((ri >> cs) == (rj >> cs)) & (rj <= ri), 1.0, 0.0).astype(BF16)
    rowid = lax.broadcasted_iota(jnp.int32, (R, 1), 0)
    h0 = _half0((C, LANES))
    zero_blk = jnp.zeros((2 * C, LANES), F32)

    def stack(x):
        return jnp.concatenate([jnp.where(h0, x, 0.0), jnp.where(h0, 0.0, x)], axis=0)

    seqs = []
    for b in range(nb):
        z = z_ref[b]
        z_prev = jnp.where(rowid == 0, prev_sc[b], pltpu.roll(z, 1, 0))
        prev_sc[b] = z_ref[b, pl.ds(R - 1, 1), :]
        zs = z + (z_prev - z) * mu_ref[...]
        if n_valid < R:
            zs = jnp.where(rowid < n_valid, zs, 0.0)
        lat = zs[:, 3 * RWKV_W:]
        lat_hi, lat_lo = _split_bf16(jnp.where(_half0(lat.shape), jnp.tanh(lat), lat))
        dot = functools.partial(jnp.dot, preferred_element_type=F32)
        lora = dot(lat_hi, whi_ref[...]) + dot(lat_lo, whi_ref[...]) + dot(lat_hi, wlo_ref[...])
        w = -jax.nn.softplus(-(w0_ref[...] + lora[:, :RWKV_W])) - 0.5
        lw = -jnp.exp(w)
        if n_valid < R:
            lw = jnp.where(rowid < n_valid, lw, 0.0)
        a = jax.nn.sigmoid(a0_ref[...] + lora[:, RWKV_W:])
        lw_hi, lw_lo = _split_bf16(lw)
        cum = dot(tri, lw_hi) + dot(tri, lw_lo)
        seqs.append((zs, lw, a, cum))

    chains = [(b, g, p) for g in range(groups) for b in range(nb) for p in range(npairs)]
    pre = {}
    for (b, g, p) in chains:
        zs, lw, a, cum = seqs[b]
        rows = slice(g * C, (g + 1) * C)
        sl = slice(p * LANES, (p + 1) * LANES)
        r = zs[rows, sl]
        k = zs[rows, RWKV_W + p * LANES:RWKV_W + (p + 1) * LANES]
        v = zs[rows, 2 * RWKV_W + p * LANES:2 * RWKV_W + (p + 1) * LANES]
        ap = a[rows, sl]
        kk = k * kk_ref[:, sl]
        kk = kk / jnp.maximum(jnp.sqrt(_seg_sum(kk * kk, h0)), 1e-12)
        k = k * (1.0 + (ap - 1.0) * ka_ref[:, sl])
        bb = kk * ap
        L = cum[rows, sl]
        l_end = L[C - 1:C, :]
        e_out = jnp.exp(-L)
        e_end = jnp.exp(l_end - L)
        rt = stack(r * jnp.exp(L))
        kkh = stack(kk * jnp.exp(L - lw[rows, sl]))
        pre[(b, g, p)] = dict(
            left=jnp.concatenate([rt, kkh], axis=0).astype(BF16),
            right=jnp.concatenate([stack(k * e_out), stack(-(bb * e_out))], axis=0).astype(BF16),
            ends=jnp.concatenate([stack(k * e_end), stack(-(bb * e_end))], axis=0).astype(BF16),
            rt=rt, kkh=kkh, v2=stack(v), decay=jnp.exp(l_end),
            bonus=_seg_sum(r * k * rk_ref[:, sl], h0) * v)

    for c in chains:
        d = pre[c]
        aa = lax.dot_general(d["left"], d["right"], (((1,), (1,)), ((), ())), preferred_element_type=F32)
        d["a_r"] = jnp.concatenate([jnp.where(incl, aa[:2 * C, :2 * C], 0.0),
                                    jnp.where(incl, aa[:2 * C, 2 * C:], 0.0)], axis=1).astype(BF16)
        d["a_kk"] = jnp.where(strict, aa[2 * C:, :2 * C], 0.0)
        d["n_kb"] = jnp.where(strict, aa[2 * C:, 2 * C:], 0.0)
        d["t"] = eye + jnp.where(first_level, d["n_kb"], 0.0)
    for off in levels:
        for c in chains:
            d = pre[c]
            d["tn"] = _dot(d["t"], jnp.where(off, d["n_kb"], 0.0))
        for c in chains:
            d = pre[c]
            d["t"] = d["t"] + _dot(d["tn"], d["t"])
    for c in chains:
        d = pre[c]
        d["akv"] = _dot(d["a_kk"], d["v2"])
    for c in chains:
        d = pre[c]
        tr = _dot(d["t"], jnp.concatenate([d["kkh"], d["akv"]], axis=1))
        kq, u0 = tr[:, :LANES], tr[:, LANES:]
        d["big"] = jnp.concatenate([jnp.concatenate([d["v2"], zero_blk], axis=1),
                                    jnp.concatenate([u0, kq], axis=1)], axis=0).astype(BF16)
    for c in chains:
        d = pre[c]
        yr = jnp.dot(d["a_r"], d["big"], preferred_element_type=F32)
        d["y0"] = yr[:, :LANES]
        d["rq"] = (d["rt"] + yr[:, LANES:]).astype(BF16)
    for c in chains:
        d = pre[c]
        dg = lax.dot_general(d["big"], d["ends"], (((0,), (0,)), ((), ())), preferred_element_type=F32)
        d["dd"] = dg[:LANES]
        d["gm"] = dg[LANES:].astype(BF16)

    state = {(b, p): st_sc[b, p] for b in range(nb) for p in range(npairs)}
    for (b, g, p) in chains:
        d = pre[(b, g, p)]
        st = state[(b, p)]
        st_b = st.astype(BF16)
        d["y2"] = lax.dot_general(d["rq"], st_b, (((1,), (1,)), ((), ())), preferred_element_type=F32) + d["y0"]
        state[(b, p)] = st * d["decay"] + jnp.dot(st_b, d["gm"], preferred_element_type=F32) + d["dd"]
    for (b, g, p) in chains:
        d = pre[(b, g, p)]
        y2 = d["y2"]
        y = y2[:C] + y2[C:]
        sl = slice(p * LANES, (p + 1) * LANES)
        mean = _seg_sum(y, h0) * (1.0 / HEAD_DIM)
        yc = y - mean
        var = _seg_sum(yc * yc, h0) * (1.0 / HEAD_DIM)
        yn = yc * lax.rsqrt(var + LNX_EPS) * lng_ref[:, sl] + lnb_ref[:, sl]
        o_ref[b, g * C:(g + 1) * C, sl] = yn + d["bonus"]
    for (b, p), st in state.items():
        st_sc[b, p] = st

    @pl.when(pl.program_id(1) == pl.num_programs(1) - 1)
    def _():
        st_ref[...] = st_sc[...]


def _rwkv(z, shift0, st0, mu, w0, a0, w_hi, w_lo, k_k, k_a, r_k, ln_g, ln_b, nb, groups):
    b, n_valid, _ = z.shape
    rows = groups * RWKV_CHUNK
    if n_valid % rows:
        assert n_valid < rows
        z = jnp.pad(z, ((0, 0), (0, rows - n_valid), (0, 0)))
    s = z.shape[1]
    n_valid = min(n_valid, rows)
    vec = lambda w: pl.BlockSpec((1, w), lambda i, j: (0, 0))
    st_spec = pl.BlockSpec((nb, 3, LANES, LANES), lambda i, j: (i, 0, 0, 0))
    wspec = pl.BlockSpec((LANES, 2 * RWKV_W), lambda i, j: (0, 0))
    return pl.pallas_call(
        functools.partial(_rwkv_kernel, nb=nb, groups=groups, n_valid=n_valid),
        grid=(b // nb, s // rows),
        in_specs=[pl.BlockSpec((nb, rows, SHIFT_W), lambda i, j: (i, j, 0)),
                  pl.BlockSpec((nb, 1, SHIFT_W), lambda i, j: (i, 0, 0)),
                  st_spec,
                  vec(SHIFT_W), vec(RWKV_W), vec(RWKV_W), wspec, wspec,
                  vec(RWKV_W), vec(RWKV_W), vec(RWKV_W), vec(RWKV_W), vec(RWKV_W)],
        out_specs=[pl.BlockSpec((nb, rows, RWKV_W), lambda i, j: (i, j, 0)), st_spec],
        out_shape=[jax.ShapeDtypeStruct((b, s, RWKV_W), F32),
                   jax.ShapeDtypeStruct((b, 3, LANES, LANES), F32)],
        scratch_shapes=[pltpu.VMEM((nb, 3, LANES, LANES), F32), pltpu.VMEM((nb, 1, SHIFT_W), F32)],
        compiler_params=_cparams(("arbitrary", "arbitrary")),
        name="rwkv",
    )(z, shift0, st0, mu, w0, a0, w_hi, w_lo, k_k, k_a, r_k, ln_g, ln_b)


def _state_to_pairs(st):
    b = st.shape[0]
    st = st.reshape(b, 3, 2, HEAD_DIM, HEAD_DIM)
    zero = jnp.zeros_like(st[:, :, 0])
    top = jnp.concatenate([st[:, :, 0], zero], axis=-1)
    bot = jnp.concatenate([zero, st[:, :, 1]], axis=-1)
    return jnp.concatenate([top, bot], axis=-2)


def _pairs_to_state(sp):
    b = sp.shape[0]
    return jnp.stack([sp[:, :, :HEAD_DIM, :HEAD_DIM], sp[:, :, HEAD_DIM:, HEAD_DIM:]], axis=2).reshape(
        b, 6, HEAD_DIM, HEAD_DIM)


def _memkv_kernel(x_ref, g_ref, w_ref, k_ref, v_ref):
    x = x_ref[...]
    ms = jnp.mean(x * x, axis=-1, keepdims=True)
    h = ((x * lax.rsqrt(ms + NORM_EPS)) * g_ref[...]).astype(BF16)
    k_ref[...] = jnp.dot(h, w_ref[:, :MEM_W], preferred_element_type=F32)
    v_ref[...] = jnp.dot(h, w_ref[:, MEM_W:], preferred_element_type=F32)


def _memkv(mem2d, g, w_bf16):
    rows = mem2d.shape[0]
    out = jax.ShapeDtypeStruct((rows, MEM_W), F32)
    return pl.pallas_call(
        _memkv_kernel,
        out_shape=[out, out],
        compiler_params=pltpu.CompilerParams(vmem_limit_bytes=VMEM_LIMIT),
        name="memkv",
    )(mem2d, g, w_bf16)


def _memattn_kernel(q_ref, mk_ref, mv_ref, o_ref, *, nb, mem_minor):
    head_of_lane = lax.broadcasted_iota(jnp.int32, q_ref.shape[1:], 1) >> 6
    qs = [q_ref[b] for b in range(nb)]
    mks = [mk_ref[b].astype(BF16) for b in range(nb)]
    mvs = [mv_ref[b].astype(BF16) for b in range(nb)]
    outs = [jnp.zeros(q_ref.shape[1:], F32) for _ in range(nb)]
    for h in range(MEM_W // HEAD_DIM):
        mine = head_of_lane == h
        qk = _dot if mem_minor else _dot_nt
        pv = _dot_nt if mem_minor else _dot
        s = [qk(jnp.where(mine, qs[b], 0.0), mks[b]) for b in range(nb)]
        pexp = [jnp.exp(s[b] - jnp.max(s[b], axis=-1, keepdims=True)) for b in range(nb)]
        l = [jnp.sum(pexp[b], axis=-1, keepdims=True) for b in range(nb)]
        outs = [jnp.where(mine, pv(pexp[b], mvs[b]) / l[b], outs[b]) for b in range(nb)]
    for b in range(nb):
        o_ref[b] = outs[b]


def _memattn(q, mk, mv, tm, nb, mem_minor):
    b, s, _ = q.shape
    qspec = pl.BlockSpec((nb, tm, MEM_W), lambda i, j: (i, j, 0))
    mspec = pl.BlockSpec((nb,) + mk.shape[1:], lambda i, j: (i, 0, 0))
    return pl.pallas_call(
        functools.partial(_memattn_kernel, nb=nb, mem_minor=mem_minor),
        grid=(b // nb, s // tm),
        in_specs=[qspec, mspec, mspec],
        out_specs=qspec,
        out_shape=jax.ShapeDtypeStruct(q.shape, F32),
        compiler_params=_cparams(("arbitrary", "arbitrary")),
        name="memattn",
    )(q, mk, mv)


def _out_kernel(oa_ref, ob_ref, om_ref, gate_ref, x_ref, w_ref, g_ref, y_ref, *, split_pairs):
    acc = x_ref[...]

    def add(acc, o, c0):
        width = o.shape[1]
        gate = gate_ref[:, c0:c0 + width]
        return acc + _dot(o * (gate * jax.nn.sigmoid(gate)), w_ref[c0:c0 + width, :])

    if split_pairs:
        for p in range(ATT_W // LANES):
            acc = add(acc, oa_ref[p], p * LANES)
    else:
        acc = add(acc, oa_ref[...], 0)
    acc = add(acc, ob_ref[...], ATT_W)
    acc = add(acc, om_ref[...], ATT_W + RWKV_W)
    ms = jnp.mean(acc * acc, axis=-1, keepdims=True)
    y_ref[...] = (acc * lax.rsqrt(ms + NORM_EPS)) * g_ref[...]


def _out(oa, ob, om, gate, x2d, w_bf16, g, tm, split_pairs):
    rows = x2d.shape[0]
    row = lambda w: pl.BlockSpec((tm, w), lambda i: (i, 0))
    oa_spec = pl.BlockSpec((3, tm, LANES), lambda i: (0, i, 0)) if split_pairs else row(ATT_W)
    return pl.pallas_call(
        functools.partial(_out_kernel, split_pairs=split_pairs),
        grid=(rows // tm,),
        in_specs=[oa_spec, row(RWKV_W), row(MEM_W), row(MIX_W), row(D_MODEL),
                  pl.BlockSpec((MIX_W, D_MODEL), lambda i: (0, 0)),
                  pl.BlockSpec((1, D_MODEL), lambda i: (0, 0))],
        out_specs=row(D_MODEL),
        out_shape=jax.ShapeDtypeStruct((rows, D_MODEL), F32),
        compiler_params=_cparams(("arbitrary",)),
        name="outproj",
    )(oa, ob, om, gate, x2d, w_bf16, g)


def kernel(x_prompt, x_sample, cache_win_k, cache_win_v, state_rwkv, state_rwkv_shift, cache_mem_k, cache_mem_v, mem_prompt, norm_in, w_in, rwkv_mu, rwkv_w0, rwkv_w2, rwkv_a0, rwkv_a2, rwkv_k_k, rwkv_k_a, rwkv_r_k, rwkv_lnx_g, rwkv_lnx_b, norm_mem, w_mem_kv, w_out, norm_final):
    B, S, _ = x_prompt.shape
    DB, T, _ = x_sample.shape
    depth = w_in.shape[0]
    assert depth == 1 and S % ATT_TILE == 0 and cache_win_k.shape[2] == max(DILATIONS) * BAND
    l = 0
    past_len = S

    half = ROPE_DIM // 2
    inv_freq = ROPE_THETA ** (-jnp.arange(half, dtype=F32) / half)
    lane = jnp.arange(LANES)
    invf = jnp.where((lane % HEAD_DIM) < ROPE_DIM, inv_freq[lane % half], 0.0).reshape(1, LANES)

    row = lambda t: t.reshape(1, -1)
    w_in_b = w_in[l].astype(BF16)
    w_out_b = w_out[l].astype(BF16)
    w_kv_b = w_mem_kv[l].astype(BF16)
    zero = jnp.zeros((LORA_W, RWKV_W), F32)
    w2a2 = jnp.concatenate([jnp.concatenate([rwkv_w2[l], zero], axis=1),
                            jnp.concatenate([zero, rwkv_a2[l]], axis=1)], axis=0)
    w2a2_hi = w2a2.astype(BF16)
    w2a2_lo = (w2a2 - w2a2_hi.astype(F32)).astype(BF16)
    rw = (row(rwkv_mu[l]), row(rwkv_w0[l]), row(rwkv_a0[l]), w2a2_hi, w2a2_lo, row(rwkv_k_k[l]),
          row(rwkv_k_a[l]), row(rwkv_r_k[l]), row(rwkv_lnx_g[l]), row(rwkv_lnx_b[l]))

    xp = x_prompt.reshape(B * S, D_MODEL)
    pos_p = jnp.tile(jnp.arange(S, dtype=F32), B).reshape(B * S, 1)
    q3, k3, v3, zb, qm, gate = _proj(xp, pos_p, row(norm_in[l]), invf, w_in_b, 512, True)
    oa3 = _attn_prompt(q3, k3, v3, B, S)
    ob, st_p = _rwkv(zb.reshape(B, S, SHIFT_W), jnp.zeros((B, 1, SHIFT_W), F32),
                     jnp.zeros((B, 3, LANES, LANES), F32), *rw, nb=B, groups=2)
    mk, mv = _memkv(mem_prompt.reshape(B * N_MEM, D_MODEL), row(norm_mem[l]), w_kv_b)
    mk = mk.reshape(B, N_MEM, MEM_W)
    mv = mv.reshape(B, N_MEM, MEM_W)
    om = _memattn(qm.reshape(B, S, MEM_W), mk, mv, 1024, 1, False)
    y_p = _out(oa3, ob.reshape(B * S, RWKV_W), om.reshape(B * S, MEM_W), gate, xp, w_out_b,
               row(norm_final), 512, True)

    win = min(max(DILATIONS) * BAND, S)
    tail = lambda t3: jnp.transpose(t3.reshape(3, B, S, LANES)[:, :, S - win:], (1, 2, 0, 3)).reshape(
        1, B, win, ATT_W // HEAD_DIM, HEAD_DIM)
    heads = lambda t, n: t.reshape(1, t.shape[0], t.shape[1], n, HEAD_DIM)

    xs = x_sample.reshape(DB * T, D_MODEL)
    pos_s = jnp.tile(past_len + jnp.arange(T, dtype=F32), DB).reshape(DB * T, 1)
    qs, ks, vs, zbs, qms, gates = _proj(xs, pos_s, row(norm_in[l]), invf, w_in_b, DB * T, False)
    n_past = cache_win_k.shape[2]
    minor = lambda c: jnp.transpose(c, (0, 2, 3, 1)).reshape(c.shape[0], c.shape[2] * c.shape[3], c.shape[1])
    oas = _attn_sample(qs.reshape(DB, T, ATT_W), ks.reshape(DB, T, ATT_W), vs.reshape(DB, T, ATT_W),
                       minor(cache_win_k[l]), minor(cache_win_v[l]))
    zbs3 = zbs.reshape(DB, T, SHIFT_W)
    obs, st_s = _rwkv(zbs3, state_rwkv_shift[l].reshape(DB, 1, SHIFT_W), _state_to_pairs(state_rwkv[l]),
                      *rw, nb=4, groups=1)
    obs = obs[:, :T]
    oms = _memattn(qms.reshape(DB, T, MEM_W), minor(cache_mem_k[l]), minor(cache_mem_v[l]), T, 8, True)
    y_s = _out(oas.reshape(DB * T, ATT_W), obs.reshape(DB * T, RWKV_W), oms.reshape(DB * T, MEM_W), gates, xs,
               w_out_b, row(norm_final), DB * T, False)

    return (y_p.reshape(B, S, D_MODEL), y_s.reshape(DB, T, D_MODEL),
            tail(k3), tail(v3),
            _pairs_to_state(st_p)[None], zb.reshape(B, S, SHIFT_W)[:, -1][None],
            heads(mk, MEM_W // HEAD_DIM), heads(mv, MEM_W // HEAD_DIM),
            heads(ks.reshape(DB, T, ATT_W), ATT_W // HEAD_DIM), heads(vs.reshape(DB, T, ATT_W), ATT_W // HEAD_DIM),
            _pairs_to_state(st_s)[None], zbs3[:, -1][None])
```

```python
import functools

import jax
import jax.numpy as jnp
from jax import lax
from jax.experimental import pallas as pl
from jax.experimental.pallas import tpu as pltpu

F32 = jnp.float32
BF16 = jnp.bfloat16

D_MODEL = 1024
HEAD_DIM = 64
ATT_W = 384
RWKV_W = 384
MEM_W = 256
MIX_W = 1024
LORA_W = 64
SHIFT_W = 3 * RWKV_W + 2 * LORA_W
N_MEM = 256
ROPE_DIM = 16
ROPE_THETA = 500000.0
NORM_EPS = 1e-6
LNX_EPS = 64e-5
DILATIONS = (1, 4, 16)
BAND = 128
ATT_TILE = BAND * max(DILATIONS)
LANES = 128
RWKV_CHUNK = 64
PROJ_TILE = 512
NEG = -1e30
Q_SCALE = HEAD_DIM ** -0.5 * 1.4426950408889634
VMEM_LIMIT = 56 * 1024 * 1024


def _cparams(sem):
    return pltpu.CompilerParams(dimension_semantics=sem, vmem_limit_bytes=VMEM_LIMIT)


def _dot(a, b):
    return jnp.dot(a.astype(BF16), b.astype(BF16), preferred_element_type=F32)


def _dot_nt(a, b):
    return lax.dot_general(a.astype(BF16), b.astype(BF16), (((1,), (1,)), ((), ())),
                           preferred_element_type=F32)


def _dot_tn(a, b):
    return lax.dot_general(a.astype(BF16), b.astype(BF16), (((0,), (0,)), ((), ())),
                           preferred_element_type=F32)


def _dot_f32(a, b):
    return jnp.dot(a, b, preferred_element_type=F32, precision=lax.Precision.HIGHEST)


def _half0(shape):
    return (lax.broadcasted_iota(jnp.int32, shape, len(shape) - 1) & 64) == 0


def _rope_table_kernel(pos_row_ref, pos_base_ref, invf_ref, cr_ref, sr_ref, cb_ref, sb_ref):
    ang_r = pos_row_ref[...] * invf_ref[...]
    cr_ref[...] = jnp.cos(ang_r)
    sr_ref[...] = jnp.sin(ang_r)
    ang_b = pos_base_ref[...] * invf_ref[...]
    cb_ref[...] = jnp.cos(ang_b)
    sb_ref[...] = jnp.sin(ang_b)


def _rope_tables(pos_row, pos_base, invf):
    tab = lambda n: jax.ShapeDtypeStruct((n, LANES), F32)
    return pl.pallas_call(
        _rope_table_kernel,
        out_shape=[tab(pos_row.shape[0]), tab(pos_row.shape[0]), tab(pos_base.shape[0]), tab(pos_base.shape[0])],
        name="rope_tables",
    )(pos_row, pos_base, invf)


def _proj_kernel(x_ref, cr_ref, sr_ref, cb_ref, sb_ref, g_ref, w_ref, q_ref, k_ref, v_ref, zb_ref, qm_ref, gate_ref,
                 *, split_pairs, tiles_per_seq, base_offset):
    tm = x_ref.shape[0]
    base = base_offset + lax.rem(pl.program_id(0), tiles_per_seq)
    cb = cb_ref[pl.ds(base, 1), :]
    sb = sb_ref[pl.ds(base, 1), :]
    second = (lax.broadcasted_iota(jnp.int32, (1, LANES), 1) & 8) != 0

    def put(ref, p, rows, val):
        if split_pairs:
            ref[p, rows, :] = val
        else:
            ref[rows, p * LANES:(p + 1) * LANES] = val

    nsplit = 2 if tm % 16 == 0 else 1
    hm = tm // nsplit
    for part in range(nsplit):
        rows = pl.ds(part * hm, hm)
        x = x_ref[rows, :]
        ms = jnp.mean(x * x, axis=-1, keepdims=True)
        h = ((x * lax.rsqrt(ms + NORM_EPS)) * g_ref[...]).astype(BF16)
        cr = cr_ref[rows, :]
        sr = sr_ref[rows, :]
        cos = cb * cr - sb * sr
        sin = sb * cr + cb * sr
        sin_up = jnp.where(second, sin, 0.0)
        sin_dn = jnp.where(second, 0.0, -sin)

        def rope(t):
            return t * cos + pltpu.roll(t, 8, 1) * sin_up + pltpu.roll(t, LANES - 8, 1) * sin_dn

        qkv = jnp.dot(h, w_ref[:, :3 * ATT_W], preferred_element_type=F32)
        for p in range(ATT_W // LANES):
            put(q_ref, p, rows, rope(qkv[:, p * LANES:(p + 1) * LANES]) * Q_SCALE)
            put(k_ref, p, rows, rope(qkv[:, ATT_W + p * LANES:ATT_W + (p + 1) * LANES]))
            put(v_ref, p, rows, qkv[:, 2 * ATT_W + p * LANES:2 * ATT_W + (p + 1) * LANES])
        c0 = 3 * ATT_W
        zb_ref[rows, :] = jnp.dot(h, w_ref[:, c0:c0 + SHIFT_W], preferred_element_type=F32)
        c0 += SHIFT_W
        qm_ref[rows, :] = (jnp.dot(h, w_ref[:, c0:c0 + MEM_W], preferred_element_type=F32) * 0.125).astype(
            qm_ref.dtype)
        c0 += MEM_W
        gate_ref[rows, :] = jnp.dot(h, w_ref[:, c0:c0 + MIX_W], preferred_element_type=F32).astype(gate_ref.dtype)


def _proj(x2d, cr, sr, cb, sb, g, w_bf16, tm, split_pairs, tiles_per_seq, base_offset, act):
    rows = x2d.shape[0]
    in_w = w_bf16.shape[1]
    if split_pairs:
        qkv_shape = jax.ShapeDtypeStruct((3, rows, LANES), F32)
        qkv_spec = pl.BlockSpec((3, tm, LANES), lambda i: (0, i, 0))
    else:
        qkv_shape = jax.ShapeDtypeStruct((rows, ATT_W), F32)
        qkv_spec = pl.BlockSpec((tm, ATT_W), lambda i: (i, 0))
    row = lambda w: pl.BlockSpec((tm, w), lambda i: (i, 0))
    full = lambda a: pl.BlockSpec(a.shape, lambda i: (0, 0))
    return pl.pallas_call(
        functools.partial(_proj_kernel, split_pairs=split_pairs, tiles_per_seq=tiles_per_seq,
                          base_offset=base_offset),
        grid=(rows // tm,),
        in_specs=[row(D_MODEL), full(cr), full(sr), full(cb), full(sb), full(g), full(w_bf16)],
        out_specs=[qkv_spec, qkv_spec, qkv_spec, row(SHIFT_W), row(MEM_W), row(MIX_W)],
        out_shape=[qkv_shape, qkv_shape, qkv_shape,
                   jax.ShapeDtypeStruct((rows, SHIFT_W), F32),
                   jax.ShapeDtypeStruct((rows, MEM_W), act),
                   jax.ShapeDtypeStruct((rows, MIX_W), act)],
        compiler_params=_cparams(("arbitrary",)),
        name="proj",
    )(x2d, cr, sr, cb, sb, g, w_bf16)


def _attn_blocks(npairs, d, q_ref, kc_ref, vc_ref, blocks):
    h0 = _half0((BAND, LANES))
    h0k = _half0((2 * BAND, LANES))
    units = [(b, p) for b in range(len(blocks)) for p in range(npairs)]
    heads = [(i, hh) for i in range(len(units)) for hh in range(2)]
    row = [pl.ds(blk[2], BAND, stride=d) for blk in blocks]
    prow = [pl.ds(blk[3], BAND, stride=d) for blk in blocks]
    q = [q_ref[p, row[b], :] for b, p in units]
    kcat = [jnp.concatenate([blocks[b][0][p, prow[b], :], kc_ref[p, row[b], :]], axis=0).astype(BF16)
            for b, p in units]
    vcat = [jnp.concatenate([blocks[b][1][p, prow[b], :], vc_ref[p, row[b], :]], axis=0) for b, p in units]
    s = [_dot_nt(jnp.where(h0 if hh == 0 else jnp.logical_not(h0), q[i], 0.0), kcat[i]) + blocks[units[i][0]][4]
         for i, hh in heads]
    m = [jnp.max(s[j], axis=-1, keepdims=True) for j in range(len(heads))]
    pexp = [jnp.exp2(s[j] - m[j]).astype(BF16) for j in range(len(heads))]
    res = [jnp.dot(pexp[j], jnp.where(h0k if hh == 0 else jnp.logical_not(h0k), vcat[i], 1.0).astype(BF16),
                   preferred_element_type=F32) for j, (i, hh) in enumerate(heads)]
    lsum = [pltpu.roll(jnp.where(h0, res[2 * i + 1], res[2 * i]), HEAD_DIM, 1) for i in range(len(units))]
    outs = [[] for _ in blocks]
    for i, (b, p) in enumerate(units):
        outs[b].append((jnp.where(h0, res[2 * i], res[2 * i + 1]) / lsum[i],
                        jnp.where(h0, m[2 * i], m[2 * i + 1]) + jnp.log2(lsum[i])))
    return outs


def _attn_kernel(q_ref, kc_ref, kp_ref, vc_ref, vp_ref, o_ref, o_sc, lse_sc):
    has_prev = pl.program_id(1) > 0
    qi = lax.broadcasted_iota(jnp.int32, (BAND, 2 * BAND), 0)
    kj = lax.broadcasted_iota(jnp.int32, (BAND, 2 * BAND), 1)
    in_cur = (kj >= BAND) & (kj - BAND <= qi)
    in_prev = (kj < BAND) & (kj >= qi)
    bias_full = jnp.where(in_cur | in_prev, 0.0, NEG).astype(F32)
    bias_first = jnp.where(in_cur | (in_prev & has_prev), 0.0, NEG).astype(F32)
    npairs = q_ref.shape[0]
    nblk = ATT_TILE // BAND
    per_iter = 2

    def desc(d, u, first):
        span = BAND * d
        if first:
            return (kp_ref, vp_ref, u, ATT_TILE - span + u, bias_first)
        qs = (u >> (d.bit_length() - 1)) * span + (u & (d - 1))
        return (kc_ref, vc_ref, qs, qs - span, bias_full)

    def merge(outs, start):
        row = pl.ds(start, BAND)
        for p, (o1, lse1) in enumerate(outs):
            os_ = [o1] + [o_sc[j, p, row, :] for j in range(len(DILATIONS) - 1)]
            ls_ = [lse1] + [lse_sc[j, p, row, :] for j in range(len(DILATIONS) - 1)]
            top = functools.reduce(jnp.maximum, ls_)
            ws = [jnp.exp2(l - top) for l in ls_]
            num = functools.reduce(lambda a, b: a + b, [w * o for w, o in zip(ws, os_)])
            o_ref[p, row, :] = (num / functools.reduce(lambda a, b: a + b, ws)).astype(o_ref.dtype)

    def run(j, d, u0, first):
        blocks = [desc(d, u0 + i, first) for i in range(per_iter)]
        for blk, outs in zip(blocks, _attn_blocks(npairs, d, q_ref, kc_ref, vc_ref, blocks)):
            if j < 0:
                merge(outs, blk[2] if first else pl.multiple_of(blk[2], BAND))
            else:
                for p, (o, lse) in enumerate(outs):
                    o_sc[j, p, pl.ds(blk[2], BAND, stride=d), :] = o
                    lse_sc[j, p, pl.ds(blk[2], BAND, stride=d), :] = lse

    def sweep(j, d):
        n_first = d
        if n_first % per_iter == 0:
            lax.fori_loop(0, n_first // per_iter, lambda i, c: (run(j, d, i * per_iter, True), c)[1], 0)
            lo = n_first
        else:
            assert n_first == 1 and per_iter == 2
            blocks = [desc(d, 0, True), desc(d, 1, False)]
            for blk, outs in zip(blocks, _attn_blocks(npairs, d, q_ref, kc_ref, vc_ref, blocks)):
                merge(outs, blk[2])
            lo = per_iter
        if lo < nblk:
            lax.fori_loop(lo // per_iter, nblk // per_iter, lambda i, c: (run(j, d, i * per_iter, False), c)[1], 0)

    for j, d in enumerate(DILATIONS[1:]):
        sweep(j, d)
    sweep(-1, DILATIONS[0])


def _attn_prompt(q3, k3, v3, batch, seq):
    nt = seq // ATT_TILE
    cur = pl.BlockSpec((3, ATT_TILE, LANES), lambda b, i: (0, b * nt + i, 0))
    prev = pl.BlockSpec((3, ATT_TILE, LANES), lambda b, i: (0, b * nt + jnp.maximum(i - 1, 0), 0))
    return pl.pallas_call(
        _attn_kernel,
        grid=(batch, nt),
        in_specs=[cur, cur, prev, cur, prev],
        out_specs=cur,
        out_shape=jax.ShapeDtypeStruct(q3.shape, BF16),
        scratch_shapes=[pltpu.VMEM((len(DILATIONS) - 1, 3, ATT_TILE, LANES), F32),
                        pltpu.VMEM((len(DILATIONS) - 1, 3, ATT_TILE, LANES), F32)],
        compiler_params=_cparams(("arbitrary", "arbitrary")),
        name="attn_prompt",
    )(q3, k3, k3, v3, v3)


def _attn_sample_kernel(q_ref, kn_ref, vn_ref, kc_ref, vc_ref, o_ref, *, n_past, t_new):
    nh = ATT_W // HEAD_DIM
    q = q_ref[0]
    lane = lax.broadcasted_iota(jnp.int32, q.shape, 1)
    head_of_lane = lane >> 6
    qst = jnp.concatenate([jnp.where(head_of_lane == h, q, 0.0) for h in range(nh)], axis=0)

    def count(delta):
        c = jnp.zeros(delta.shape, F32)
        for d in DILATIONS:
            ok = (delta >= 0) & (delta <= BAND * d) & ((delta & (d - 1)) == 0)
            c = c + jnp.where(ok, 1.0, 0.0)
        return c

    rows = nh * t_new
    t_past = lax.broadcasted_iota(jnp.int32, (rows, n_past), 0) & (t_new - 1)
    cnt_past = count(n_past + t_past - lax.broadcasted_iota(jnp.int32, (rows, n_past), 1))
    t_n = lax.broadcasted_iota(jnp.int32, (rows, t_new), 0) & (t_new - 1)
    cnt_new = count(t_n - lax.broadcasted_iota(jnp.int32, (rows, t_new), 1))

    s_past = jnp.where(cnt_past > 0, _dot(qst, kc_ref[0]), NEG)
    s_new = jnp.where(cnt_new > 0, _dot_nt(qst, kn_ref[0]), NEG)
    m = jnp.maximum(jnp.max(s_past, axis=-1, keepdims=True), jnp.max(s_new, axis=-1, keepdims=True))
    p_past = cnt_past * jnp.exp2(s_past - m)
    p_new = cnt_new * jnp.exp2(s_new - m)
    l = jnp.sum(p_past, axis=-1, keepdims=True) + jnp.sum(p_new, axis=-1, keepdims=True)
    o = (_dot_nt(p_past, vc_ref[0]) + _dot(p_new, vn_ref[0])) / l
    out = jnp.zeros(q.shape, F32)
    for h in range(nh):
        out = jnp.where(head_of_lane == h, o[h * t_new:(h + 1) * t_new, :], out)
    o_ref[0] = out.astype(o_ref.dtype)


def _attn_sample(q, k_new, v_new, k_cache, v_cache):
    db, t_new, _ = q.shape
    n_past = k_cache.shape[2]
    new = pl.BlockSpec((1, t_new, ATT_W), lambda b: (b, 0, 0))
    cache = pl.BlockSpec((1, ATT_W, n_past), lambda b: (b, 0, 0))
    return pl.pallas_call(
        functools.partial(_attn_sample_kernel, n_past=n_past, t_new=t_new),
        grid=(db,),
        in_specs=[new, new, new, cache, cache],
        out_specs=new,
        out_shape=jax.ShapeDtypeStruct(q.shape, F32),
        compiler_params=_cparams(("arbitrary",)),
        name="attn_sample",
    )(q, k_new, v_new, k_cache, v_cache)


def _seg_sum(x, h0):
    s0 = jnp.sum(jnp.where(h0, x, 0.0), axis=-1, keepdims=True)
    s1 = jnp.sum(jnp.where(h0, 0.0, x), axis=-1, keepdims=True)
    return jnp.where(h0, s0, s1)


def _split_bf16(x):
    hi = x.astype(BF16)
    return hi, (x - hi.astype(F32)).astype(BF16)


def _rwkv_kernel(z_ref, sh0_ref, st0_ref, mu_ref, w0_ref, a0_ref, whi_ref, wlo_ref, kk_ref, ka_ref, rk_ref,
                 lng_ref, lnb_ref, o_ref, st_ref, st_sc, prev_sc, *, nb, groups, n_valid):
    C = RWKV_CHUNK
    R = groups * C
    npairs = RWKV_W // LANES

    @pl.when(pl.program_id(1) == 0)
    def _():
        st_sc[...] = st0_ref[...]
        prev_sc[...] = sh0_ref[...]

    si = lax.broadcasted_iota(jnp.int32, (2 * C, 2 * C), 0)
    sj = lax.broadcasted_iota(jnp.int32, (2 * C, 2 * C), 1)
    same_head = (si >= C) == (sj >= C)
    ti2 = si & (C - 1)
    tj2 = sj & (C - 1)
    incl = same_head & (tj2 <= ti2)
    strict = same_head & (tj2 < ti2)
    eye = jnp.where(si == sj, 1.0, 0.0).astype(F32)
    levels = []
    s = 2
    while s < C:
        sh = s.bit_length() - 1
        levels.append(((ti2 >> (sh + 1)) == (tj2 >> (sh + 1))) & (((ti2 >> sh) & 1) == 1) & (((tj2 >> sh) & 1) == 0))
        s *= 2
    first_level = (ti2 >> 1) == (tj2 >> 1)

    ri = lax.broadcasted_iota(jnp.int32, (R, R), 0)
    rj = lax.broadcasted_iota(jnp.int32, (R, R), 1)
    cs = C.bit_length() - 1
    tri = jnp.where(((ri >> cs) == (rj >> cs)) & (rj <= ri), 1.0, 0.0).astype(BF16)
    rowid = lax.broadcasted_iota(jnp.int32, (R, 1), 0)
    h0 = _half0((C, LANES))
    zero_blk = jnp.zeros((2 * C, LANES), F32)

    def stack(x):
        return jnp.concatenate([jnp.where(h0, x, 0.0), jnp.where(h0, 0.0, x)], axis=0)

    seqs = []
    for b in range(nb):
        z = z_ref[b]
        z_prev = jnp.where(rowid == 0, prev_sc[b], pltpu.roll(z, 1, 0))
        prev_sc[b] = z_ref[b, pl.ds(R - 1, 1), :]
        zs = z + (z_prev - z) * mu_ref[...]
        if n_valid < R:
            zs = jnp.where(rowid < n_valid, zs, 0.0)
        lat = zs[:, 3 * RWKV_W:]
        lat_hi, lat_lo = _split_bf16(jnp.where(_half0(lat.shape), jnp.tanh(lat), lat))
        dot = functools.partial(jnp.dot, preferred_element_type=F32)
        lora = dot(lat_hi, whi_ref[...]) + dot(lat_lo, whi_ref[...]) + dot(lat_hi, wlo_ref[...])
        w = -jax.nn.softplus(-(w0_ref[...] + lora[:, :RWKV_W])) - 0.5
        lw = -jnp.exp(w)
        if n_valid < R:
            lw = jnp.where(rowid < n_valid, lw, 0.0)
        a = jax.nn.sigmoid(a0_ref[...] + lora[:, RWKV_W:])
        lw_hi, lw_lo = _split_bf16(lw)
        cum = dot(tri, lw_hi) + dot(tri, lw_lo)
        seqs.append((zs, lw, a, cum))

    chains = [(b, g, p) for g in range(groups) for b in range(nb) for p in range(npairs)]
    pre = {}
    for (b, g, p) in chains:
        zs, lw, a, cum = seqs[b]
        rows = slice(g * C, (g + 1) * C)
        sl = slice(p * LANES, (p + 1) * LANES)
        r = zs[rows, sl]
        k = zs[rows, RWKV_W + p * LANES:RWKV_W + (p + 1) * LANES]
        v = zs[rows, 2 * RWKV_W + p * LANES:2 * RWKV_W + (p + 1) * LANES]
        ap = a[rows, sl]
        kk = k * kk_ref[:, sl]
        kk = kk / jnp.maximum(jnp.sqrt(_seg_sum(kk * kk, h0)), 1e-12)
        k = k * (1.0 + (ap - 1.0) * ka_ref[:, sl])
        bb = kk * ap
        L = cum[rows, sl]
        l_end = L[C - 1:C, :]
        e_out = jnp.exp(-L)
        e_end = jnp.exp(l_end - L)
        rt = stack(r * jnp.exp(L))
        kkh = stack(kk * jnp.exp(L - lw[rows, sl]))
        pre[(b, g, p)] = dict(
            left=jnp.concatenate([rt, kkh], axis=0).astype(BF16),
            right=jnp.concatenate([stack(k * e_out), stack(-(bb * e_out))], axis=0).astype(BF16),
            ends=jnp.concatenate([stack(k * e_end), stack(-(bb * e_end))], axis=0).astype(BF16),
            rt=rt, kkh=kkh, v2=stack(v), decay=jnp.exp(l_end),
            bonus=_seg_sum(r * k * rk_ref[:, sl], h0) * v)

    for c in chains:
        d = pre[c]
        aa = lax.dot_general(d["left"], d["right"], (((1,), (1,)), ((), ())), preferred_element_type=F32)
        d["a_r"] = jnp.concatenate([jnp.where(incl, aa[:2 * C, :2 * C], 0.0),
                                    jnp.where(incl, aa[:2 * C, 2 * C:], 0.0)], axis=1).astype(BF16)
        d["a_kk"] = jnp.where(strict, aa[2 * C:, :2 * C], 0.0)
        d["n_kb"] = jnp.where(strict, aa[2 * C:, 2 * C:], 0.0)
        d["t"] = eye + jnp.where(first_level, d["n_kb"], 0.0)
    for off in levels:
        for c in chains:
            d = pre[c]
            d["tn"] = _dot(d["t"], jnp.where(off, d["n_kb"], 0.0))
        for c in chains:
            d = pre[c]
            d["t"] = d["t"] + _dot(d["tn"], d["t"])
    for c in chains:
        d = pre[c]
        d["akv"] = _dot(d["a_kk"], d["v2"])
    for c in chains:
        d = pre[c]
        tr = _dot(d["t"], jnp.concatenate([d["kkh"], d["akv"]], axis=1))
        kq, u0 = tr[:, :LANES], tr[:, LANES:]
        d["big"] = jnp.concatenate([jnp.concatenate([d["v2"], zero_blk], axis=1),
                                    jnp.concatenate([u0, kq], axis=1)], axis=0).astype(BF16)
    for c in chains:
        d = pre[c]
        yr = jnp.dot(d["a_r"], d["big"], preferred_element_type=F32)
        d["y0"] = yr[:, :LANES]
        d["rq"] = (d["rt"] + yr[:, LANES:]).astype(BF16)
    for c in chains:
        d = pre[c]
        dg = lax.dot_general(d["big"], d["ends"], (((0,), (0,)), ((), ())), preferred_element_type=F32)
        d["dd"] = dg[:LANES]
        d["gm"] = dg[LANES:].astype(BF16)

    state = {(b, p): st_sc[b, p] for b in range(nb) for p in range(npairs)}
    for (b, g, p) in chains:
        d = pre[(b, g, p)]
        st = state[(b, p)]
        st_b = st.astype(BF16)
        d["y2"] = lax.dot_general(d["rq"], st_b, (((1,), (1,)), ((), ())), preferred_element_type=F32) + d["y0"]
        state[(b, p)] = st * d["decay"] + jnp.dot(st_b, d["gm"], preferred_element_type=F32) + d["dd"]
    for (b, g, p) in chains:
        d = pre[(b, g, p)]
        y2 = d["y2"]
        y = y2[:C] + y2[C:]
        sl = slice(p * LANES, (p + 1) * LANES)
        mean = _seg_sum(y, h0) * (1.0 / HEAD_DIM)
        yc = y - mean
        var = _seg_sum(yc * yc, h0) * (1.0 / HEAD_DIM)
        yn = yc * lax.rsqrt(var + LNX_EPS) * lng_ref[:, sl] + lnb_ref[:, sl]
        o_ref[b, g * C:(g + 1) * C, sl] = (yn + d["bonus"]).astype(o_ref.dtype)
    for (b, p), st in state.items():
        st_sc[b, p] = st

    @pl.when(pl.program_id(1) == pl.num_programs(1) - 1)
    def _():
        st_ref[...] = st_sc[...]


def _rwkv(z, shift0, st0, mu, w0, a0, w_hi, w_lo, k_k, k_a, r_k, ln_g, ln_b, nb, groups):
    b, n_valid, _ = z.shape
    rows = groups * RWKV_CHUNK
    if n_valid % rows:
        assert n_valid < rows
        z = jnp.pad(z, ((0, 0), (0, rows - n_valid), (0, 0)))
    s = z.shape[1]
    n_valid = min(n_valid, rows)
    vec = lambda w: pl.BlockSpec((1, w), lambda i, j: (0, 0))
    st_spec = pl.BlockSpec((nb, 3, LANES, LANES), lambda i, j: (i, 0, 0, 0))
    wspec = pl.BlockSpec((LANES, 2 * RWKV_W), lambda i, j: (0, 0))
    return pl.pallas_call(
        functools.partial(_rwkv_kernel, nb=nb, groups=groups, n_valid=n_valid),
        grid=(b // nb, s // rows),
        in_specs=[pl.BlockSpec((nb, rows, SHIFT_W), lambda i, j: (i, j, 0)),
                  pl.BlockSpec((nb, 1, SHIFT_W), lambda i, j: (i, 0, 0)),
                  st_spec,
                  vec(SHIFT_W), vec(RWKV_W), vec(RWKV_W), wspec, wspec,
                  vec(RWKV_W), vec(RWKV_W), vec(RWKV_W), vec(RWKV_W), vec(RWKV_W)],
        out_specs=[pl.BlockSpec((nb, rows, RWKV_W), lambda i, j: (i, j, 0)), st_spec],
        out_shape=[jax.ShapeDtypeStruct((b, s, RWKV_W), BF16),
                   jax.ShapeDtypeStruct((b, 3, LANES, LANES), F32)],
        scratch_shapes=[pltpu.VMEM((nb, 3, LANES, LANES), F32), pltpu.VMEM((nb, 1, SHIFT_W), F32)],
        compiler_params=_cparams(("arbitrary", "arbitrary")),
        name="rwkv",
    )(z, shift0, st0, mu, w0, a0, w_hi, w_lo, k_k, k_a, r_k, ln_g, ln_b)


def _state_to_pairs(st):
    b = st.shape[0]
    st = st.reshape(b, 3, 2, HEAD_DIM, HEAD_DIM)
    zero = jnp.zeros_like(st[:, :, 0])
    top = jnp.concatenate([st[:, :, 0], zero], axis=-1)
    bot = jnp.concatenate([zero, st[:, :, 1]], axis=-1)
    return jnp.concatenate([top, bot], axis=-2)


def _pairs_to_state(sp):
    b = sp.shape[0]
    return jnp.stack([sp[:, :, :HEAD_DIM, :HEAD_DIM], sp[:, :, HEAD_DIM:, HEAD_DIM:]], axis=2).reshape(
        b, 6, HEAD_DIM, HEAD_DIM)


def _memkv_kernel(x_ref, g_ref, w_ref, k_ref, v_ref):
    x = x_ref[...]
    ms = jnp.mean(x * x, axis=-1, keepdims=True)
    h = ((x * lax.rsqrt(ms + NORM_EPS)) * g_ref[...]).astype(BF16)
    k_ref[...] = jnp.dot(h, w_ref[:, :MEM_W], preferred_element_type=F32)
    v_ref[...] = jnp.dot(h, w_ref[:, MEM_W:], preferred_element_type=F32)


def _memkv(mem2d, g, w_bf16):
    rows = mem2d.shape[0]
    out = jax.ShapeDtypeStruct((rows, MEM_W), F32)
    return pl.pallas_call(
        _memkv_kernel,
        out_shape=[out, out],
        compiler_params=pltpu.CompilerParams(vmem_limit_bytes=VMEM_LIMIT),
        name="memkv",
    )(mem2d, g, w_bf16)


def _memattn_kernel(q_ref, mk_ref, mv_ref, o_ref, *, nb, mem_minor):
    head_of_lane = lax.broadcasted_iota(jnp.int32, q_ref.shape[1:], 1) >> 6
    qs = [q_ref[b] for b in range(nb)]
    mks = [mk_ref[b].astype(BF16) for b in range(nb)]
    mvs = [mv_ref[b].astype(BF16) for b in range(nb)]
    outs = [jnp.zeros(q_ref.shape[1:], F32) for _ in range(nb)]
    for h in range(MEM_W // HEAD_DIM):
        mine = head_of_lane == h
        qk = _dot if mem_minor else _dot_nt
        pv = _dot_nt if mem_minor else _dot
        s = [qk(jnp.where(mine, qs[b], 0.0), mks[b]) for b in range(nb)]
        pexp = [jnp.exp(s[b] - jnp.max(s[b], axis=-1, keepdims=True)) for b in range(nb)]
        l = [jnp.sum(pexp[b], axis=-1, keepdims=True) for b in range(nb)]
        outs = [jnp.where(mine, pv(pexp[b], mvs[b]) / l[b], outs[b]) for b in range(nb)]
    for b in range(nb):
        o_ref[b] = outs[b].astype(o_ref.dtype)


def _memattn(q, mk, mv, tm, nb, mem_minor, act):
    b, s, _ = q.shape
    qspec = pl.BlockSpec((nb, tm, MEM_W), lambda i, j: (i, j, 0))
    mspec = pl.BlockSpec((nb,) + mk.shape[1:], lambda i, j: (i, 0, 0))
    return pl.pallas_call(
        functools.partial(_memattn_kernel, nb=nb, mem_minor=mem_minor),
        grid=(b // nb, s // tm),
        in_specs=[qspec, mspec, mspec],
        out_specs=qspec,
        out_shape=jax.ShapeDtypeStruct(q.shape, act),
        compiler_params=_cparams(("arbitrary", "arbitrary")),
        name="memattn",
    )(q, mk, mv)


def _out_kernel(oa_ref, ob_ref, om_ref, gate_ref, x_ref, w_ref, g_ref, y_ref, *, split_pairs):
    acc = x_ref[...]

    def add(acc, o, c0):
        width = o.shape[1]
        gate = gate_ref[:, c0:c0 + width].astype(F32)
        return acc + _dot(o.astype(F32) * (gate * jax.nn.sigmoid(gate)), w_ref[c0:c0 + width, :])

    if split_pairs:
        for p in range(ATT_W // LANES):
            acc = add(acc, oa_ref[p], p * LANES)
    else:
        acc = add(acc, oa_ref[...], 0)
    acc = add(acc, ob_ref[...], ATT_W)
    acc = add(acc, om_ref[...], ATT_W + RWKV_W)
    ms = jnp.mean(acc * acc, axis=-1, keepdims=True)
    y_ref[...] = (acc * lax.rsqrt(ms + NORM_EPS)) * g_ref[...]


def _out(oa, ob, om, gate, x2d, w_bf16, g, tm, split_pairs):
    rows = x2d.shape[0]
    row = lambda w: pl.BlockSpec((tm, w), lambda i: (i, 0))
    oa_spec = pl.BlockSpec((3, tm, LANES), lambda i: (0, i, 0)) if split_pairs else row(ATT_W)
    return pl.pallas_call(
        functools.partial(_out_kernel, split_pairs=split_pairs),
        grid=(rows // tm,),
        in_specs=[oa_spec, row(RWKV_W), row(MEM_W), row(MIX_W), row(D_MODEL),
                  pl.BlockSpec((MIX_W, D_MODEL), lambda i: (0, 0)),
                  pl.BlockSpec((1, D_MODEL), lambda i: (0, 0))],
        out_specs=row(D_MODEL),
        out_shape=jax.ShapeDtypeStruct((rows, D_MODEL), F32),
        compiler_params=_cparams(("arbitrary",)),
        name="outproj",
    )(oa, ob, om, gate, x2d, w_bf16, g)


def kernel(x_prompt, x_sample, cache_win_k, cache_win_v, state_rwkv, state_rwkv_shift, cache_mem_k, cache_mem_v, mem_prompt, norm_in, w_in, rwkv_mu, rwkv_w0, rwkv_w2, rwkv_a0, rwkv_a2, rwkv_k_k, rwkv_k_a, rwkv_r_k, rwkv_lnx_g, rwkv_lnx_b, norm_mem, w_mem_kv, w_out, norm_final):
    B, S, _ = x_prompt.shape
    DB, T, _ = x_sample.shape
    depth = w_in.shape[0]
    assert depth == 1 and S % ATT_TILE == 0 and cache_win_k.shape[2] == max(DILATIONS) * BAND
    l = 0
    past_len = S

    half = ROPE_DIM // 2
    inv_freq = ROPE_THETA ** (-jnp.arange(half, dtype=F32) / half)
    lane = jnp.arange(LANES)
    invf = jnp.where((lane % HEAD_DIM) < ROPE_DIM, inv_freq[lane % half], 0.0).reshape(1, LANES)

    row = lambda t: t.reshape(1, -1)
    w_in_b = w_in[l].astype(BF16)
    w_out_b = w_out[l].astype(BF16)
    w_kv_b = w_mem_kv[l].astype(BF16)
    zero = jnp.zeros((LORA_W, RWKV_W), F32)
    w2a2 = jnp.concatenate([jnp.concatenate([rwkv_w2[l], zero], axis=1),
                            jnp.concatenate([zero, rwkv_a2[l]], axis=1)], axis=0)
    w2a2_hi = w2a2.astype(BF16)
    w2a2_lo = (w2a2 - w2a2_hi.astype(F32)).astype(BF16)
    rw = (row(rwkv_mu[l]), row(rwkv_w0[l]), row(rwkv_a0[l]), w2a2_hi, w2a2_lo, row(rwkv_k_k[l]),
          row(rwkv_k_a[l]), row(rwkv_r_k[l]), row(rwkv_lnx_g[l]), row(rwkv_lnx_b[l]))

    tiles = S // PROJ_TILE
    n_base = -(-(tiles + 1) // 8) * 8
    pos_base = jnp.where(jnp.arange(n_base) < tiles, jnp.arange(n_base) * PROJ_TILE, past_len)
    cr, sr, cb, sb = _rope_tables(jnp.arange(PROJ_TILE, dtype=F32).reshape(-1, 1),
                                  pos_base.astype(F32).reshape(-1, 1), invf)

    xp = x_prompt.reshape(B * S, D_MODEL)
    q3, k3, v3, zb, qm, gate = _proj(xp, cr, sr, cb, sb, row(norm_in[l]), w_in_b, PROJ_TILE, True, tiles, 0, BF16)
    oa3 = _attn_prompt(q3, k3, v3, B, S)
    ob, st_p = _rwkv(zb.reshape(B, S, SHIFT_W), jnp.zeros((B, 1, SHIFT_W), F32),
                     jnp.zeros((B, 3, LANES, LANES), F32), *rw, nb=B, groups=2)
    mk, mv = _memkv(mem_prompt.reshape(B * N_MEM, D_MODEL), row(norm_mem[l]), w_kv_b)
    mk = mk.reshape(B, N_MEM, MEM_W)
    mv = mv.reshape(B, N_MEM, MEM_W)
    om = _memattn(qm.reshape(B, S, MEM_W), mk, mv, 1024, 1, False, BF16)
    y_p = _out(oa3, ob.reshape(B * S, RWKV_W), om.reshape(B * S, MEM_W), gate, xp, w_out_b,
               row(norm_final), 512, True)

    win = min(max(DILATIONS) * BAND, S)
    tail = lambda t3: jnp.transpose(t3.reshape(3, B, S, LANES)[:, :, S - win:], (1, 2, 0, 3)).reshape(
        1, B, win, ATT_W // HEAD_DIM, HEAD_DIM)
    heads = lambda t, n: t.reshape(1, t.shape[0], t.shape[1], n, HEAD_DIM)

    xs = x_sample.reshape(DB * T, D_MODEL)
    qs, ks, vs, zbs, qms, gates = _proj(xs, jnp.tile(cr[:T], (DB, 1)), jnp.tile(sr[:T], (DB, 1)), cb, sb,
                                        row(norm_in[l]), w_in_b, DB * T, False, 1, tiles, F32)
    n_past = cache_win_k.shape[2]
    minor = lambda c: jnp.transpose(c, (0, 2, 3, 1)).reshape(c.shape[0], c.shape[2] * c.shape[3], c.shape[1])
    oas = _attn_sample(qs.reshape(DB, T, ATT_W), ks.reshape(DB, T, ATT_W), vs.reshape(DB, T, ATT_W),
                       minor(cache_win_k[l]), minor(cache_win_v[l]))
    zbs3 = zbs.reshape(DB, T, SHIFT_W)
    obs, st_s = _rwkv(zbs3, state_rwkv_shift[l].reshape(DB, 1, SHIFT_W), _state_to_pairs(state_rwkv[l]),
                      *rw, nb=4, groups=1)
    obs = obs[:, :T]
    oms = _memattn(qms.reshape(DB, T, MEM_W), minor(cache_mem_k[l]), minor(cache_mem_v[l]), T, 8, True, F32)
    y_s = _out(oas.reshape(DB * T, ATT_W), obs.reshape(DB * T, RWKV_W), oms.reshape(DB * T, MEM_W), gates, xs,
               w_out_b, row(norm_final), DB * T, False)

    return (y_p.reshape(B, S, D_MODEL), y_s.reshape(DB, T, D_MODEL),
            tail(k3), tail(v3),
            _pairs_to_state(st_p)[None], zb.reshape(B, S, SHIFT_W)[:, -1][None],
            heads(mk, MEM_W // HEAD_DIM), heads(mv, MEM_W // HEAD_DIM),
            heads(ks.reshape(DB, T, ATT_W), ATT_W // HEAD_DIM), heads(vs.reshape(DB, T, ATT_W), ATT_W // HEAD_DIM),
            _pairs_to_state(st_s)[None], zbs3[:, -1][None])
```

```python
import functools

import jax
import jax.numpy as jnp
from jax import lax
from jax.experimental import pallas as pl
from jax.experimental.pallas import tpu as pltpu

F32 = jnp.float32
BF16 = jnp.bfloat16

D_MODEL = 1024
HEAD_DIM = 64
ATT_W = 384
RWKV_W = 384
MEM_W = 256
MIX_W = 1024
LORA_W = 64
SHIFT_W = 3 * RWKV_W + 2 * LORA_W
N_MEM = 256
ROPE_DIM = 16
ROPE_THETA = 500000.0
NORM_EPS = 1e-6
LNX_EPS = 64e-5
DILATIONS = (1, 4, 16)
BAND = 128
ATT_TILE = BAND * max(DILATIONS)
LANES = 128
RWKV_CHUNK = 64
PROJ_TILE = 512
NEG = -1e30
Q_SCALE = HEAD_DIM ** -0.5 * 1.4426950408889634
VMEM_LIMIT = 56 * 1024 * 1024


def _cparams(sem):
    return pltpu.CompilerParams(dimension_semantics=sem, vmem_limit_bytes=VMEM_LIMIT)


def _dot(a, b):
    return jnp.dot(a.astype(BF16), b.astype(BF16), preferred_element_type=F32)


def _dot_nt(a, b):
    return lax.dot_general(a.astype(BF16), b.astype(BF16), (((1,), (1,)), ((), ())),
                           preferred_element_type=F32)


def _dot_tn(a, b):
    return lax.dot_general(a.astype(BF16), b.astype(BF16), (((0,), (0,)), ((), ())),
                           preferred_element_type=F32)


def _dot_f32(a, b):
    return jnp.dot(a, b, preferred_element_type=F32, precision=lax.Precision.HIGHEST)


def _half0(shape):
    return (lax.broadcasted_iota(jnp.int32, shape, len(shape) - 1) & 64) == 0


def _rope_table_kernel(pos_row_ref, pos_base_ref, invf_ref, cr_ref, sr_ref, cb_ref, sb_ref):
    ang_r = pos_row_ref[...] * invf_ref[...]
    cr_ref[...] = jnp.cos(ang_r)
    sr_ref[...] = jnp.sin(ang_r)
    ang_b = pos_base_ref[...] * invf_ref[...]
    cb_ref[...] = jnp.cos(ang_b)
    sb_ref[...] = jnp.sin(ang_b)


def _rope_tables(pos_row, pos_base, invf):
    tab = lambda n: jax.ShapeDtypeStruct((n, LANES), F32)
    return pl.pallas_call(
        _rope_table_kernel,
        out_shape=[tab(pos_row.shape[0]), tab(pos_row.shape[0]), tab(pos_base.shape[0]), tab(pos_base.shape[0])],
        name="rope_tables",
    )(pos_row, pos_base, invf)


def _proj_kernel(x_ref, cr_ref, sr_ref, cb_ref, sb_ref, g_ref, w_ref, q_ref, k_ref, v_ref, zb_ref, qm_ref, gate_ref,
                 *, split_pairs, tiles_per_seq, base_offset):
    tm = x_ref.shape[0]
    base = base_offset + lax.rem(pl.program_id(0), tiles_per_seq)
    cb = cb_ref[pl.ds(base, 1), :]
    sb = sb_ref[pl.ds(base, 1), :]
    second = (lax.broadcasted_iota(jnp.int32, (1, LANES), 1) & 8) != 0

    def put(ref, p, rows, val):
        if split_pairs:
            ref[p, rows, :] = val
        else:
            ref[rows, p * LANES:(p + 1) * LANES] = val

    nsplit = 2 if tm % 16 == 0 else 1
    hm = tm // nsplit
    for part in range(nsplit):
        rows = pl.ds(part * hm, hm)
        x = x_ref[rows, :]
        ms = jnp.mean(x * x, axis=-1, keepdims=True)
        h = ((x * lax.rsqrt(ms + NORM_EPS)) * g_ref[...]).astype(BF16)
        cr = cr_ref[rows, :]
        sr = sr_ref[rows, :]
        cos = cb * cr - sb * sr
        sin = sb * cr + cb * sr
        sin_up = jnp.where(second, sin, 0.0)
        sin_dn = jnp.where(second, 0.0, -sin)

        def rope(t):
            return t * cos + pltpu.roll(t, 8, 1) * sin_up + pltpu.roll(t, LANES - 8, 1) * sin_dn

        qkv = jnp.dot(h, w_ref[:, :3 * ATT_W], preferred_element_type=F32)
        for p in range(ATT_W // LANES):
            put(q_ref, p, rows, rope(qkv[:, p * LANES:(p + 1) * LANES]) * Q_SCALE)
            put(k_ref, p, rows, rope(qkv[:, ATT_W + p * LANES:ATT_W + (p + 1) * LANES]))
            put(v_ref, p, rows, qkv[:, 2 * ATT_W + p * LANES:2 * ATT_W + (p + 1) * LANES])
        c0 = 3 * ATT_W
        zb_ref[rows, :] = jnp.dot(h, w_ref[:, c0:c0 + SHIFT_W], preferred_element_type=F32)
        c0 += SHIFT_W
        qm_ref[rows, :] = (jnp.dot(h, w_ref[:, c0:c0 + MEM_W], preferred_element_type=F32) * 0.125).astype(
            qm_ref.dtype)
        c0 += MEM_W
        gate_ref[rows, :] = jnp.dot(h, w_ref[:, c0:c0 + MIX_W], preferred_element_type=F32).astype(gate_ref.dtype)


def _proj(x2d, cr, sr, cb, sb, g, w_bf16, tm, split_pairs, tiles_per_seq, base_offset, act):
    rows = x2d.shape[0]
    in_w = w_bf16.shape[1]
    if split_pairs:
        qkv_shape = jax.ShapeDtypeStruct((3, rows, LANES), F32)
        qkv_spec = pl.BlockSpec((3, tm, LANES), lambda i: (0, i, 0))
    else:
        qkv_shape = jax.ShapeDtypeStruct((rows, ATT_W), F32)
        qkv_spec = pl.BlockSpec((tm, ATT_W), lambda i: (i, 0))
    row = lambda w: pl.BlockSpec((tm, w), lambda i: (i, 0))
    full = lambda a: pl.BlockSpec(a.shape, lambda i: (0, 0))
    return pl.pallas_call(
        functools.partial(_proj_kernel, split_pairs=split_pairs, tiles_per_seq=tiles_per_seq,
                          base_offset=base_offset),
        grid=(rows // tm,),
        in_specs=[row(D_MODEL), full(cr), full(sr), full(cb), full(sb), full(g), full(w_bf16)],
        out_specs=[qkv_spec, qkv_spec, qkv_spec, row(SHIFT_W), row(MEM_W), row(MIX_W)],
        out_shape=[qkv_shape, qkv_shape, qkv_shape,
                   jax.ShapeDtypeStruct((rows, SHIFT_W), F32),
                   jax.ShapeDtypeStruct((rows, MEM_W), act),
                   jax.ShapeDtypeStruct((rows, MIX_W), act)],
        compiler_params=_cparams(("arbitrary",)),
        name="proj",
    )(x2d, cr, sr, cb, sb, g, w_bf16)


def _attn_blocks(npairs, d, q_ref, kc_ref, vc_ref, blocks):
    h0 = _half0((BAND, LANES))
    h0k = _half0((2 * BAND, LANES))
    units = [(b, p) for b in range(len(blocks)) for p in range(npairs)]
    heads = [(i, hh) for i in range(len(units)) for hh in range(2)]
    row = [pl.ds(blk[2], BAND, stride=d) for blk in blocks]
    prow = [pl.ds(blk[3], BAND, stride=d) for blk in blocks]
    q = [q_ref[p, row[b], :] for b, p in units]
    kcat = [jnp.concatenate([blocks[b][0][p, prow[b], :], kc_ref[p, row[b], :]], axis=0).astype(BF16)
            for b, p in units]
    vcat = [jnp.concatenate([blocks[b][1][p, prow[b], :], vc_ref[p, row[b], :]], axis=0) for b, p in units]
    s = [_dot_nt(jnp.where(h0 if hh == 0 else jnp.logical_not(h0), q[i], 0.0), kcat[i]) + blocks[units[i][0]][4]
         for i, hh in heads]
    m = [jnp.max(s[j], axis=-1, keepdims=True) for j in range(len(heads))]
    pexp = [jnp.exp2(s[j] - m[j]).astype(BF16) for j in range(len(heads))]
    res = [jnp.dot(pexp[j], jnp.where(h0k if hh == 0 else jnp.logical_not(h0k), vcat[i], 1.0).astype(BF16),
                   preferred_element_type=F32) for j, (i, hh) in enumerate(heads)]
    lsum = [pltpu.roll(jnp.where(h0, res[2 * i + 1], res[2 * i]), HEAD_DIM, 1) for i in range(len(units))]
    outs = [[] for _ in blocks]
    for i, (b, p) in enumerate(units):
        outs[b].append((jnp.where(h0, res[2 * i], res[2 * i + 1]) / lsum[i],
                        jnp.where(h0, m[2 * i], m[2 * i + 1]) + jnp.log2(lsum[i])))
    return outs


def _attn_kernel(q_ref, kc_ref, kp_ref, vc_ref, vp_ref, o_ref, o_sc, lse_sc):
    has_prev = pl.program_id(1) > 0
    qi = lax.broadcasted_iota(jnp.int32, (BAND, 2 * BAND), 0)
    kj = lax.broadcasted_iota(jnp.int32, (BAND, 2 * BAND), 1)
    in_cur = (kj >= BAND) & (kj - BAND <= qi)
    in_prev = (kj < BAND) & (kj >= qi)
    bias_full = jnp.where(in_cur | in_prev, 0.0, NEG).astype(F32)
    bias_first = jnp.where(in_cur | (in_prev & has_prev), 0.0, NEG).astype(F32)
    npairs = q_ref.shape[0]
    nblk = ATT_TILE // BAND
    per_iter = 2

    def desc(d, u, first):
        span = BAND * d
        if first:
            return (kp_ref, vp_ref, u, ATT_TILE - span + u, bias_first)
        qs = (u >> (d.bit_length() - 1)) * span + (u & (d - 1))
        return (kc_ref, vc_ref, qs, qs - span, bias_full)

    def merge(outs, start):
        row = pl.ds(start, BAND)
        for p, (o1, lse1) in enumerate(outs):
            os_ = [o1] + [o_sc[j, p, row, :] for j in range(len(DILATIONS) - 1)]
            ls_ = [lse1] + [lse_sc[j, p, row, :] for j in range(len(DILATIONS) - 1)]
            top = functools.reduce(jnp.maximum, ls_)
            ws = [jnp.exp2(l - top) for l in ls_]
            num = functools.reduce(lambda a, b: a + b, [w * o for w, o in zip(ws, os_)])
            o_ref[p, row, :] = (num / functools.reduce(lambda a, b: a + b, ws)).astype(o_ref.dtype)

    def run(j, d, u0, first):
        blocks = [desc(d, u0 + i, first) for i in range(per_iter)]
        for blk, outs in zip(blocks, _attn_blocks(npairs, d, q_ref, kc_ref, vc_ref, blocks)):
            if j < 0:
                merge(outs, blk[2] if first else pl.multiple_of(blk[2], BAND))
            else:
                for p, (o, lse) in enumerate(outs):
                    o_sc[j, p, pl.ds(blk[2], BAND, stride=d), :] = o
                    lse_sc[j, p, pl.ds(blk[2], BAND, stride=d), :] = lse

    def sweep(j, d):
        n_first = d
        if n_first % per_iter == 0:
            lax.fori_loop(0, n_first // per_iter, lambda i, c: (run(j, d, i * per_iter, True), c)[1], 0)
            lo = n_first
        else:
            assert n_first == 1 and per_iter == 2
            blocks = [desc(d, 0, True), desc(d, 1, False)]
            for blk, outs in zip(blocks, _attn_blocks(npairs, d, q_ref, kc_ref, vc_ref, blocks)):
                merge(outs, blk[2])
            lo = per_iter
        if lo < nblk:
            lax.fori_loop(lo // per_iter, nblk // per_iter, lambda i, c: (run(j, d, i * per_iter, False), c)[1], 0)

    for j, d in enumerate(DILATIONS[1:]):
        sweep(j, d)
    sweep(-1, DILATIONS[0])


def _attn_prompt(q3, k3, v3, batch, seq):
    nt = seq // ATT_TILE
    cur = pl.BlockSpec((3, ATT_TILE, LANES), lambda b, i: (0, b * nt + i, 0))
    prev = pl.BlockSpec((3, ATT_TILE, LANES), lambda b, i: (0, b * nt + jnp.maximum(i - 1, 0), 0))
    return pl.pallas_call(
        _attn_kernel,
        grid=(batch, nt),
        in_specs=[cur, cur, prev, cur, prev],
        out_specs=cur,
        out_shape=jax.ShapeDtypeStruct(q3.shape, BF16),
        scratch_shapes=[pltpu.VMEM((len(DILATIONS) - 1, 3, ATT_TILE, LANES), F32),
                        pltpu.VMEM((len(DILATIONS) - 1, 3, ATT_TILE, LANES), F32)],
        compiler_params=_cparams(("arbitrary", "arbitrary")),
        name="attn_prompt",
    )(q3, k3, k3, v3, v3)


def _attn_sample_kernel(q_ref, kn_ref, vn_ref, kc_ref, vc_ref, o_ref, *, n_past, t_new, nb):
    nh = ATT_W // HEAD_DIM
    head_of_lane = lax.broadcasted_iota(jnp.int32, (t_new, ATT_W), 1) >> 6
    seqs = range(nb)
    qst = [jnp.concatenate([jnp.where(head_of_lane == h, q_ref[b], 0.0) for h in range(nh)], axis=0) for b in seqs]

    def count(delta):
        c = jnp.zeros(delta.shape, F32)
        for d in DILATIONS:
            ok = (delta >= 0) & (delta <= BAND * d) & ((delta & (d - 1)) == 0)
            c = c + jnp.where(ok, 1.0, 0.0)
        return c

    rows = nh * t_new
    t_past = lax.broadcasted_iota(jnp.int32, (rows, n_past), 0) & (t_new - 1)
    cnt_past = count(n_past + t_past - lax.broadcasted_iota(jnp.int32, (rows, n_past), 1))
    t_n = lax.broadcasted_iota(jnp.int32, (rows, t_new), 0) & (t_new - 1)
    cnt_new = count(t_n - lax.broadcasted_iota(jnp.int32, (rows, t_new), 1))

    s_past = [jnp.where(cnt_past > 0, _dot(qst[b], kc_ref[b]), NEG) for b in seqs]
    s_new = [jnp.where(cnt_new > 0, _dot_nt(qst[b], kn_ref[b]), NEG) for b in seqs]
    m = [jnp.maximum(jnp.max(s_past[b], axis=-1, keepdims=True), jnp.max(s_new[b], axis=-1, keepdims=True))
         for b in seqs]
    p_past = [cnt_past * jnp.exp2(s_past[b] - m[b]) for b in seqs]
    p_new = [cnt_new * jnp.exp2(s_new[b] - m[b]) for b in seqs]
    l = [jnp.sum(p_past[b], axis=-1, keepdims=True) + jnp.sum(p_new[b], axis=-1, keepdims=True) for b in seqs]
    o = [(_dot_nt(p_past[b], vc_ref[b]) + _dot(p_new[b], vn_ref[b])) / l[b] for b in seqs]
    for b in seqs:
        out = jnp.zeros((t_new, ATT_W), F32)
        for h in range(nh):
            out = jnp.where(head_of_lane == h, o[b][h * t_new:(h + 1) * t_new, :], out)
        o_ref[b] = out.astype(o_ref.dtype)


def _attn_sample(q, k_new, v_new, k_cache, v_cache, nb):
    db, t_new, _ = q.shape
    n_past = k_cache.shape[2]
    new = pl.BlockSpec((nb, t_new, ATT_W), lambda b: (b, 0, 0))
    cache = pl.BlockSpec((nb, ATT_W, n_past), lambda b: (b, 0, 0))
    return pl.pallas_call(
        functools.partial(_attn_sample_kernel, n_past=n_past, t_new=t_new, nb=nb),
        grid=(db // nb,),
        in_specs=[new, new, new, cache, cache],
        out_specs=new,
        out_shape=jax.ShapeDtypeStruct(q.shape, F32),
        compiler_params=_cparams(("arbitrary",)),
        name="attn_sample",
    )(q, k_new, v_new, k_cache, v_cache)


def _seg_sum(x, h0):
    s0 = jnp.sum(jnp.where(h0, x, 0.0), axis=-1, keepdims=True)
    s1 = jnp.sum(jnp.where(h0, 0.0, x), axis=-1, keepdims=True)
    return jnp.where(h0, s0, s1)


def _split_bf16(x):
    hi = x.astype(BF16)
    return hi, (x - hi.astype(F32)).astype(BF16)


def _rwkv_kernel(z_ref, sh0_ref, st0_ref, mu_ref, w0_ref, a0_ref, whi_ref, wlo_ref, kk_ref, ka_ref, rk_ref,
                 lng_ref, lnb_ref, o_ref, st_ref, st_sc, prev_sc, *, nb, groups, n_valid):
    C = RWKV_CHUNK
    R = groups * C
    npairs = RWKV_W // LANES

    @pl.when(pl.program_id(1) == 0)
    def _():
        st_sc[...] = st0_ref[...]
        prev_sc[...] = sh0_ref[...]

    si = lax.broadcasted_iota(jnp.int32, (2 * C, 2 * C), 0)
    sj = lax.broadcasted_iota(jnp.int32, (2 * C, 2 * C), 1)
    same_head = (si >= C) == (sj >= C)
    ti2 = si & (C - 1)
    tj2 = sj & (C - 1)
    incl = same_head & (tj2 <= ti2)
    strict = same_head & (tj2 < ti2)
    eye = jnp.where(si == sj, 1.0, 0.0).astype(F32)
    levels = []
    s = 2
    while s < min(C, n_valid):
        sh = s.bit_length() - 1
        levels.append(((ti2 >> (sh + 1)) == (tj2 >> (sh + 1))) & (((ti2 >> sh) & 1) == 1) & (((tj2 >> sh) & 1) == 0))
        s *= 2
    first_level = (ti2 >> 1) == (tj2 >> 1)

    ri = lax.broadcasted_iota(jnp.int32, (C, C), 0)
    rj = lax.broadcasted_iota(jnp.int32, (C, C), 1)
    tri = jnp.where(rj <= ri, 1.0, 0.0).astype(BF16)
    rowid = lax.broadcasted_iota(jnp.int32, (C, 1), 0)
    h0 = _half0((C, LANES))
    zero_blk = jnp.zeros((2 * C, LANES), F32)
    dot = functools.partial(jnp.dot, preferred_element_type=F32)
    pre = {}

    def stack(x):
        return jnp.concatenate([jnp.where(h0, x, 0.0), jnp.where(h0, 0.0, x)], axis=0)

    def prep(b, g):
        if n_valid < C:
            z = jnp.concatenate([z_ref[b], jnp.zeros((C - n_valid, SHIFT_W), F32)], axis=0)
        else:
            z = z_ref[b, g * C:(g + 1) * C, :]
        before = prev_sc[b] if g == 0 else z_ref[b, pl.ds(g * C - 1, 1), :]
        z_prev = jnp.where(rowid == 0, before, pltpu.roll(z, 1, 0))
        zs = z + (z_prev - z) * mu_ref[...]
        valid = rowid < (n_valid - g * C)
        if n_valid < (g + 1) * C:
            zs = jnp.where(valid, zs, 0.0)
        lat = zs[:, 3 * RWKV_W:]
        lat_hi, lat_lo = _split_bf16(jnp.where(_half0(lat.shape), jnp.tanh(lat), lat))
        lora = dot(lat_hi, whi_ref[...]) + dot(lat_lo, whi_ref[...]) + dot(lat_hi, wlo_ref[...])
        w = -jax.nn.softplus(-(w0_ref[...] + lora[:, :RWKV_W])) - 0.5
        lw = -jnp.exp(w)
        if n_valid < (g + 1) * C:
            lw = jnp.where(valid, lw, 0.0)
        a = jax.nn.sigmoid(a0_ref[...] + lora[:, RWKV_W:])
        lw_hi, lw_lo = _split_bf16(lw)
        cum = dot(tri, lw_hi) + dot(tri, lw_lo)
        for p in range(npairs):
            sl = slice(p * LANES, (p + 1) * LANES)
            r = zs[:, sl]
            k = zs[:, RWKV_W + p * LANES:RWKV_W + (p + 1) * LANES]
            v = zs[:, 2 * RWKV_W + p * LANES:2 * RWKV_W + (p + 1) * LANES]
            ap = a[:, sl]
            kk = k * kk_ref[:, sl]
            kk = kk / jnp.maximum(jnp.sqrt(_seg_sum(kk * kk, h0)), 1e-12)
            k = k * (1.0 + (ap - 1.0) * ka_ref[:, sl])
            bb = kk * ap
            L = cum[:, sl]
            l_end = L[C - 1:C, :]
            e_out = jnp.exp(-L)
            e_end = jnp.exp(l_end - L)
            rt = stack(r * jnp.exp(L))
            kkh = stack(kk * jnp.exp(L - lw[:, sl]))
            left = jnp.concatenate([rt, kkh], axis=0).astype(BF16)
            right = jnp.concatenate([stack(k * e_out), stack(-(bb * e_out))], axis=0).astype(BF16)
            ends = jnp.concatenate([stack(k * e_end), stack(-(bb * e_end))], axis=0).astype(BF16)
            pre[(b, g, p)] = dict(left=left, right=right, ends=ends, rt=rt, kkh=kkh, v2=stack(v),
                                  decay=jnp.exp(l_end), bonus=_seg_sum(r * k * rk_ref[:, sl], h0) * v)

    def transition_stages(chains):
        def scores():
            for c in chains:
                d = pre[c]
                aa = lax.dot_general(d["left"], d["right"], (((1,), (1,)), ((), ())), preferred_element_type=F32)
                d["a_r"] = jnp.concatenate([jnp.where(incl, aa[:2 * C, :2 * C], 0.0),
                                            jnp.where(incl, aa[:2 * C, 2 * C:], 0.0)], axis=1).astype(BF16)
                d["a_kk"] = jnp.where(strict, aa[2 * C:, :2 * C], 0.0)
                d["n_kb"] = jnp.where(strict, aa[2 * C:, 2 * C:], 0.0)
                d["t"] = eye + jnp.where(first_level, d["n_kb"], 0.0)
        def odd_rows(x, s):
            return jnp.concatenate([x[i:i + s] for i in range(s, 2 * C, 2 * s)], axis=0) if s >= 8 else x

        def level_a(off, s):
            for c in chains:
                pre[c]["tn"] = _dot(odd_rows(pre[c]["t"], s), jnp.where(off, pre[c]["n_kb"], 0.0))
        def level_b(s):
            for c in chains:
                t = pre[c]["t"]
                new = odd_rows(t, s) + _dot(pre[c]["tn"], t)
                if s >= 8:
                    parts = []
                    for j, i in enumerate(range(0, 2 * C, 2 * s)):
                        parts += [t[i:i + s], new[j * s:(j + 1) * s]]
                    new = jnp.concatenate(parts, axis=0)
                pre[c]["t"] = new
        def akv():
            for c in chains:
                pre[c]["akv"] = _dot(pre[c]["a_kk"], pre[c]["v2"])
        def solve():
            for c in chains:
                d = pre[c]
                tr = _dot(d["t"], jnp.concatenate([d["kkh"], d["akv"]], axis=1))
                kq, u0 = tr[:, :LANES], tr[:, LANES:]
                d["big"] = jnp.concatenate([jnp.concatenate([d["v2"], zero_blk], axis=1),
                                            jnp.concatenate([u0, kq], axis=1)], axis=0).astype(BF16)
        def readout():
            for c in chains:
                d = pre[c]
                yr = dot(d["a_r"], d["big"])
                d["y0"] = yr[:, :LANES]
                d["rq"] = (d["rt"] + yr[:, LANES:]).astype(BF16)
        def update():
            for c in chains:
                d = pre[c]
                dg = lax.dot_general(d["big"], d["ends"], (((0,), (0,)), ((), ())), preferred_element_type=F32)
                d["dd"] = dg[:LANES]
                d["gm"] = dg[LANES:].astype(BF16)
        stages = [scores]
        for i, off in enumerate(levels):
            stages += [functools.partial(level_a, off, 2 << i), functools.partial(level_b, 2 << i)]
        return stages + [akv, solve, readout, update]

    state = {(b, p): st_sc[b, p] for b in range(nb) for p in range(npairs)}

    def carried(chains):
        for (b, g, p) in chains:
            d = pre[(b, g, p)]
            st = state[(b, p)]
            st_b = st.astype(BF16)
            d["y2"] = lax.dot_general(d["rq"], st_b, (((1,), (1,)), ((), ())), preferred_element_type=F32) + d["y0"]
            state[(b, p)] = st * d["decay"] + dot(st_b, d["gm"]) + d["dd"]
        for (b, g, p) in chains:
            d = pre.pop((b, g, p))
            y2 = d["y2"]
            y = y2[:C] + y2[C:]
            sl = slice(p * LANES, (p + 1) * LANES)
            mean = _seg_sum(y, h0) * (1.0 / HEAD_DIM)
            yc = y - mean
            var = _seg_sum(yc * yc, h0) * (1.0 / HEAD_DIM)
            yn = yc * lax.rsqrt(var + LNX_EPS) * lng_ref[:, sl] + lnb_ref[:, sl]
            rows_out = min(C, n_valid - g * C)
            o_ref[b, g * C:g * C + rows_out, sl] = (yn + d["bonus"])[:rows_out].astype(o_ref.dtype)

    items = [(b, g) for g in range(groups) for b in range(nb)]
    for item in items:
        prep(*item)
    chains = [(b, g, p) for (b, g) in items for p in range(npairs)]
    for stage in transition_stages(chains):
        stage()
    carried(chains)
    for b in range(nb):
        prev_sc[b] = z_ref[b, pl.ds(min(R, n_valid) - 1, 1), :]
    for (b, p), st in state.items():
        st_sc[b, p] = st

    @pl.when(pl.program_id(1) == pl.num_programs(1) - 1)
    def _():
        st_ref[...] = st_sc[...]


def _rwkv(z, shift0, st0, mu, w0, a0, w_hi, w_lo, k_k, k_a, r_k, ln_g, ln_b, nb, groups, act):
    b, s, _ = z.shape
    rows = groups * RWKV_CHUNK
    if s % rows:
        assert groups == 1 and s < rows and s % 8 == 0
        rows = s
    n_valid = rows
    vec = lambda w: pl.BlockSpec((1, w), lambda i, j: (0, 0))
    st_spec = pl.BlockSpec((nb, 3, LANES, LANES), lambda i, j: (i, 0, 0, 0))
    wspec = pl.BlockSpec((LANES, 2 * RWKV_W), lambda i, j: (0, 0))
    return pl.pallas_call(
        functools.partial(_rwkv_kernel, nb=nb, groups=groups, n_valid=n_valid),
        grid=(b // nb, s // rows),
        in_specs=[pl.BlockSpec((nb, rows, SHIFT_W), lambda i, j: (i, j, 0)),
                  pl.BlockSpec((nb, 1, SHIFT_W), lambda i, j: (i, 0, 0)),
                  st_spec,
                  vec(SHIFT_W), vec(RWKV_W), vec(RWKV_W), wspec, wspec,
                  vec(RWKV_W), vec(RWKV_W), vec(RWKV_W), vec(RWKV_W), vec(RWKV_W)],
        out_specs=[pl.BlockSpec((nb, rows, RWKV_W), lambda i, j: (i, j, 0)), st_spec],
        out_shape=[jax.ShapeDtypeStruct((b, s, RWKV_W), act),
                   jax.ShapeDtypeStruct((b, 3, LANES, LANES), F32)],
        scratch_shapes=[pltpu.VMEM((nb, 3, LANES, LANES), F32), pltpu.VMEM((nb, 1, SHIFT_W), F32)],
        compiler_params=_cparams(("arbitrary", "arbitrary")),
        name="rwkv",
    )(z, shift0, st0, mu, w0, a0, w_hi, w_lo, k_k, k_a, r_k, ln_g, ln_b)


def _state_to_pairs(st):
    b = st.shape[0]
    st = st.reshape(b, 3, 2, HEAD_DIM, HEAD_DIM)
    zero = jnp.zeros_like(st[:, :, 0])
    top = jnp.concatenate([st[:, :, 0], zero], axis=-1)
    bot = jnp.concatenate([zero, st[:, :, 1]], axis=-1)
    return jnp.concatenate([top, bot], axis=-2)


def _pairs_to_state(sp):
    b = sp.shape[0]
    return jnp.stack([sp[:, :, :HEAD_DIM, :HEAD_DIM], sp[:, :, HEAD_DIM:, HEAD_DIM:]], axis=2).reshape(
        b, 6, HEAD_DIM, HEAD_DIM)


def _memkv_kernel(x_ref, g_ref, w_ref, k_ref, v_ref):
    x = x_ref[...]
    ms = jnp.mean(x * x, axis=-1, keepdims=True)
    h = ((x * lax.rsqrt(ms + NORM_EPS)) * g_ref[...]).astype(BF16)
    k_ref[...] = jnp.dot(h, w_ref[:, :MEM_W], preferred_element_type=F32)
    v_ref[...] = jnp.dot(h, w_ref[:, MEM_W:], preferred_element_type=F32)


def _memkv(mem2d, g, w_bf16):
    rows = mem2d.shape[0]
    out = jax.ShapeDtypeStruct((rows, MEM_W), F32)
    return pl.pallas_call(
        _memkv_kernel,
        out_shape=[out, out],
        compiler_params=pltpu.CompilerParams(vmem_limit_bytes=VMEM_LIMIT),
        name="memkv",
    )(mem2d, g, w_bf16)


def _memattn_kernel(q_ref, mk_ref, mv_ref, o_ref, *, nb, mem_minor):
    head_of_lane = lax.broadcasted_iota(jnp.int32, q_ref.shape[1:], 1) >> 6
    qs = [q_ref[b] for b in range(nb)]
    mks = [mk_ref[b].astype(BF16) for b in range(nb)]
    mvs = [mv_ref[b].astype(BF16) for b in range(nb)]
    outs = [jnp.zeros(q_ref.shape[1:], F32) for _ in range(nb)]
    for h in range(MEM_W // HEAD_DIM):
        mine = head_of_lane == h
        qk = _dot if mem_minor else _dot_nt
        pv = _dot_nt if mem_minor else _dot
        s = [qk(jnp.where(mine, qs[b], 0.0), mks[b]) for b in range(nb)]
        pexp = [jnp.exp(s[b] - jnp.max(s[b], axis=-1, keepdims=True)) for b in range(nb)]
        l = [jnp.sum(pexp[b], axis=-1, keepdims=True) for b in range(nb)]
        outs = [jnp.where(mine, pv(pexp[b], mvs[b]) / l[b], outs[b]) for b in range(nb)]
    for b in range(nb):
        o_ref[b] = outs[b].astype(o_ref.dtype)


def _memattn(q, mk, mv, tm, nb, mem_minor, act):
    b, s, _ = q.shape
    qspec = pl.BlockSpec((nb, tm, MEM_W), lambda i, j: (i, j, 0))
    mspec = pl.BlockSpec((nb,) + mk.shape[1:], lambda i, j: (i, 0, 0))
    return pl.pallas_call(
        functools.partial(_memattn_kernel, nb=nb, mem_minor=mem_minor),
        grid=(b // nb, s // tm),
        in_specs=[qspec, mspec, mspec],
        out_specs=qspec,
        out_shape=jax.ShapeDtypeStruct(q.shape, act),
        compiler_params=_cparams(("arbitrary", "arbitrary")),
        name="memattn",
    )(q, mk, mv)


def _out_kernel(oa_ref, ob_ref, om_ref, gate_ref, x_ref, w_ref, g_ref, y_ref, *, split_pairs):
    acc = x_ref[...]

    def add(acc, o, c0):
        width = o.shape[1]
        gate = gate_ref[:, c0:c0 + width].astype(F32)
        return acc + _dot(o.astype(F32) * (gate * jax.nn.sigmoid(gate)), w_ref[c0:c0 + width, :])

    if split_pairs:
        for p in range(ATT_W // LANES):
            acc = add(acc, oa_ref[p], p * LANES)
    else:
        acc = add(acc, oa_ref[...], 0)
    acc = add(acc, ob_ref[...], ATT_W)
    acc = add(acc, om_ref[...], ATT_W + RWKV_W)
    ms = jnp.mean(acc * acc, axis=-1, keepdims=True)
    y_ref[...] = (acc * lax.rsqrt(ms + NORM_EPS)) * g_ref[...]


def _out(oa, ob, om, gate, x2d, w_bf16, g, tm, split_pairs):
    rows = x2d.shape[0]
    row = lambda w: pl.BlockSpec((tm, w), lambda i: (i, 0))
    oa_spec = pl.BlockSpec((3, tm, LANES), lambda i: (0, i, 0)) if split_pairs else row(ATT_W)
    return pl.pallas_call(
        functools.partial(_out_kernel, split_pairs=split_pairs),
        grid=(rows // tm,),
        in_specs=[oa_spec, row(RWKV_W), row(MEM_W), row(MIX_W), row(D_MODEL),
                  pl.BlockSpec((MIX_W, D_MODEL), lambda i: (0, 0)),
                  pl.BlockSpec((1, D_MODEL), lambda i: (0, 0))],
        out_specs=row(D_MODEL),
        out_shape=jax.ShapeDtypeStruct((rows, D_MODEL), F32),
        compiler_params=_cparams(("arbitrary",)),
        name="outproj",
    )(oa, ob, om, gate, x2d, w_bf16, g)


def kernel(x_prompt, x_sample, cache_win_k, cache_win_v, state_rwkv, state_rwkv_shift, cache_mem_k, cache_mem_v, mem_prompt, norm_in, w_in, rwkv_mu, rwkv_w0, rwkv_w2, rwkv_a0, rwkv_a2, rwkv_k_k, rwkv_k_a, rwkv_r_k, rwkv_lnx_g, rwkv_lnx_b, norm_mem, w_mem_kv, w_out, norm_final):
    B, S, _ = x_prompt.shape
    DB, T, _ = x_sample.shape
    depth = w_in.shape[0]
    assert depth == 1 and S % ATT_TILE == 0 and cache_win_k.shape[2] == max(DILATIONS) * BAND
    l = 0
    past_len = S

    half = ROPE_DIM // 2
    inv_freq = ROPE_THETA ** (-jnp.arange(half, dtype=F32) / half)
    lane = jnp.arange(LANES)
    invf = jnp.where((lane % HEAD_DIM) < ROPE_DIM, inv_freq[lane % half], 0.0).reshape(1, LANES)

    row = lambda t: t.reshape(1, -1)
    w_in_b = w_in[l].astype(BF16)
    w_out_b = w_out[l].astype(BF16)
    w_kv_b = w_mem_kv[l].astype(BF16)
    zero = jnp.zeros((LORA_W, RWKV_W), F32)
    w2a2 = jnp.concatenate([jnp.concatenate([rwkv_w2[l], zero], axis=1),
                            jnp.concatenate([zero, rwkv_a2[l]], axis=1)], axis=0)
    w2a2_hi = w2a2.astype(BF16)
    w2a2_lo = (w2a2 - w2a2_hi.astype(F32)).astype(BF16)
    rw = (row(rwkv_mu[l]), row(rwkv_w0[l]), row(rwkv_a0[l]), w2a2_hi, w2a2_lo, row(rwkv_k_k[l]),
          row(rwkv_k_a[l]), row(rwkv_r_k[l]), row(rwkv_lnx_g[l]), row(rwkv_lnx_b[l]))

    tiles = S // PROJ_TILE
    n_base = -(-(tiles + 1) // 8) * 8
    pos_base = jnp.where(jnp.arange(n_base) < tiles, jnp.arange(n_base) * PROJ_TILE, past_len)
    cr, sr, cb, sb = _rope_tables(jnp.arange(PROJ_TILE, dtype=F32).reshape(-1, 1),
                                  pos_base.astype(F32).reshape(-1, 1), invf)

    xp = x_prompt.reshape(B * S, D_MODEL)
    q3, k3, v3, zb, qm, gate = _proj(xp, cr, sr, cb, sb, row(norm_in[l]), w_in_b, PROJ_TILE, True, tiles, 0, BF16)
    oa3 = _attn_prompt(q3, k3, v3, B, S)
    ob, st_p = _rwkv(zb.reshape(B, S, SHIFT_W), jnp.zeros((B, 1, SHIFT_W), F32),
                     jnp.zeros((B, 3, LANES, LANES), F32), *rw, nb=B, groups=2, act=BF16)
    mk, mv = _memkv(mem_prompt.reshape(B * N_MEM, D_MODEL), row(norm_mem[l]), w_kv_b)
    mk = mk.reshape(B, N_MEM, MEM_W)
    mv = mv.reshape(B, N_MEM, MEM_W)
    om = _memattn(qm.reshape(B, S, MEM_W), mk, mv, 1024, 1, False, BF16)
    y_p = _out(oa3, ob.reshape(B * S, RWKV_W), om.reshape(B * S, MEM_W), gate, xp, w_out_b,
               row(norm_final), 1024, True)

    win = min(max(DILATIONS) * BAND, S)
    tail = lambda t3: jnp.transpose(t3.reshape(3, B, S, LANES)[:, :, S - win:], (1, 2, 0, 3)).reshape(
        1, B, win, ATT_W // HEAD_DIM, HEAD_DIM)
    heads = lambda t, n: t.reshape(1, t.shape[0], t.shape[1], n, HEAD_DIM)

    xs = x_sample.reshape(DB * T, D_MODEL)
    qs, ks, vs, zbs, qms, gates = _proj(xs, jnp.tile(cr[:T], (DB, 1)), jnp.tile(sr[:T], (DB, 1)), cb, sb,
                                        row(norm_in[l]), w_in_b, DB * T, False, 1, tiles, F32)
    n_past = cache_win_k.shape[2]
    minor = lambda c: jnp.transpose(c, (0, 2, 3, 1)).reshape(c.shape[0], c.shape[2] * c.shape[3], c.shape[1])
    oas = _attn_sample(qs.reshape(DB, T, ATT_W), ks.reshape(DB, T, ATT_W), vs.reshape(DB, T, ATT_W),
                       minor(cache_win_k[l]), minor(cache_win_v[l]), 2)
    zbs3 = zbs.reshape(DB, T, SHIFT_W)
    obs, st_s = _rwkv(zbs3, state_rwkv_shift[l].reshape(DB, 1, SHIFT_W), _state_to_pairs(state_rwkv[l]),
                      *rw, nb=4, groups=1, act=F32)
    oms = _memattn(qms.reshape(DB, T, MEM_W), minor(cache_mem_k[l]), minor(cache_mem_v[l]), T, 8, True, F32)
    y_s = _out(oas.reshape(DB * T, ATT_W), obs.reshape(DB * T, RWKV_W), oms.reshape(DB * T, MEM_W), gates, xs,
               w_out_b, row(norm_final), DB * T, False)

    return (y_p.reshape(B, S, D_MODEL), y_s.reshape(DB, T, D_MODEL),
            tail(k3), tail(v3),
            _pairs_to_state(st_p)[None], zb.reshape(B, S, SHIFT_W)[:, -1][None],
            heads(mk, MEM_W // HEAD_DIM), heads(mv, MEM_W // HEAD_DIM),
            heads(ks.reshape(DB, T, ATT_W), ATT_W // HEAD_DIM), heads(vs.reshape(DB, T, ATT_W), ATT_W // HEAD_DIM),
            _pairs_to_state(st_s)[None], zbs3[:, -1][None])
```

```python
import functools

import jax
import jax.numpy as jnp
from jax import lax
from jax.experimental import pallas as pl
from jax.experimental.pallas import tpu as pltpu

F32 = jnp.float32
BF16 = jnp.bfloat16

D_MODEL = 1024
HEAD_DIM = 64
ATT_W = 384
RWKV_W = 384
MEM_W = 256
MIX_W = 1024
LORA_W = 64
SHIFT_W = 3 * RWKV_W + 2 * LORA_W
N_MEM = 256
ROPE_DIM = 16
ROPE_THETA = 500000.0
NORM_EPS = 1e-6
LNX_EPS = 64e-5
DILATIONS = (1, 4, 16)
BAND = 128
ATT_TILE = BAND * max(DILATIONS)
LANES = 128
RWKV_CHUNK = 64
PROJ_TILE = 512
NEG = -1e30
Q_SCALE = HEAD_DIM ** -0.5 * 1.4426950408889634
VMEM_LIMIT = 56 * 1024 * 1024


def _cparams(sem):
    return pltpu.CompilerParams(dimension_semantics=sem, vmem_limit_bytes=VMEM_LIMIT)


def _dot(a, b):
    return jnp.dot(a.astype(BF16), b.astype(BF16), preferred_element_type=F32)


def _dot_nt(a, b):
    return lax.dot_general(a.astype(BF16), b.astype(BF16), (((1,), (1,)), ((), ())),
                           preferred_element_type=F32)


def _dot_tn(a, b):
    return lax.dot_general(a.astype(BF16), b.astype(BF16), (((0,), (0,)), ((), ())),
                           preferred_element_type=F32)


def _dot_f32(a, b):
    return jnp.dot(a, b, preferred_element_type=F32, precision=lax.Precision.HIGHEST)


def _half0(shape):
    return (lax.broadcasted_iota(jnp.int32, shape, len(shape) - 1) & 64) == 0


def _rope_table_kernel(pos_row_ref, pos_base_ref, invf_ref, cr_ref, sr_ref, cb_ref, sb_ref):
    ang_r = pos_row_ref[...] * invf_ref[...]
    cr_ref[...] = jnp.cos(ang_r)
    sr_ref[...] = jnp.sin(ang_r)
    ang_b = pos_base_ref[...] * invf_ref[...]
    cb_ref[...] = jnp.cos(ang_b)
    sb_ref[...] = jnp.sin(ang_b)


def _rope_tables(pos_row, pos_base, invf):
    tab = lambda n: jax.ShapeDtypeStruct((n, LANES), F32)
    return pl.pallas_call(
        _rope_table_kernel,
        out_shape=[tab(pos_row.shape[0]), tab(pos_row.shape[0]), tab(pos_base.shape[0]), tab(pos_base.shape[0])],
        name="rope_tables",
    )(pos_row, pos_base, invf)


def _proj_kernel(x_ref, cr_ref, sr_ref, cb_ref, sb_ref, g_ref, w_ref, q_ref, k_ref, v_ref, zb_ref, qm_ref, gate_ref,
                 *, split_pairs, tiles_per_seq, base_offset):
    tm = x_ref.shape[0]
    base = base_offset + lax.rem(pl.program_id(0), tiles_per_seq)
    cb = cb_ref[pl.ds(base, 1), :]
    sb = sb_ref[pl.ds(base, 1), :]
    second = (lax.broadcasted_iota(jnp.int32, (1, LANES), 1) & 8) != 0

    def put(ref, p, rows, val):
        if split_pairs:
            ref[p, rows, :] = val
        else:
            ref[rows, p * LANES:(p + 1) * LANES] = val

    nsplit = 2 if tm % 16 == 0 else 1
    hm = tm // nsplit
    for part in range(nsplit):
        rows = pl.ds(part * hm, hm)
        x = x_ref[rows, :]
        ms = jnp.mean(x * x, axis=-1, keepdims=True)
        h = ((x * lax.rsqrt(ms + NORM_EPS)) * g_ref[...]).astype(BF16)
        cr = cr_ref[rows, :]
        sr = sr_ref[rows, :]
        cos = cb * cr - sb * sr
        sin = sb * cr + cb * sr
        sin_up = jnp.where(second, sin, 0.0)
        sin_dn = jnp.where(second, 0.0, -sin)

        def rope(t):
            return t * cos + pltpu.roll(t, 8, 1) * sin_up + pltpu.roll(t, LANES - 8, 1) * sin_dn

        qkv = jnp.dot(h, w_ref[:, :3 * ATT_W], preferred_element_type=F32)
        for p in range(ATT_W // LANES):
            put(q_ref, p, rows, rope(qkv[:, p * LANES:(p + 1) * LANES]) * Q_SCALE)
            put(k_ref, p, rows, rope(qkv[:, ATT_W + p * LANES:ATT_W + (p + 1) * LANES]))
            put(v_ref, p, rows, qkv[:, 2 * ATT_W + p * LANES:2 * ATT_W + (p + 1) * LANES])
        c0 = 3 * ATT_W
        zb_ref[rows, :] = jnp.dot(h, w_ref[:, c0:c0 + SHIFT_W], preferred_element_type=F32)
        c0 += SHIFT_W
        qm_ref[rows, :] = (jnp.dot(h, w_ref[:, c0:c0 + MEM_W], preferred_element_type=F32) * Q_SCALE).astype(
            qm_ref.dtype)
        c0 += MEM_W
        gate_ref[rows, :] = jnp.dot(h, w_ref[:, c0:c0 + MIX_W], preferred_element_type=F32).astype(gate_ref.dtype)


def _proj(x2d, cr, sr, cb, sb, g, w_bf16, tm, split_pairs, tiles_per_seq, base_offset, act):
    rows = x2d.shape[0]
    in_w = w_bf16.shape[1]
    if split_pairs:
        qkv_shape = jax.ShapeDtypeStruct((3, rows, LANES), F32)
        qkv_spec = pl.BlockSpec((3, tm, LANES), lambda i: (0, i, 0))
    else:
        qkv_shape = jax.ShapeDtypeStruct((rows, ATT_W), F32)
        qkv_spec = pl.BlockSpec((tm, ATT_W), lambda i: (i, 0))
    row = lambda w: pl.BlockSpec((tm, w), lambda i: (i, 0))
    full = lambda a: pl.BlockSpec(a.shape, lambda i: (0, 0))
    return pl.pallas_call(
        functools.partial(_proj_kernel, split_pairs=split_pairs, tiles_per_seq=tiles_per_seq,
                          base_offset=base_offset),
        grid=(rows // tm,),
        in_specs=[row(D_MODEL), full(cr), full(sr), full(cb), full(sb), full(g), full(w_bf16)],
        out_specs=[qkv_spec, qkv_spec, qkv_spec, row(SHIFT_W), row(MEM_W), row(MIX_W)],
        out_shape=[qkv_shape, qkv_shape, qkv_shape,
                   jax.ShapeDtypeStruct((rows, SHIFT_W), F32),
                   jax.ShapeDtypeStruct((rows, MEM_W), act),
                   jax.ShapeDtypeStruct((rows, MIX_W), act)],
        compiler_params=_cparams(("arbitrary",)),
        name="proj",
    )(x2d, cr, sr, cb, sb, g, w_bf16)


def _attn_blocks(npairs, d, q_ref, kc_ref, vc_ref, blocks):
    h0 = _half0((BAND, LANES))
    h0k = _half0((2 * BAND, LANES))
    units = [(b, p) for b in range(len(blocks)) for p in range(npairs)]
    heads = [(i, hh) for i in range(len(units)) for hh in range(2)]
    row = [pl.ds(blk[2], BAND, stride=d) for blk in blocks]
    prow = [pl.ds(blk[3], BAND, stride=d) for blk in blocks]
    q = [q_ref[p, row[b], :] for b, p in units]
    kcat = [jnp.concatenate([blocks[b][0][p, prow[b], :], kc_ref[p, row[b], :]], axis=0).astype(BF16)
            for b, p in units]
    vcat = [jnp.concatenate([blocks[b][1][p, prow[b], :], vc_ref[p, row[b], :]], axis=0) for b, p in units]
    s = [_dot_nt(jnp.where(h0 if hh == 0 else jnp.logical_not(h0), q[i], 0.0), kcat[i]) + blocks[units[i][0]][4]
         for i, hh in heads]
    m = [jnp.max(s[j], axis=-1, keepdims=True) for j in range(len(heads))]
    pexp = [jnp.exp2(s[j] - m[j]).astype(BF16) for j in range(len(heads))]
    res = [jnp.dot(pexp[j], jnp.where(h0k if hh == 0 else jnp.logical_not(h0k), vcat[i], 1.0).astype(BF16),
                   preferred_element_type=F32) for j, (i, hh) in enumerate(heads)]
    lsum = [pltpu.roll(jnp.where(h0, res[2 * i + 1], res[2 * i]), HEAD_DIM, 1) for i in range(len(units))]
    outs = [[] for _ in blocks]
    for i, (b, p) in enumerate(units):
        outs[b].append((jnp.where(h0, res[2 * i], res[2 * i + 1]) / lsum[i],
                        jnp.where(h0, m[2 * i], m[2 * i + 1]) + jnp.log2(lsum[i])))
    return outs


def _attn_kernel(q_ref, kc_ref, kp_ref, vc_ref, vp_ref, o_ref, o_sc, lse_sc):
    has_prev = pl.program_id(1) > 0
    qi = lax.broadcasted_iota(jnp.int32, (BAND, 2 * BAND), 0)
    kj = lax.broadcasted_iota(jnp.int32, (BAND, 2 * BAND), 1)
    in_cur = (kj >= BAND) & (kj - BAND <= qi)
    in_prev = (kj < BAND) & (kj >= qi)
    bias_full = jnp.where(in_cur | in_prev, 0.0, NEG).astype(F32)
    bias_first = jnp.where(in_cur | (in_prev & has_prev), 0.0, NEG).astype(F32)
    npairs = q_ref.shape[0]
    nblk = ATT_TILE // BAND
    per_iter = 2

    def desc(d, u, first):
        span = BAND * d
        if first:
            return (kp_ref, vp_ref, u, ATT_TILE - span + u, bias_first)
        qs = (u >> (d.bit_length() - 1)) * span + (u & (d - 1))
        return (kc_ref, vc_ref, qs, qs - span, bias_full)

    def merge(outs, start):
        row = pl.ds(start, BAND)
        for p, (o1, lse1) in enumerate(outs):
            os_ = [o1] + [o_sc[j, p, row, :] for j in range(len(DILATIONS) - 1)]
            ls_ = [lse1] + [lse_sc[j, p, row, :] for j in range(len(DILATIONS) - 1)]
            top = functools.reduce(jnp.maximum, ls_)
            ws = [jnp.exp2(l - top) for l in ls_]
            num = functools.reduce(lambda a, b: a + b, [w * o for w, o in zip(ws, os_)])
            o_ref[p, row, :] = (num / functools.reduce(lambda a, b: a + b, ws)).astype(o_ref.dtype)

    def run(j, d, u0, first):
        blocks = [desc(d, u0 + i, first) for i in range(per_iter)]
        for blk, outs in zip(blocks, _attn_blocks(npairs, d, q_ref, kc_ref, vc_ref, blocks)):
            if j < 0:
                merge(outs, blk[2] if first else pl.multiple_of(blk[2], BAND))
            else:
                for p, (o, lse) in enumerate(outs):
                    o_sc[j, p, pl.ds(blk[2], BAND, stride=d), :] = o
                    lse_sc[j, p, pl.ds(blk[2], BAND, stride=d), :] = lse

    def sweep(j, d):
        n_first = d
        if n_first % per_iter == 0:
            lax.fori_loop(0, n_first // per_iter, lambda i, c: (run(j, d, i * per_iter, True), c)[1], 0)
            lo = n_first
        else:
            assert n_first == 1 and per_iter == 2
            blocks = [desc(d, 0, True), desc(d, 1, False)]
            for blk, outs in zip(blocks, _attn_blocks(npairs, d, q_ref, kc_ref, vc_ref, blocks)):
                merge(outs, blk[2])
            lo = per_iter
        if lo < nblk:
            lax.fori_loop(lo // per_iter, nblk // per_iter, lambda i, c: (run(j, d, i * per_iter, False), c)[1], 0)

    for j, d in enumerate(DILATIONS[1:]):
        sweep(j, d)
    sweep(-1, DILATIONS[0])


def _attn_prompt(q3, k3, v3, batch, seq):
    nt = seq // ATT_TILE
    cur = pl.BlockSpec((3, ATT_TILE, LANES), lambda b, i: (0, b * nt + i, 0))
    prev = pl.BlockSpec((3, ATT_TILE, LANES), lambda b, i: (0, b * nt + jnp.maximum(i - 1, 0), 0))
    return pl.pallas_call(
        _attn_kernel,
        grid=(batch, nt),
        in_specs=[cur, cur, prev, cur, prev],
        out_specs=cur,
        out_shape=jax.ShapeDtypeStruct(q3.shape, BF16),
        scratch_shapes=[pltpu.VMEM((len(DILATIONS) - 1, 3, ATT_TILE, LANES), F32),
                        pltpu.VMEM((len(DILATIONS) - 1, 3, ATT_TILE, LANES), F32)],
        compiler_params=_cparams(("arbitrary", "arbitrary")),
        name="attn_prompt",
    )(q3, k3, k3, v3, v3)


def _attn_sample_kernel(q_ref, kn_ref, vn_ref, kc_ref, vc_ref, o_ref, *, n_past, t_new, nb):
    nh = ATT_W // HEAD_DIM
    head_of_lane = lax.broadcasted_iota(jnp.int32, (t_new, ATT_W), 1) >> 6
    seqs = range(nb)
    qst = [jnp.concatenate([jnp.where(head_of_lane == h, q_ref[b], 0.0) for h in range(nh)], axis=0) for b in seqs]

    def count(delta):
        c = jnp.zeros(delta.shape, F32)
        for d in DILATIONS:
            ok = (delta >= 0) & (delta <= BAND * d) & ((delta & (d - 1)) == 0)
            c = c + jnp.where(ok, 1.0, 0.0)
        return c

    rows = nh * t_new
    t_past = lax.broadcasted_iota(jnp.int32, (rows, n_past), 0) & (t_new - 1)
    cnt_past = count(n_past + t_past - lax.broadcasted_iota(jnp.int32, (rows, n_past), 1))
    t_n = lax.broadcasted_iota(jnp.int32, (rows, t_new), 0) & (t_new - 1)
    cnt_new = count(t_n - lax.broadcasted_iota(jnp.int32, (rows, t_new), 1))

    s_past = [jnp.where(cnt_past > 0, _dot(qst[b], kc_ref[b]), NEG) for b in seqs]
    s_new = [jnp.where(cnt_new > 0, _dot_nt(qst[b], kn_ref[b]), NEG) for b in seqs]
    m = [jnp.maximum(jnp.max(s_past[b], axis=-1, keepdims=True), jnp.max(s_new[b], axis=-1, keepdims=True))
         for b in seqs]
    p_past = [cnt_past * jnp.exp2(s_past[b] - m[b]) for b in seqs]
    p_new = [cnt_new * jnp.exp2(s_new[b] - m[b]) for b in seqs]
    l = [jnp.sum(p_past[b], axis=-1, keepdims=True) + jnp.sum(p_new[b], axis=-1, keepdims=True) for b in seqs]
    o = [(_dot_nt(p_past[b], vc_ref[b]) + _dot(p_new[b], vn_ref[b])) / l[b] for b in seqs]
    for b in seqs:
        out = jnp.zeros((t_new, ATT_W), F32)
        for h in range(nh):
            out = jnp.where(head_of_lane == h, o[b][h * t_new:(h + 1) * t_new, :], out)
        o_ref[b] = out.astype(o_ref.dtype)


def _attn_sample(q, k_new, v_new, k_cache, v_cache, nb):
    db, t_new, _ = q.shape
    n_past = k_cache.shape[2]
    new = pl.BlockSpec((nb, t_new, ATT_W), lambda b: (b, 0, 0))
    cache = pl.BlockSpec((nb, ATT_W, n_past), lambda b: (b, 0, 0))
    return pl.pallas_call(
        functools.partial(_attn_sample_kernel, n_past=n_past, t_new=t_new, nb=nb),
        grid=(db // nb,),
        in_specs=[new, new, new, cache, cache],
        out_specs=new,
        out_shape=jax.ShapeDtypeStruct(q.shape, F32),
        compiler_params=_cparams(("arbitrary",)),
        name="attn_sample",
    )(q, k_new, v_new, k_cache, v_cache)


def _seg_sum(x, h0):
    s0 = jnp.sum(jnp.where(h0, x, 0.0), axis=-1, keepdims=True)
    s1 = jnp.sum(jnp.where(h0, 0.0, x), axis=-1, keepdims=True)
    return jnp.where(h0, s0, s1)


def _split_bf16(x):
    hi = x.astype(BF16)
    return hi, (x - hi.astype(F32)).astype(BF16)


def _rwkv_kernel(z_ref, sh0_ref, st0_ref, mu_ref, w0_ref, a0_ref, whi_ref, wlo_ref, kk_ref, ka_ref, rk_ref,
                 lng_ref, lnb_ref, o_ref, st_ref, st_sc, prev_sc, *, nb, groups, n_valid):
    C = RWKV_CHUNK
    R = groups * C
    npairs = RWKV_W // LANES

    @pl.when(pl.program_id(1) == 0)
    def _():
        st_sc[...] = st0_ref[...]
        prev_sc[...] = sh0_ref[...]

    si = lax.broadcasted_iota(jnp.int32, (2 * C, 2 * C), 0)
    sj = lax.broadcasted_iota(jnp.int32, (2 * C, 2 * C), 1)
    same_head = (si >= C) == (sj >= C)
    ti2 = si & (C - 1)
    tj2 = sj & (C - 1)
    incl = same_head & (tj2 <= ti2)
    strict = same_head & (tj2 < ti2)
    eye = jnp.where(si == sj, 1.0, 0.0).astype(F32)
    levels = []
    s = 2
    while s < min(C, n_valid):
        sh = s.bit_length() - 1
        levels.append(((ti2 >> (sh + 1)) == (tj2 >> (sh + 1))) & (((ti2 >> sh) & 1) == 1) & (((tj2 >> sh) & 1) == 0))
        s *= 2
    first_level = (ti2 >> 1) == (tj2 >> 1)

    ri = lax.broadcasted_iota(jnp.int32, (C, C), 0)
    rj = lax.broadcasted_iota(jnp.int32, (C, C), 1)
    tri = jnp.where(rj <= ri, 1.0, 0.0).astype(BF16)
    rowid = lax.broadcasted_iota(jnp.int32, (C, 1), 0)
    h0 = _half0((C, LANES))
    zero_blk = jnp.zeros((2 * C, LANES), F32)
    dot = functools.partial(jnp.dot, preferred_element_type=F32)
    pre = {}

    def stack(x):
        return jnp.concatenate([jnp.where(h0, x, 0.0), jnp.where(h0, 0.0, x)], axis=0)

    def prep(b, g):
        if n_valid < C:
            z = jnp.concatenate([z_ref[b], jnp.zeros((C - n_valid, SHIFT_W), F32)], axis=0)
        else:
            z = z_ref[b, g * C:(g + 1) * C, :]
        before = prev_sc[b] if g == 0 else z_ref[b, pl.ds(g * C - 1, 1), :]
        z_prev = jnp.where(rowid == 0, before, pltpu.roll(z, 1, 0))
        zs = z + (z_prev - z) * mu_ref[...]
        valid = rowid < (n_valid - g * C)
        if n_valid < (g + 1) * C:
            zs = jnp.where(valid, zs, 0.0)
        lat = zs[:, 3 * RWKV_W:]
        lat_hi, lat_lo = _split_bf16(jnp.where(_half0(lat.shape), jnp.tanh(lat), lat))
        lora = dot(lat_hi, whi_ref[...]) + dot(lat_lo, whi_ref[...]) + dot(lat_hi, wlo_ref[...])
        w = -jax.nn.softplus(-(w0_ref[...] + lora[:, :RWKV_W])) - 0.5
        lw = -jnp.exp(w)
        if n_valid < (g + 1) * C:
            lw = jnp.where(valid, lw, 0.0)
        a = jax.nn.sigmoid(a0_ref[...] + lora[:, RWKV_W:])
        lw_hi, lw_lo = _split_bf16(lw)
        cum = dot(tri, lw_hi) + dot(tri, lw_lo)
        for p in range(npairs):
            sl = slice(p * LANES, (p + 1) * LANES)
            r = zs[:, sl]
            k = zs[:, RWKV_W + p * LANES:RWKV_W + (p + 1) * LANES]
            v = zs[:, 2 * RWKV_W + p * LANES:2 * RWKV_W + (p + 1) * LANES]
            ap = a[:, sl]
            kk = k * kk_ref[:, sl]
            kk = kk * jnp.minimum(lax.rsqrt(_seg_sum(kk * kk, h0)), 1e12)
            k = k * (1.0 + (ap - 1.0) * ka_ref[:, sl])
            bb = kk * ap
            L = cum[:, sl]
            l_end = L[C - 1:C, :]
            e_out = jnp.exp(-L)
            e_end = jnp.exp(l_end - L)
            rt = stack(r * jnp.exp(L))
            kkh = stack(kk * jnp.exp(L - lw[:, sl]))
            left = jnp.concatenate([rt, kkh], axis=0).astype(BF16)
            right = jnp.concatenate([stack(k * e_out), stack(-(bb * e_out))], axis=0).astype(BF16)
            ends = jnp.concatenate([stack(k * e_end), stack(-(bb * e_end))], axis=0).astype(BF16)
            pre[(b, g, p)] = dict(left=left, right=right, ends=ends, rt=rt, kkh=kkh, v2=stack(v),
                                  decay=jnp.exp(l_end), bonus=_seg_sum(r * k * rk_ref[:, sl], h0) * v)

    def transition_stages(chains):
        def scores():
            for c in chains:
                d = pre[c]
                aa = lax.dot_general(d["left"], d["right"], (((1,), (1,)), ((), ())), preferred_element_type=F32)
                d["a_r"] = jnp.concatenate([jnp.where(incl, aa[:2 * C, :2 * C], 0.0),
                                            jnp.where(incl, aa[:2 * C, 2 * C:], 0.0)], axis=1).astype(BF16)
                d["a_kk"] = jnp.where(strict, aa[2 * C:, :2 * C], 0.0)
                d["n_kb"] = jnp.where(strict, aa[2 * C:, 2 * C:], 0.0)
                d["t"] = eye + jnp.where(first_level, d["n_kb"], 0.0)
        def odd_rows(x, s):
            return jnp.concatenate([x[i:i + s] for i in range(s, 2 * C, 2 * s)], axis=0) if s >= 8 else x

        def level_a(off, s):
            for c in chains:
                pre[c]["tn"] = _dot(odd_rows(pre[c]["t"], s), jnp.where(off, pre[c]["n_kb"], 0.0))
        def level_b(s):
            for c in chains:
                t = pre[c]["t"]
                new = odd_rows(t, s) + _dot(pre[c]["tn"], t)
                if s >= 8:
                    parts = []
                    for j, i in enumerate(range(0, 2 * C, 2 * s)):
                        parts += [t[i:i + s], new[j * s:(j + 1) * s]]
                    new = jnp.concatenate(parts, axis=0)
                pre[c]["t"] = new
        def akv():
            for c in chains:
                pre[c]["akv"] = _dot(pre[c]["a_kk"], pre[c]["v2"])
        def solve():
            for c in chains:
                d = pre[c]
                tr = _dot(d["t"], jnp.concatenate([d["kkh"], d["akv"]], axis=1))
                kq, u0 = tr[:, :LANES], tr[:, LANES:]
                d["big"] = jnp.concatenate([jnp.concatenate([d["v2"], zero_blk], axis=1),
                                            jnp.concatenate([u0, kq], axis=1)], axis=0).astype(BF16)
        def readout():
            for c in chains:
                d = pre[c]
                yr = dot(d["a_r"], d["big"])
                d["y0"] = yr[:, :LANES]
                d["rq"] = (d["rt"] + yr[:, LANES:]).astype(BF16)
        def update():
            for c in chains:
                d = pre[c]
                dg = lax.dot_general(d["big"], d["ends"], (((0,), (0,)), ((), ())), preferred_element_type=F32)
                d["dd"] = dg[:LANES]
                d["gm"] = dg[LANES:].astype(BF16)
        stages = [scores]
        for i, off in enumerate(levels):
            stages += [functools.partial(level_a, off, 2 << i), functools.partial(level_b, 2 << i)]
        return stages + [akv, solve, readout, update]

    state = {(b, p): st_sc[b, p] for b in range(nb) for p in range(npairs)}

    def carried(chains):
        for (b, g, p) in chains:
            d = pre[(b, g, p)]
            st = state[(b, p)]
            st_b = st.astype(BF16)
            d["y2"] = lax.dot_general(d["rq"], st_b, (((1,), (1,)), ((), ())), preferred_element_type=F32) + d["y0"]
            state[(b, p)] = st * d["decay"] + dot(st_b, d["gm"]) + d["dd"]
        for (b, g, p) in chains:
            d = pre.pop((b, g, p))
            y2 = d["y2"]
            y = y2[:C] + y2[C:]
            sl = slice(p * LANES, (p + 1) * LANES)
            mean = _seg_sum(y, h0) * (1.0 / HEAD_DIM)
            yc = y - mean
            var = _seg_sum(yc * yc, h0) * (1.0 / HEAD_DIM)
            yn = yc * lax.rsqrt(var + LNX_EPS) * lng_ref[:, sl] + lnb_ref[:, sl]
            rows_out = min(C, n_valid - g * C)
            o_ref[b, g * C:g * C + rows_out, sl] = (yn + d["bonus"])[:rows_out].astype(o_ref.dtype)

    items = [(b, g) for g in range(groups) for b in range(nb)]
    for item in items:
        prep(*item)
    chains = [(b, g, p) for (b, g) in items for p in range(npairs)]
    for stage in transition_stages(chains):
        stage()
    carried(chains)
    for b in range(nb):
        prev_sc[b] = z_ref[b, pl.ds(min(R, n_valid) - 1, 1), :]
    for (b, p), st in state.items():
        st_sc[b, p] = st

    @pl.when(pl.program_id(1) == pl.num_programs(1) - 1)
    def _():
        st_ref[...] = st_sc[...]


def _rwkv(z, shift0, st0, mu, w0, a0, w_hi, w_lo, k_k, k_a, r_k, ln_g, ln_b, nb, groups, act):
    b, s, _ = z.shape
    rows = groups * RWKV_CHUNK
    if s % rows:
        assert groups == 1 and s < rows and s % 8 == 0
        rows = s
    n_valid = rows
    vec = lambda w: pl.BlockSpec((1, w), lambda i, j: (0, 0))
    st_spec = pl.BlockSpec((nb, 3, LANES, LANES), lambda i, j: (i, 0, 0, 0))
    wspec = pl.BlockSpec((LANES, 2 * RWKV_W), lambda i, j: (0, 0))
    return pl.pallas_call(
        functools.partial(_rwkv_kernel, nb=nb, groups=groups, n_valid=n_valid),
        grid=(b // nb, s // rows),
        in_specs=[pl.BlockSpec((nb, rows, SHIFT_W), lambda i, j: (i, j, 0)),
                  pl.BlockSpec((nb, 1, SHIFT_W), lambda i, j: (i, 0, 0)),
                  st_spec,
                  vec(SHIFT_W), vec(RWKV_W), vec(RWKV_W), wspec, wspec,
                  vec(RWKV_W), vec(RWKV_W), vec(RWKV_W), vec(RWKV_W), vec(RWKV_W)],
        out_specs=[pl.BlockSpec((nb, rows, RWKV_W), lambda i, j: (i, j, 0)), st_spec],
        out_shape=[jax.ShapeDtypeStruct((b, s, RWKV_W), act),
                   jax.ShapeDtypeStruct((b, 3, LANES, LANES), F32)],
        scratch_shapes=[pltpu.VMEM((nb, 3, LANES, LANES), F32), pltpu.VMEM((nb, 1, SHIFT_W), F32)],
        compiler_params=_cparams(("arbitrary", "arbitrary")),
        name="rwkv",
    )(z, shift0, st0, mu, w0, a0, w_hi, w_lo, k_k, k_a, r_k, ln_g, ln_b)


def _state_to_pairs(st):
    b = st.shape[0]
    st = st.reshape(b, 3, 2, HEAD_DIM, HEAD_DIM)
    zero = jnp.zeros_like(st[:, :, 0])
    top = jnp.concatenate([st[:, :, 0], zero], axis=-1)
    bot = jnp.concatenate([zero, st[:, :, 1]], axis=-1)
    return jnp.concatenate([top, bot], axis=-2)


def _pairs_to_state(sp):
    b = sp.shape[0]
    return jnp.stack([sp[:, :, :HEAD_DIM, :HEAD_DIM], sp[:, :, HEAD_DIM:, HEAD_DIM:]], axis=2).reshape(
        b, 6, HEAD_DIM, HEAD_DIM)


def _memkv_kernel(x_ref, g_ref, w_ref, k_ref, v_ref):
    x = x_ref[...]
    ms = jnp.mean(x * x, axis=-1, keepdims=True)
    h = ((x * lax.rsqrt(ms + NORM_EPS)) * g_ref[...]).astype(BF16)
    k_ref[...] = jnp.dot(h, w_ref[:, :MEM_W], preferred_element_type=F32)
    v_ref[...] = jnp.dot(h, w_ref[:, MEM_W:], preferred_element_type=F32)


def _memkv(mem2d, g, w_bf16):
    rows = mem2d.shape[0]
    out = jax.ShapeDtypeStruct((rows, MEM_W), F32)
    return pl.pallas_call(
        _memkv_kernel,
        out_shape=[out, out],
        compiler_params=pltpu.CompilerParams(vmem_limit_bytes=VMEM_LIMIT),
        name="memkv",
    )(mem2d, g, w_bf16)


def _memattn_kernel(q_ref, mk_ref, mv_ref, o_ref, *, nb, mem_minor):
    head_of_lane = lax.broadcasted_iota(jnp.int32, q_ref.shape[1:], 1) >> 6
    qs = [q_ref[b] for b in range(nb)]
    mks = [mk_ref[b].astype(BF16) for b in range(nb)]
    mvs = [mv_ref[b].astype(BF16) for b in range(nb)]
    outs = [jnp.zeros(q_ref.shape[1:], F32) for _ in range(nb)]
    for h in range(MEM_W // HEAD_DIM):
        mine = head_of_lane == h
        qk = _dot if mem_minor else _dot_nt
        pv = _dot_nt if mem_minor else _dot
        s = [qk(jnp.where(mine, qs[b], 0.0), mks[b]) for b in range(nb)]
        pexp = [jnp.exp2(s[b] - jnp.max(s[b], axis=-1, keepdims=True)) for b in range(nb)]
        l = [jnp.sum(pexp[b], axis=-1, keepdims=True) for b in range(nb)]
        outs = [jnp.where(mine, pv(pexp[b], mvs[b]) / l[b], outs[b]) for b in range(nb)]
    for b in range(nb):
        o_ref[b] = outs[b].astype(o_ref.dtype)


def _memattn(q, mk, mv, tm, nb, mem_minor, act):
    b, s, _ = q.shape
    qspec = pl.BlockSpec((nb, tm, MEM_W), lambda i, j: (i, j, 0))
    mspec = pl.BlockSpec((nb,) + mk.shape[1:], lambda i, j: (i, 0, 0))
    return pl.pallas_call(
        functools.partial(_memattn_kernel, nb=nb, mem_minor=mem_minor),
        grid=(b // nb, s // tm),
        in_specs=[qspec, mspec, mspec],
        out_specs=qspec,
        out_shape=jax.ShapeDtypeStruct(q.shape, act),
        compiler_params=_cparams(("arbitrary", "arbitrary")),
        name="memattn",
    )(q, mk, mv)


def _out_kernel(oa_ref, ob_ref, om_ref, gate_ref, x_ref, w_ref, g_ref, y_ref, *, split_pairs):
    acc = x_ref[...]

    def add(acc, o, c0):
        width = o.shape[1]
        gate = gate_ref[:, c0:c0 + width]
        return acc + _dot(o * (gate * jax.nn.sigmoid(gate)), w_ref[c0:c0 + width, :])

    if split_pairs:
        for p in range(ATT_W // LANES):
            acc = add(acc, oa_ref[p], p * LANES)
    else:
        acc = add(acc, oa_ref[...], 0)
    acc = add(acc, ob_ref[...], ATT_W)
    acc = add(acc, om_ref[...], ATT_W + RWKV_W)
    ms = jnp.mean(acc * acc, axis=-1, keepdims=True)
    y_ref[...] = (acc * lax.rsqrt(ms + NORM_EPS)) * g_ref[...]


def _out(oa, ob, om, gate, x2d, w_bf16, g, tm, split_pairs):
    rows = x2d.shape[0]
    row = lambda w: pl.BlockSpec((tm, w), lambda i: (i, 0))
    oa_spec = pl.BlockSpec((3, tm, LANES), lambda i: (0, i, 0)) if split_pairs else row(ATT_W)
    return pl.pallas_call(
        functools.partial(_out_kernel, split_pairs=split_pairs),
        grid=(rows // tm,),
        in_specs=[oa_spec, row(RWKV_W), row(MEM_W), row(MIX_W), row(D_MODEL),
                  pl.BlockSpec((MIX_W, D_MODEL), lambda i: (0, 0)),
                  pl.BlockSpec((1, D_MODEL), lambda i: (0, 0))],
        out_specs=row(D_MODEL),
        out_shape=jax.ShapeDtypeStruct((rows, D_MODEL), F32),
        compiler_params=_cparams(("arbitrary",)),
        name="outproj",
    )(oa, ob, om, gate, x2d, w_bf16, g)


def kernel(x_prompt, x_sample, cache_win_k, cache_win_v, state_rwkv, state_rwkv_shift, cache_mem_k, cache_mem_v, mem_prompt, norm_in, w_in, rwkv_mu, rwkv_w0, rwkv_w2, rwkv_a0, rwkv_a2, rwkv_k_k, rwkv_k_a, rwkv_r_k, rwkv_lnx_g, rwkv_lnx_b, norm_mem, w_mem_kv, w_out, norm_final):
    B, S, _ = x_prompt.shape
    DB, T, _ = x_sample.shape
    depth = w_in.shape[0]
    assert depth == 1 and S % ATT_TILE == 0 and cache_win_k.shape[2] == max(DILATIONS) * BAND
    l = 0
    past_len = S

    half = ROPE_DIM // 2
    lane = jnp.arange(LANES)
    inv_freq = ROPE_THETA ** (-(lane % half).astype(F32) / half)
    invf = jnp.where((lane % HEAD_DIM) < ROPE_DIM, inv_freq, 0.0).reshape(1, LANES)

    row = lambda t: t.reshape(1, -1)
    w_in_b = w_in[l].astype(BF16)
    w_out_b = w_out[l].astype(BF16)
    w_kv_b = w_mem_kv[l].astype(BF16)
    zero = jnp.zeros((LORA_W, RWKV_W), F32)
    w2a2 = jnp.concatenate([jnp.concatenate([rwkv_w2[l], zero], axis=1),
                            jnp.concatenate([zero, rwkv_a2[l]], axis=1)], axis=0)
    w2a2_hi = w2a2.astype(BF16)
    w2a2_lo = (w2a2 - w2a2_hi.astype(F32)).astype(BF16)
    rw = (row(rwkv_mu[l]), row(rwkv_w0[l]), row(rwkv_a0[l]), w2a2_hi, w2a2_lo, row(rwkv_k_k[l]),
          row(rwkv_k_a[l]), row(rwkv_r_k[l]), row(rwkv_lnx_g[l]), row(rwkv_lnx_b[l]))

    tiles = S // PROJ_TILE
    n_base = -(-(tiles + 1) // 8) * 8
    pos_base = jnp.where(jnp.arange(n_base) < tiles, jnp.arange(n_base) * PROJ_TILE, past_len)
    cr, sr, cb, sb = _rope_tables(jnp.arange(PROJ_TILE, dtype=F32).reshape(-1, 1),
                                  pos_base.astype(F32).reshape(-1, 1), invf)

    xp = x_prompt.reshape(B * S, D_MODEL)
    q3, k3, v3, zb, qm, gate = _proj(xp, cr, sr, cb, sb, row(norm_in[l]), w_in_b, PROJ_TILE, True, tiles, 0, BF16)
    oa3 = _attn_prompt(q3, k3, v3, B, S)
    ob, st_p = _rwkv(zb.reshape(B, S, SHIFT_W), jnp.zeros((B, 1, SHIFT_W), F32),
                     jnp.zeros((B, 3, LANES, LANES), F32), *rw, nb=B, groups=2, act=BF16)
    mk, mv = _memkv(mem_prompt.reshape(B * N_MEM, D_MODEL), row(norm_mem[l]), w_kv_b)
    mk = mk.reshape(B, N_MEM, MEM_W)
    mv = mv.reshape(B, N_MEM, MEM_W)
    om = _memattn(qm.reshape(B, S, MEM_W), mk, mv, 1024, 1, False, BF16)
    y_p = _out(oa3, ob.reshape(B * S, RWKV_W), om.reshape(B * S, MEM_W), gate, xp, w_out_b,
               row(norm_final), 1024, True)

    win = min(max(DILATIONS) * BAND, S)
    tail = lambda t3: jnp.transpose(t3.reshape(3, B, S, LANES)[:, :, S - win:], (1, 2, 0, 3)).reshape(
        1, B, win, ATT_W // HEAD_DIM, HEAD_DIM)
    heads = lambda t, n: t.reshape(1, t.shape[0], t.shape[1], n, HEAD_DIM)

    xs = x_sample.reshape(DB * T, D_MODEL)
    qs, ks, vs, zbs, qms, gates = _proj(xs, jnp.tile(cr[:T], (DB, 1)), jnp.tile(sr[:T], (DB, 1)), cb, sb,
                                        row(norm_in[l]), w_in_b, DB * T, False, 1, tiles, F32)
    n_past = cache_win_k.shape[2]
    minor = lambda c: jnp.transpose(c, (0, 2, 3, 1)).reshape(c.shape[0], c.shape[2] * c.shape[3], c.shape[1])
    oas = _attn_sample(qs.reshape(DB, T, ATT_W), ks.reshape(DB, T, ATT_W), vs.reshape(DB, T, ATT_W),
                       minor(cache_win_k[l]), minor(cache_win_v[l]), 2)
    zbs3 = zbs.reshape(DB, T, SHIFT_W)
    obs, st_s = _rwkv(zbs3, state_rwkv_shift[l].reshape(DB, 1, SHIFT_W), _state_to_pairs(state_rwkv[l]),
                      *rw, nb=4, groups=1, act=F32)
    oms = _memattn(qms.reshape(DB, T, MEM_W), minor(cache_mem_k[l]), minor(cache_mem_v[l]), T, 8, True, F32)
    y_s = _out(oas.reshape(DB * T, ATT_W), obs.reshape(DB * T, RWKV_W), oms.reshape(DB * T, MEM_W), gates, xs,
               w_out_b, row(norm_final), DB * T, False)

    return (y_p.reshape(B, S, D_MODEL), y_s.reshape(DB, T, D_MODEL),
            tail(k3), tail(v3),
            _pairs_to_state(st_p)[None], zb.reshape(B, S, SHIFT_W)[:, -1][None],
            heads(mk, MEM_W // HEAD_DIM), heads(mv, MEM_W // HEAD_DIM),
            heads(ks.reshape(DB, T, ATT_W), ATT_W // HEAD_DIM), heads(vs.reshape(DB, T, ATT_W), ATT_W // HEAD_DIM),
            _pairs_to_state(st_s)[None], zbs3[:, -1][None])
```

```python
import functools

import jax
import jax.numpy as jnp
from jax import lax
from jax.experimental import pallas as pl
from jax.experimental.pallas import tpu as pltpu

F32 = jnp.float32
BF16 = jnp.bfloat16

D_MODEL = 1024
HEAD_DIM = 64
ATT_W = 384
RWKV_W = 384
MEM_W = 256
MIX_W = 1024
LORA_W = 64
SHIFT_W = 3 * RWKV_W + 2 * LORA_W
N_MEM = 256
ROPE_DIM = 16
ROPE_THETA = 500000.0
NORM_EPS = 1e-6
LNX_EPS = 64e-5
DILATIONS = (1, 4, 16)
BAND = 128
ATT_TILE = BAND * max(DILATIONS)
LANES = 128
RWKV_CHUNK = 64
PROJ_TILE = 512
NEG = -1e30
Q_SCALE = HEAD_DIM ** -0.5 * 1.4426950408889634
VMEM_LIMIT = 56 * 1024 * 1024


def _cparams(sem):
    return pltpu.CompilerParams(dimension_semantics=sem, vmem_limit_bytes=VMEM_LIMIT)


def _dot(a, b):
    return jnp.dot(a.astype(BF16), b.astype(BF16), preferred_element_type=F32)


def _dot_nt(a, b):
    return lax.dot_general(a.astype(BF16), b.astype(BF16), (((1,), (1,)), ((), ())),
                           preferred_element_type=F32)


def _dot_tn(a, b):
    return lax.dot_general(a.astype(BF16), b.astype(BF16), (((0,), (0,)), ((), ())),
                           preferred_element_type=F32)


def _dot_f32(a, b):
    return jnp.dot(a, b, preferred_element_type=F32, precision=lax.Precision.HIGHEST)


def _half0(shape):
    return (lax.broadcasted_iota(jnp.int32, shape, len(shape) - 1) & 64) == 0


def _rope_table_kernel(pos_row_ref, pos_base_ref, invf_ref, cr_ref, sr_ref, cb_ref, sb_ref):
    ang_r = pos_row_ref[...] * invf_ref[...]
    cr_ref[...] = jnp.cos(ang_r)
    sr_ref[...] = jnp.sin(ang_r)
    ang_b = pos_base_ref[...] * invf_ref[...]
    cb_ref[...] = jnp.cos(ang_b)
    sb_ref[...] = jnp.sin(ang_b)


def _rope_tables(pos_row, pos_base, invf):
    tab = lambda n: jax.ShapeDtypeStruct((n, LANES), F32)
    return pl.pallas_call(
        _rope_table_kernel,
        out_shape=[tab(pos_row.shape[0]), tab(pos_row.shape[0]), tab(pos_base.shape[0]), tab(pos_base.shape[0])],
        name="rope_tables",
    )(pos_row, pos_base, invf)


def _proj_kernel(x_ref, cr_ref, sr_ref, cb_ref, sb_ref, g_ref, w_ref, q_ref, k_ref, v_ref, zb_ref, qm_ref, gate_ref,
                 *, split_pairs, tiles_per_seq, base_offset):
    tm = x_ref.shape[0]
    base = base_offset + lax.rem(pl.program_id(0), tiles_per_seq)
    cb = cb_ref[pl.ds(base, 1), :]
    sb = sb_ref[pl.ds(base, 1), :]
    second = (lax.broadcasted_iota(jnp.int32, (1, LANES), 1) & 8) != 0

    def put(ref, p, rows, val):
        if split_pairs:
            ref[p, rows, :] = val
        else:
            ref[rows, p * LANES:(p + 1) * LANES] = val

    nsplit = 2 if tm % 16 == 0 else 1
    hm = tm // nsplit
    for part in range(nsplit):
        rows = pl.ds(part * hm, hm)
        x = x_ref[rows, :]
        ms = jnp.mean(x * x, axis=-1, keepdims=True)
        h = ((x * lax.rsqrt(ms + NORM_EPS)) * g_ref[...]).astype(BF16)
        cr = cr_ref[rows, :]
        sr = sr_ref[rows, :]
        cos = cb * cr - sb * sr
        sin = sb * cr + cb * sr
        sin_up = jnp.where(second, sin, 0.0)
        sin_dn = jnp.where(second, 0.0, -sin)

        def rope(t):
            return t * cos + pltpu.roll(t, 8, 1) * sin_up + pltpu.roll(t, LANES - 8, 1) * sin_dn

        qkv = jnp.dot(h, w_ref[:, :3 * ATT_W], preferred_element_type=F32)
        for p in range(ATT_W // LANES):
            put(q_ref, p, rows, rope(qkv[:, p * LANES:(p + 1) * LANES]) * Q_SCALE)
            put(k_ref, p, rows, rope(qkv[:, ATT_W + p * LANES:ATT_W + (p + 1) * LANES]))
            put(v_ref, p, rows, qkv[:, 2 * ATT_W + p * LANES:2 * ATT_W + (p + 1) * LANES])
        c0 = 3 * ATT_W
        zb_ref[rows, :] = jnp.dot(h, w_ref[:, c0:c0 + SHIFT_W], preferred_element_type=F32)
        c0 += SHIFT_W
        qm_ref[rows, :] = (jnp.dot(h, w_ref[:, c0:c0 + MEM_W], preferred_element_type=F32) * Q_SCALE).astype(
            qm_ref.dtype)
        c0 += MEM_W
        gate_ref[rows, :] = jnp.dot(h, w_ref[:, c0:c0 + MIX_W], preferred_element_type=F32).astype(gate_ref.dtype)


def _proj(x2d, cr, sr, cb, sb, g, w_bf16, tm, split_pairs, tiles_per_seq, base_offset, act):
    rows = x2d.shape[0]
    in_w = w_bf16.shape[1]
    if split_pairs:
        qkv_shape = jax.ShapeDtypeStruct((3, rows, LANES), F32)
        qkv_spec = pl.BlockSpec((3, tm, LANES), lambda i: (0, i, 0))
    else:
        qkv_shape = jax.ShapeDtypeStruct((rows, ATT_W), F32)
        qkv_spec = pl.BlockSpec((tm, ATT_W), lambda i: (i, 0))
    row = lambda w: pl.BlockSpec((tm, w), lambda i: (i, 0))
    full = lambda a: pl.BlockSpec(a.shape, lambda i: (0, 0))
    return pl.pallas_call(
        functools.partial(_proj_kernel, split_pairs=split_pairs, tiles_per_seq=tiles_per_seq,
                          base_offset=base_offset),
        grid=(rows // tm,),
        in_specs=[row(D_MODEL), full(cr), full(sr), full(cb), full(sb), full(g), full(w_bf16)],
        out_specs=[qkv_spec, qkv_spec, qkv_spec, row(SHIFT_W), row(MEM_W), row(MIX_W)],
        out_shape=[qkv_shape, qkv_shape, qkv_shape,
                   jax.ShapeDtypeStruct((rows, SHIFT_W), F32),
                   jax.ShapeDtypeStruct((rows, MEM_W), act),
                   jax.ShapeDtypeStruct((rows, MIX_W), act)],
        compiler_params=_cparams(("arbitrary",)),
        name="proj",
    )(x2d, cr, sr, cb, sb, g, w_bf16)


def _attn_blocks(npairs, d, q_ref, kc_ref, vc_ref, blocks):
    h0 = _half0((BAND, LANES))
    h0k = _half0((2 * BAND, LANES))
    units = [(b, p) for b in range(len(blocks)) for p in range(npairs)]
    heads = [(i, hh) for i in range(len(units)) for hh in range(2)]
    row = [pl.ds(blk[2], BAND, stride=d) for blk in blocks]
    prow = [pl.ds(blk[3], BAND, stride=d) for blk in blocks]
    q = [q_ref[p, row[b], :] for b, p in units]
    kcat = [jnp.concatenate([blocks[b][0][p, prow[b], :], kc_ref[p, row[b], :]], axis=0).astype(BF16)
            for b, p in units]
    vcat = [jnp.concatenate([blocks[b][1][p, prow[b], :], vc_ref[p, row[b], :]], axis=0) for b, p in units]
    s = [_dot_nt(jnp.where(h0 if hh == 0 else jnp.logical_not(h0), q[i], 0.0), kcat[i]) + blocks[units[i][0]][4]
         for i, hh in heads]
    m = [jnp.max(s[j], axis=-1, keepdims=True) for j in range(len(heads))]
    pexp = [jnp.exp2(s[j] - m[j]).astype(BF16) for j in range(len(heads))]
    res = [jnp.dot(pexp[j], jnp.where(h0k if hh == 0 else jnp.logical_not(h0k), vcat[i], 1.0).astype(BF16),
                   preferred_element_type=F32) for j, (i, hh) in enumerate(heads)]
    lsum = [pltpu.roll(jnp.where(h0, res[2 * i + 1], res[2 * i]), HEAD_DIM, 1) for i in range(len(units))]
    outs = [[] for _ in blocks]
    for i, (b, p) in enumerate(units):
        outs[b].append((jnp.where(h0, res[2 * i], res[2 * i + 1]) / lsum[i],
                        jnp.where(h0, m[2 * i], m[2 * i + 1]) + jnp.log2(lsum[i])))
    return outs


def _attn_kernel(q_ref, kc_ref, kp_ref, vc_ref, vp_ref, o_ref, o_sc, lse_sc):
    has_prev = pl.program_id(1) > 0
    qi = lax.broadcasted_iota(jnp.int32, (BAND, 2 * BAND), 0)
    kj = lax.broadcasted_iota(jnp.int32, (BAND, 2 * BAND), 1)
    in_cur = (kj >= BAND) & (kj - BAND <= qi)
    in_prev = (kj < BAND) & (kj >= qi)
    bias_full = jnp.where(in_cur | in_prev, 0.0, NEG).astype(F32)
    bias_first = jnp.where(in_cur | (in_prev & has_prev), 0.0, NEG).astype(F32)
    npairs = q_ref.shape[0]
    nblk = ATT_TILE // BAND
    per_iter = 2

    def desc(d, u, first):
        span = BAND * d
        if first:
            return (kp_ref, vp_ref, u, ATT_TILE - span + u, bias_first)
        qs = (u >> (d.bit_length() - 1)) * span + (u & (d - 1))
        return (kc_ref, vc_ref, qs, qs - span, bias_full)

    def merge(outs, start):
        row = pl.ds(start, BAND)
        for p, (o1, lse1) in enumerate(outs):
            os_ = [o1] + [o_sc[j, p, row, :] for j in range(len(DILATIONS) - 1)]
            ls_ = [lse1] + [lse_sc[j, p, row, :] for j in range(len(DILATIONS) - 1)]
            top = functools.reduce(jnp.maximum, ls_)
            ws = [jnp.exp2(l - top) for l in ls_]
            num = functools.reduce(lambda a, b: a + b, [w * o for w, o in zip(ws, os_)])
            o_ref[p, row, :] = (num / functools.reduce(lambda a, b: a + b, ws)).astype(o_ref.dtype)

    def run(j, d, u0, first):
        blocks = [desc(d, u0 + i, first) for i in range(per_iter)]
        for blk, outs in zip(blocks, _attn_blocks(npairs, d, q_ref, kc_ref, vc_ref, blocks)):
            if j < 0:
                merge(outs, blk[2] if first else pl.multiple_of(blk[2], BAND))
            else:
                for p, (o, lse) in enumerate(outs):
                    o_sc[j, p, pl.ds(blk[2], BAND, stride=d), :] = o
                    lse_sc[j, p, pl.ds(blk[2], BAND, stride=d), :] = lse

    def sweep(j, d):
        n_first = d
        if n_first % per_iter == 0:
            lax.fori_loop(0, n_first // per_iter, lambda i, c: (run(j, d, i * per_iter, True), c)[1], 0)
            lo = n_first
        else:
            assert n_first == 1 and per_iter == 2
            blocks = [desc(d, 0, True), desc(d, 1, False)]
            for blk, outs in zip(blocks, _attn_blocks(npairs, d, q_ref, kc_ref, vc_ref, blocks)):
                merge(outs, blk[2])
            lo = per_iter
        if lo < nblk:
            lax.fori_loop(lo // per_iter, nblk // per_iter, lambda i, c: (run(j, d, i * per_iter, False), c)[1], 0)

    for j, d in enumerate(DILATIONS[1:]):
        sweep(j, d)
    sweep(-1, DILATIONS[0])


def _attn_prompt(q3, k3, v3, batch, seq):
    nt = seq // ATT_TILE
    cur = pl.BlockSpec((3, ATT_TILE, LANES), lambda b, i: (0, b * nt + i, 0))
    prev = pl.BlockSpec((3, ATT_TILE, LANES), lambda b, i: (0, b * nt + jnp.maximum(i - 1, 0), 0))
    return pl.pallas_call(
        _attn_kernel,
        grid=(batch, nt),
        in_specs=[cur, cur, prev, cur, prev],
        out_specs=cur,
        out_shape=jax.ShapeDtypeStruct(q3.shape, BF16),
        scratch_shapes=[pltpu.VMEM((len(DILATIONS) - 1, 3, ATT_TILE, LANES), F32),
                        pltpu.VMEM((len(DILATIONS) - 1, 3, ATT_TILE, LANES), F32)],
        compiler_params=_cparams(("arbitrary", "arbitrary")),
        name="attn_prompt",
    )(q3, k3, k3, v3, v3)


def _attn_sample_kernel(q_ref, kn_ref, vn_ref, kc_ref, vc_ref, o_ref, *, n_past, t_new, nb):
    nh = ATT_W // HEAD_DIM
    head_of_lane = lax.broadcasted_iota(jnp.int32, (t_new, ATT_W), 1) >> 6
    seqs = range(nb)
    qst = [jnp.concatenate([jnp.where(head_of_lane == h, q_ref[b], 0.0) for h in range(nh)], axis=0) for b in seqs]

    def count(delta):
        c = jnp.zeros(delta.shape, F32)
        for d in DILATIONS:
            ok = (delta >= 0) & (delta <= BAND * d) & ((delta & (d - 1)) == 0)
            c = c + jnp.where(ok, 1.0, 0.0)
        return c

    rows = nh * t_new
    t_past = lax.broadcasted_iota(jnp.int32, (rows, n_past), 0) & (t_new - 1)
    cnt_past = count(n_past + t_past - lax.broadcasted_iota(jnp.int32, (rows, n_past), 1))
    t_n = lax.broadcasted_iota(jnp.int32, (rows, t_new), 0) & (t_new - 1)
    cnt_new = count(t_n - lax.broadcasted_iota(jnp.int32, (rows, t_new), 1))

    s_past = [jnp.where(cnt_past > 0, _dot(qst[b], kc_ref[b]), NEG) for b in seqs]
    s_new = [jnp.where(cnt_new > 0, _dot_nt(qst[b], kn_ref[b]), NEG) for b in seqs]
    m = [jnp.maximum(jnp.max(s_past[b], axis=-1, keepdims=True), jnp.max(s_new[b], axis=-1, keepdims=True))
         for b in seqs]
    p_past = [cnt_past * jnp.exp2(s_past[b] - m[b]) for b in seqs]
    p_new = [cnt_new * jnp.exp2(s_new[b] - m[b]) for b in seqs]
    l = [jnp.sum(p_past[b], axis=-1, keepdims=True) + jnp.sum(p_new[b], axis=-1, keepdims=True) for b in seqs]
    o = [(_dot_nt(p_past[b], vc_ref[b]) + _dot(p_new[b], vn_ref[b])) / l[b] for b in seqs]
    for b in seqs:
        out = jnp.zeros((t_new, ATT_W), F32)
        for h in range(nh):
            out = jnp.where(head_of_lane == h, o[b][h * t_new:(h + 1) * t_new, :], out)
        o_ref[b] = out.astype(o_ref.dtype)


def _attn_sample(q, k_new, v_new, k_cache, v_cache, nb):
    db, t_new, _ = q.shape
    n_past = k_cache.shape[2]
    new = pl.BlockSpec((nb, t_new, ATT_W), lambda b: (b, 0, 0))
    cache = pl.BlockSpec((nb, ATT_W, n_past), lambda b: (b, 0, 0))
    return pl.pallas_call(
        functools.partial(_attn_sample_kernel, n_past=n_past, t_new=t_new, nb=nb),
        grid=(db // nb,),
        in_specs=[new, new, new, cache, cache],
        out_specs=new,
        out_shape=jax.ShapeDtypeStruct(q.shape, F32),
        compiler_params=_cparams(("arbitrary",)),
        name="attn_sample",
    )(q, k_new, v_new, k_cache, v_cache)


def _seg_sum(x, h0):
    s0 = jnp.sum(jnp.where(h0, x, 0.0), axis=-1, keepdims=True)
    s1 = jnp.sum(jnp.where(h0, 0.0, x), axis=-1, keepdims=True)
    return jnp.where(h0, s0, s1)


def _split_bf16(x):
    hi = x.astype(BF16)
    return hi, (x - hi.astype(F32)).astype(BF16)


def _rwkv_kernel(z_ref, sh0_ref, st0_ref, mu_ref, w0_ref, a0_ref, whi_ref, wlo_ref, kk_ref, ka_ref, rk_ref,
                 lng_ref, lnb_ref, o_ref, st_ref, st_sc, prev_sc, *, nb, groups, n_valid):
    C = RWKV_CHUNK
    R = groups * C
    npairs = RWKV_W // LANES

    @pl.when(pl.program_id(1) == 0)
    def _():
        st_sc[...] = st0_ref[...]
        prev_sc[...] = sh0_ref[...]

    si = lax.broadcasted_iota(jnp.int32, (2 * C, 2 * C), 0)
    sj = lax.broadcasted_iota(jnp.int32, (2 * C, 2 * C), 1)
    same_head = (si >= C) == (sj >= C)
    ti2 = si & (C - 1)
    tj2 = sj & (C - 1)
    incl = same_head & (tj2 <= ti2)
    strict = same_head & (tj2 < ti2)
    eye = jnp.where(si == sj, 1.0, 0.0).astype(F32)
    levels = []
    s = 2
    while s < min(C, n_valid):
        sh = s.bit_length() - 1
        levels.append(((ti2 >> (sh + 1)) == (tj2 >> (sh + 1))) & (((ti2 >> sh) & 1) == 1) & (((tj2 >> sh) & 1) == 0))
        s *= 2
    first_level = (ti2 >> 1) == (tj2 >> 1)

    ri = lax.broadcasted_iota(jnp.int32, (C, C), 0)
    rj = lax.broadcasted_iota(jnp.int32, (C, C), 1)
    tri = jnp.where(rj <= ri, 1.0, 0.0).astype(BF16)
    rowid = lax.broadcasted_iota(jnp.int32, (C, 1), 0)
    h0 = _half0((C, LANES))
    zero_blk = jnp.zeros((2 * C, LANES), BF16)
    dot = functools.partial(jnp.dot, preferred_element_type=F32)
    pre = {}

    def stack(x):
        return jnp.concatenate([jnp.where(h0, x, 0.0), jnp.where(h0, 0.0, x)], axis=0)

    def prep(b, g):
        if n_valid < C:
            z = jnp.concatenate([z_ref[b], jnp.zeros((C - n_valid, SHIFT_W), F32)], axis=0)
        else:
            z = z_ref[b, g * C:(g + 1) * C, :]
        before = prev_sc[b] if g == 0 else z_ref[b, pl.ds(g * C - 1, 1), :]
        z_prev = jnp.where(rowid == 0, before, pltpu.roll(z, 1, 0))
        zs = z + (z_prev - z) * mu_ref[...]
        valid = rowid < (n_valid - g * C)
        if n_valid < (g + 1) * C:
            zs = jnp.where(valid, zs, 0.0)
        lat = zs[:, 3 * RWKV_W:]
        lat_hi, lat_lo = _split_bf16(jnp.where(_half0(lat.shape), jnp.tanh(lat), lat))
        lora = dot(lat_hi, whi_ref[...]) + dot(lat_lo, whi_ref[...]) + dot(lat_hi, wlo_ref[...])
        w = -jax.nn.softplus(-(w0_ref[...] + lora[:, :RWKV_W])) - 0.5
        lw = -jnp.exp(w)
        if n_valid < (g + 1) * C:
            lw = jnp.where(valid, lw, 0.0)
        a = jax.nn.sigmoid(a0_ref[...] + lora[:, RWKV_W:])
        lw_hi, lw_lo = _split_bf16(lw)
        cum = dot(tri, lw_hi) + dot(tri, lw_lo)
        for p in range(npairs):
            sl = slice(p * LANES, (p + 1) * LANES)
            r = zs[:, sl]
            k = zs[:, RWKV_W + p * LANES:RWKV_W + (p + 1) * LANES]
            v = zs[:, 2 * RWKV_W + p * LANES:2 * RWKV_W + (p + 1) * LANES]
            ap = a[:, sl]
            kk = k * kk_ref[:, sl]
            kk = kk * jnp.minimum(lax.rsqrt(_seg_sum(kk * kk, h0)), 1e12)
            k = k * (1.0 + (ap - 1.0) * ka_ref[:, sl])
            bb = kk * ap
            L = cum[:, sl]
            l_end = L[C - 1:C, :]
            e_out = jnp.exp(-L)
            e_end = jnp.exp(l_end - L)
            rt = stack(r * jnp.exp(L))
            kkh = stack(kk * jnp.exp(L - lw[:, sl])).astype(BF16)
            left = jnp.concatenate([rt.astype(BF16), kkh], axis=0)
            right = jnp.concatenate([stack(k * e_out), stack(-(bb * e_out))], axis=0).astype(BF16)
            ends = jnp.concatenate([stack(k * e_end), stack(-(bb * e_end))], axis=0).astype(BF16)
            pre[(b, g, p)] = dict(left=left, right=right, ends=ends, rt=rt, kkh=kkh, v2=stack(v).astype(BF16),
                                  decay=jnp.exp(l_end), bonus=_seg_sum(r * k * rk_ref[:, sl], h0) * v)

    def transition_stages(chains):
        def scores():
            for c in chains:
                d = pre[c]
                aa = lax.dot_general(d["left"], d["right"], (((1,), (1,)), ((), ())), preferred_element_type=F32)
                d["a_r"] = jnp.concatenate([jnp.where(incl, aa[:2 * C, :2 * C], 0.0),
                                            jnp.where(incl, aa[:2 * C, 2 * C:], 0.0)], axis=1).astype(BF16)
                d["a_kk"] = jnp.where(strict, aa[2 * C:, :2 * C], 0.0).astype(BF16)
                n_kb = jnp.where(strict, aa[2 * C:, 2 * C:], 0.0)
                d["n_kb"] = n_kb.astype(BF16)
                d["t"] = (eye + jnp.where(first_level, n_kb, 0.0)).astype(BF16)

        def odd_rows(x, s):
            return jnp.concatenate([x[i:i + s] for i in range(s, 2 * C, 2 * s)], axis=0) if s >= 16 else x

        def level_a(off, s):
            for c in chains:
                pre[c]["tn"] = dot(odd_rows(pre[c]["t"], s), jnp.where(off, pre[c]["n_kb"], 0.0)).astype(BF16)

        def level_b(s):
            for c in chains:
                t = pre[c]["t"]
                new = odd_rows(t, s) + dot(pre[c]["tn"], t).astype(BF16)
                if s >= 16:
                    parts = []
                    for j, i in enumerate(range(0, 2 * C, 2 * s)):
                        parts += [t[i:i + s], new[j * s:(j + 1) * s]]
                    new = jnp.concatenate(parts, axis=0)
                pre[c]["t"] = new

        def akv():
            for c in chains:
                pre[c]["akv"] = dot(pre[c]["a_kk"], pre[c]["v2"]).astype(BF16)

        def solve():
            for c in chains:
                d = pre[c]
                tr = dot(d["t"], jnp.concatenate([d["kkh"], d["akv"]], axis=1)).astype(BF16)
                kq, u0 = tr[:, :LANES], tr[:, LANES:]
                d["big"] = jnp.concatenate([jnp.concatenate([d["v2"], zero_blk], axis=1),
                                            jnp.concatenate([u0, kq], axis=1)], axis=0)
        def readout():
            for c in chains:
                d = pre[c]
                yr = dot(d["a_r"], d["big"])
                d["y0"] = yr[:, :LANES]
                d["rq"] = (d["rt"] + yr[:, LANES:]).astype(BF16)
        def update():
            for c in chains:
                d = pre[c]
                dg = lax.dot_general(d["big"], d["ends"], (((0,), (0,)), ((), ())), preferred_element_type=F32)
                d["dd"] = dg[:LANES]
                d["gm"] = dg[LANES:].astype(BF16)
        stages = [scores]
        for i, off in enumerate(levels):
            stages += [functools.partial(level_a, off, 2 << i), functools.partial(level_b, 2 << i)]
        return stages + [akv, solve, readout, update]

    state = {(b, p): st_sc[b, p] for b in range(nb) for p in range(npairs)}

    def carried(chains):
        for (b, g, p) in chains:
            d = pre[(b, g, p)]
            st = state[(b, p)]
            st_b = st.astype(BF16)
            d["y2"] = lax.dot_general(d["rq"], st_b, (((1,), (1,)), ((), ())), preferred_element_type=F32) + d["y0"]
            state[(b, p)] = st * d["decay"] + dot(st_b, d["gm"]) + d["dd"]
        for (b, g, p) in chains:
            d = pre.pop((b, g, p))
            y2 = d["y2"]
            y = y2[:C] + y2[C:]
            sl = slice(p * LANES, (p + 1) * LANES)
            mean = _seg_sum(y, h0) * (1.0 / HEAD_DIM)
            yc = y - mean
            var = _seg_sum(yc * yc, h0) * (1.0 / HEAD_DIM)
            yn = yc * lax.rsqrt(var + LNX_EPS) * lng_ref[:, sl] + lnb_ref[:, sl]
            rows_out = min(C, n_valid - g * C)
            o_ref[b, g * C:g * C + rows_out, sl] = (yn + d["bonus"])[:rows_out].astype(o_ref.dtype)

    items = [(b, g) for g in range(groups) for b in range(nb)]
    for item in items:
        prep(*item)
    chains = [(b, g, p) for (b, g) in items for p in range(npairs)]
    for stage in transition_stages(chains):
        stage()
    carried(chains)
    for b in range(nb):
        prev_sc[b] = z_ref[b, pl.ds(min(R, n_valid) - 1, 1), :]
    for (b, p), st in state.items():
        st_sc[b, p] = st

    @pl.when(pl.program_id(1) == pl.num_programs(1) - 1)
    def _():
        st_ref[...] = st_sc[...]


def _rwkv(z, shift0, st0, mu, w0, a0, w_hi, w_lo, k_k, k_a, r_k, ln_g, ln_b, nb, groups, act):
    b, s, _ = z.shape
    rows = groups * RWKV_CHUNK
    if s % rows:
        assert groups == 1 and s < rows and s % 8 == 0
        rows = s
    n_valid = rows
    vec = lambda w: pl.BlockSpec((1, w), lambda i, j: (0, 0))
    st_spec = pl.BlockSpec((nb, 3, LANES, LANES), lambda i, j: (i, 0, 0, 0))
    wspec = pl.BlockSpec((LANES, 2 * RWKV_W), lambda i, j: (0, 0))
    return pl.pallas_call(
        functools.partial(_rwkv_kernel, nb=nb, groups=groups, n_valid=n_valid),
        grid=(b // nb, s // rows),
        in_specs=[pl.BlockSpec((nb, rows, SHIFT_W), lambda i, j: (i, j, 0)),
                  pl.BlockSpec((nb, 1, SHIFT_W), lambda i, j: (i, 0, 0)),
                  st_spec,
                  vec(SHIFT_W), vec(RWKV_W), vec(RWKV_W), wspec, wspec,
                  vec(RWKV_W), vec(RWKV_W), vec(RWKV_W), vec(RWKV_W), vec(RWKV_W)],
        out_specs=[pl.BlockSpec((nb, rows, RWKV_W), lambda i, j: (i, j, 0)), st_spec],
        out_shape=[jax.ShapeDtypeStruct((b, s, RWKV_W), act),
                   jax.ShapeDtypeStruct((b, 3, LANES, LANES), F32)],
        scratch_shapes=[pltpu.VMEM((nb, 3, LANES, LANES), F32), pltpu.VMEM((nb, 1, SHIFT_W), F32)],
        compiler_params=_cparams(("arbitrary", "arbitrary")),
        name="rwkv",
    )(z, shift0, st0, mu, w0, a0, w_hi, w_lo, k_k, k_a, r_k, ln_g, ln_b)


def _state_to_pairs(st):
    b = st.shape[0]
    st = st.reshape(b, 3, 2, HEAD_DIM, HEAD_DIM)
    zero = jnp.zeros_like(st[:, :, 0])
    top = jnp.concatenate([st[:, :, 0], zero], axis=-1)
    bot = jnp.concatenate([zero, st[:, :, 1]], axis=-1)
    return jnp.concatenate([top, bot], axis=-2)


def _pairs_to_state(sp):
    b = sp.shape[0]
    return jnp.stack([sp[:, :, :HEAD_DIM, :HEAD_DIM], sp[:, :, HEAD_DIM:, HEAD_DIM:]], axis=2).reshape(
        b, 6, HEAD_DIM, HEAD_DIM)


def _memkv_kernel(x_ref, g_ref, w_ref, k_ref, v_ref):
    x = x_ref[...]
    ms = jnp.mean(x * x, axis=-1, keepdims=True)
    h = ((x * lax.rsqrt(ms + NORM_EPS)) * g_ref[...]).astype(BF16)
    k_ref[...] = jnp.dot(h, w_ref[:, :MEM_W], preferred_element_type=F32)
    v_ref[...] = jnp.dot(h, w_ref[:, MEM_W:], preferred_element_type=F32)


def _memkv(mem2d, g, w_bf16):
    rows = mem2d.shape[0]
    out = jax.ShapeDtypeStruct((rows, MEM_W), F32)
    return pl.pallas_call(
        _memkv_kernel,
        out_shape=[out, out],
        compiler_params=pltpu.CompilerParams(vmem_limit_bytes=VMEM_LIMIT),
        name="memkv",
    )(mem2d, g, w_bf16)


def _memattn_kernel(q_ref, mk_ref, mv_ref, o_ref, *, nb, mem_minor):
    head_of_lane = lax.broadcasted_iota(jnp.int32, q_ref.shape[1:], 1) >> 6
    qs = [q_ref[b] for b in range(nb)]
    mks = [mk_ref[b].astype(BF16) for b in range(nb)]
    mvs = [mv_ref[b].astype(BF16) for b in range(nb)]
    outs = [jnp.zeros(q_ref.shape[1:], F32) for _ in range(nb)]
    for h in range(MEM_W // HEAD_DIM):
        mine = head_of_lane == h
        qk = _dot if mem_minor else _dot_nt
        pv = _dot_nt if mem_minor else _dot
        s = [qk(jnp.where(mine, qs[b], 0.0), mks[b]) for b in range(nb)]
        pexp = [jnp.exp2(s[b] - jnp.max(s[b], axis=-1, keepdims=True)) for b in range(nb)]
        l = [jnp.sum(pexp[b], axis=-1, keepdims=True) for b in range(nb)]
        outs = [jnp.where(mine, pv(pexp[b], mvs[b]) / l[b], outs[b]) for b in range(nb)]
    for b in range(nb):
        o_ref[b] = outs[b].astype(o_ref.dtype)


def _memattn(q, mk, mv, tm, nb, mem_minor, act):
    b, s, _ = q.shape
    qspec = pl.BlockSpec((nb, tm, MEM_W), lambda i, j: (i, j, 0))
    mspec = pl.BlockSpec((nb,) + mk.shape[1:], lambda i, j: (i, 0, 0))
    return pl.pallas_call(
        functools.partial(_memattn_kernel, nb=nb, mem_minor=mem_minor),
        grid=(b // nb, s // tm),
        in_specs=[qspec, mspec, mspec],
        out_specs=qspec,
        out_shape=jax.ShapeDtypeStruct(q.shape, act),
        compiler_params=_cparams(("arbitrary", "arbitrary")),
        name="memattn",
    )(q, mk, mv)


def _out_kernel(oa_ref, ob_ref, om_ref, gate_ref, x_ref, w_ref, g_ref, y_ref, *, split_pairs):
    acc = x_ref[...]

    def add(acc, o, c0):
        width = o.shape[1]
        gate = gate_ref[:, c0:c0 + width]
        return acc + _dot(o * (gate * jax.nn.sigmoid(gate)), w_ref[c0:c0 + width, :])

    if split_pairs:
        for p in range(ATT_W // LANES):
            acc = add(acc, oa_ref[p], p * LANES)
    else:
        acc = add(acc, oa_ref[...], 0)
    acc = add(acc, ob_ref[...], ATT_W)
    acc = add(acc, om_ref[...], ATT_W + RWKV_W)
    ms = jnp.mean(acc * acc, axis=-1, keepdims=True)
    y_ref[...] = (acc * lax.rsqrt(ms + NORM_EPS)) * g_ref[...]


def _out(oa, ob, om, gate, x2d, w_bf16, g, tm, split_pairs):
    rows = x2d.shape[0]
    row = lambda w: pl.BlockSpec((tm, w), lambda i: (i, 0))
    oa_spec = pl.BlockSpec((3, tm, LANES), lambda i: (0, i, 0)) if split_pairs else row(ATT_W)
    return pl.pallas_call(
        functools.partial(_out_kernel, split_pairs=split_pairs),
        grid=(rows // tm,),
        in_specs=[oa_spec, row(RWKV_W), row(MEM_W), row(MIX_W), row(D_MODEL),
                  pl.BlockSpec((MIX_W, D_MODEL), lambda i: (0, 0)),
                  pl.BlockSpec((1, D_MODEL), lambda i: (0, 0))],
        out_specs=row(D_MODEL),
        out_shape=jax.ShapeDtypeStruct((rows, D_MODEL), F32),
        compiler_params=_cparams(("arbitrary",)),
        name="outproj",
    )(oa, ob, om, gate, x2d, w_bf16, g)


def kernel(x_prompt, x_sample, cache_win_k, cache_win_v, state_rwkv, state_rwkv_shift, cache_mem_k, cache_mem_v, mem_prompt, norm_in, w_in, rwkv_mu, rwkv_w0, rwkv_w2, rwkv_a0, rwkv_a2, rwkv_k_k, rwkv_k_a, rwkv_r_k, rwkv_lnx_g, rwkv_lnx_b, norm_mem, w_mem_kv, w_out, norm_final):
    B, S, _ = x_prompt.shape
    DB, T, _ = x_sample.shape
    depth = w_in.shape[0]
    assert depth == 1 and S % ATT_TILE == 0 and cache_win_k.shape[2] == max(DILATIONS) * BAND
    l = 0
    past_len = S

    half = ROPE_DIM // 2
    lane = jnp.arange(LANES)
    inv_freq = ROPE_THETA ** (-(lane % half).astype(F32) / half)
    invf = jnp.where((lane % HEAD_DIM) < ROPE_DIM, inv_freq, 0.0).reshape(1, LANES)

    row = lambda t: t.reshape(1, -1)
    w_in_b = w_in[l].astype(BF16)
    w_out_b = w_out[l].astype(BF16)
    w_kv_b = w_mem_kv[l].astype(BF16)
    zero = jnp.zeros((LORA_W, RWKV_W), F32)
    w2a2 = jnp.concatenate([jnp.concatenate([rwkv_w2[l], zero], axis=1),
                            jnp.concatenate([zero, rwkv_a2[l]], axis=1)], axis=0)
    w2a2_hi = w2a2.astype(BF16)
    w2a2_lo = (w2a2 - w2a2_hi.astype(F32)).astype(BF16)
    rw = (row(rwkv_mu[l]), row(rwkv_w0[l]), row(rwkv_a0[l]), w2a2_hi, w2a2_lo, row(rwkv_k_k[l]),
          row(rwkv_k_a[l]), row(rwkv_r_k[l]), row(rwkv_lnx_g[l]), row(rwkv_lnx_b[l]))

    tiles = S // PROJ_TILE
    n_base = -(-(tiles + 1) // 8) * 8
    pos_base = jnp.where(jnp.arange(n_base) < tiles, jnp.arange(n_base) * PROJ_TILE, past_len)
    cr, sr, cb, sb = _rope_tables(jnp.arange(PROJ_TILE, dtype=F32).reshape(-1, 1),
                                  pos_base.astype(F32).reshape(-1, 1), invf)

    xp = x_prompt.reshape(B * S, D_MODEL)
    q3, k3, v3, zb, qm, gate = _proj(xp, cr, sr, cb, sb, row(norm_in[l]), w_in_b, PROJ_TILE, True, tiles, 0, BF16)
    oa3 = _attn_prompt(q3, k3, v3, B, S)
    ob, st_p = _rwkv(zb.reshape(B, S, SHIFT_W), jnp.zeros((B, 1, SHIFT_W), F32),
                     jnp.zeros((B, 3, LANES, LANES), F32), *rw, nb=B, groups=2, act=BF16)
    mk, mv = _memkv(mem_prompt.reshape(B * N_MEM, D_MODEL), row(norm_mem[l]), w_kv_b)
    mk = mk.reshape(B, N_MEM, MEM_W)
    mv = mv.reshape(B, N_MEM, MEM_W)
    om = _memattn(qm.reshape(B, S, MEM_W), mk, mv, 1024, 1, False, BF16)
    y_p = _out(oa3, ob.reshape(B * S, RWKV_W), om.reshape(B * S, MEM_W), gate, xp, w_out_b,
               row(norm_final), 1024, True)

    win = min(max(DILATIONS) * BAND, S)
    tail = lambda t3: jnp.transpose(t3.reshape(3, B, S, LANES)[:, :, S - win:], (1, 2, 0, 3)).reshape(
        1, B, win, ATT_W // HEAD_DIM, HEAD_DIM)
    heads = lambda t, n: t.reshape(1, t.shape[0], t.shape[1], n, HEAD_DIM)

    xs = x_sample.reshape(DB * T, D_MODEL)
    qs, ks, vs, zbs, qms, gates = _proj(xs, jnp.tile(cr[:T], (DB, 1)), jnp.tile(sr[:T], (DB, 1)), cb, sb,
                                        row(norm_in[l]), w_in_b, DB * T, False, 1, tiles, F32)
    n_past = cache_win_k.shape[2]
    minor = lambda c: jnp.transpose(c, (0, 2, 3, 1)).reshape(c.shape[0], c.shape[2] * c.shape[3], c.shape[1])
    oas = _attn_sample(qs.reshape(DB, T, ATT_W), ks.reshape(DB, T, ATT_W), vs.reshape(DB, T, ATT_W),
                       minor(cache_win_k[l]), minor(cache_win_v[l]), 2)
    zbs3 = zbs.reshape(DB, T, SHIFT_W)
    obs, st_s = _rwkv(zbs3, state_rwkv_shift[l].reshape(DB, 1, SHIFT_W), _state_to_pairs(state_rwkv[l]),
                      *rw, nb=4, groups=1, act=F32)
    oms = _memattn(qms.reshape(DB, T, MEM_W), minor(cache_mem_k[l]), minor(cache_mem_v[l]), T, 8, True, F32)
    y_s = _out(oas.reshape(DB * T, ATT_W), obs.reshape(DB * T, RWKV_W), oms.reshape(DB * T, MEM_W), gates, xs,
               w_out_b, row(norm_final), DB * T, False)

    return (y_p.reshape(B, S, D_MODEL), y_s.reshape(DB, T, D_MODEL),
            tail(k3), tail(v3),
            _pairs_to_state(st_p)[None], zb.reshape(B, S, SHIFT_W)[:, -1][None],
            heads(mk, MEM_W // HEAD_DIM), heads(mv, MEM_W // HEAD_DIM),
            heads(ks.reshape(DB, T, ATT_W), ATT_W // HEAD_DIM), heads(vs.reshape(DB, T, ATT_W), ATT_W // HEAD_DIM),
            _pairs_to_state(st_s)[None], zbs3[:, -1][None])
```

```python
import functools

import jax
import jax.numpy as jnp
from jax import lax
from jax.experimental import pallas as pl
from jax.experimental.pallas import tpu as pltpu

F32 = jnp.float32
BF16 = jnp.bfloat16

D_MODEL = 1024
HEAD_DIM = 64
ATT_W = 384
RWKV_W = 384
MEM_W = 256
MIX_W = 1024
LORA_W = 64
SHIFT_W = 3 * RWKV_W + 2 * LORA_W
N_MEM = 256
ROPE_DIM = 16
ROPE_THETA = 500000.0
NORM_EPS = 1e-6
LNX_EPS = 64e-5
DILATIONS = (1, 4, 16)
BAND = 128
ATT_TILE = BAND * max(DILATIONS)
REGROUP = 4
LANES = 128
RWKV_CHUNK = 64
PROJ_TILE = 512
NEG = -1e30
Q_SCALE = HEAD_DIM ** -0.5 * 1.4426950408889634
VMEM_LIMIT = 56 * 1024 * 1024


def _cparams(sem):
    return pltpu.CompilerParams(dimension_semantics=sem, vmem_limit_bytes=VMEM_LIMIT)


def _dot(a, b):
    return jnp.dot(a.astype(BF16), b.astype(BF16), preferred_element_type=F32)


def _dot_nt(a, b):
    return lax.dot_general(a.astype(BF16), b.astype(BF16), (((1,), (1,)), ((), ())),
                           preferred_element_type=F32)


def _dot_tn(a, b):
    return lax.dot_general(a.astype(BF16), b.astype(BF16), (((0,), (0,)), ((), ())),
                           preferred_element_type=F32)


def _dot_f32(a, b):
    return jnp.dot(a, b, preferred_element_type=F32, precision=lax.Precision.HIGHEST)


def _half0(shape):
    return (lax.broadcasted_iota(jnp.int32, shape, len(shape) - 1) & 64) == 0


def _rope_table_kernel(pos_row_ref, pos_base_ref, invf_ref, cr_ref, sr_ref, cb_ref, sb_ref):
    ang_r = pos_row_ref[...] * invf_ref[...]
    cr_ref[...] = jnp.cos(ang_r)
    sr_ref[...] = jnp.sin(ang_r)
    ang_b = pos_base_ref[...] * invf_ref[...]
    cb_ref[...] = jnp.cos(ang_b)
    sb_ref[...] = jnp.sin(ang_b)


def _rope_tables(pos_row, pos_base, invf):
    tab = lambda n: jax.ShapeDtypeStruct((n, LANES), F32)
    return pl.pallas_call(
        _rope_table_kernel,
        out_shape=[tab(pos_row.shape[0]), tab(pos_row.shape[0]), tab(pos_base.shape[0]), tab(pos_base.shape[0])],
        name="rope_tables",
    )(pos_row, pos_base, invf)


def _proj_kernel(x_ref, cr_ref, sr_ref, cb_ref, sb_ref, g_ref, w_ref, q_ref, k_ref, v_ref, zb_ref, qm_ref, gate_ref,
                 *, split_pairs, tiles_per_seq, base_offset):
    tm = x_ref.shape[0]
    base = base_offset + lax.rem(pl.program_id(0), tiles_per_seq)
    cb = cb_ref[pl.ds(base, 1), :]
    sb = sb_ref[pl.ds(base, 1), :]
    second = (lax.broadcasted_iota(jnp.int32, (1, LANES), 1) & 8) != 0

    def put(ref, p, rows, val):
        if split_pairs:
            ref[p, rows, :] = val
        else:
            ref[rows, p * LANES:(p + 1) * LANES] = val

    nsplit = 2 if tm % 16 == 0 else 1
    hm = tm // nsplit
    for part in range(nsplit):
        rows = pl.ds(part * hm, hm)
        x = x_ref[rows, :]
        ms = jnp.mean(x * x, axis=-1, keepdims=True)
        h = ((x * lax.rsqrt(ms + NORM_EPS)) * g_ref[...]).astype(BF16)
        cr = cr_ref[rows, :]
        sr = sr_ref[rows, :]
        cos = cb * cr - sb * sr
        sin = sb * cr + cb * sr
        sin_up = jnp.where(second, sin, 0.0)
        sin_dn = jnp.where(second, 0.0, -sin)

        def rope(t):
            return t * cos + pltpu.roll(t, 8, 1) * sin_up + pltpu.roll(t, LANES - 8, 1) * sin_dn

        qkv = jnp.dot(h, w_ref[:, :3 * ATT_W], preferred_element_type=F32)
        for p in range(ATT_W // LANES):
            put(q_ref, p, rows, rope(qkv[:, p * LANES:(p + 1) * LANES]) * Q_SCALE)
            put(k_ref, p, rows, rope(qkv[:, ATT_W + p * LANES:ATT_W + (p + 1) * LANES]))
            put(v_ref, p, rows, qkv[:, 2 * ATT_W + p * LANES:2 * ATT_W + (p + 1) * LANES])
        c0 = 3 * ATT_W
        zb_ref[rows, :] = jnp.dot(h, w_ref[:, c0:c0 + SHIFT_W], preferred_element_type=F32)
        c0 += SHIFT_W
        qm_ref[rows, :] = (jnp.dot(h, w_ref[:, c0:c0 + MEM_W], preferred_element_type=F32) * Q_SCALE).astype(
            qm_ref.dtype)
        c0 += MEM_W
        gate_ref[rows, :] = jnp.dot(h, w_ref[:, c0:c0 + MIX_W], preferred_element_type=F32).astype(gate_ref.dtype)


def _proj(x2d, cr, sr, cb, sb, g, w_bf16, tm, split_pairs, tiles_per_seq, base_offset, act):
    rows = x2d.shape[0]
    in_w = w_bf16.shape[1]
    if split_pairs:
        qkv_shape = jax.ShapeDtypeStruct((3, rows, LANES), F32)
        qkv_spec = pl.BlockSpec((3, tm, LANES), lambda i: (0, i, 0))
    else:
        qkv_shape = jax.ShapeDtypeStruct((rows, ATT_W), F32)
        qkv_spec = pl.BlockSpec((tm, ATT_W), lambda i: (i, 0))
    row = lambda w: pl.BlockSpec((tm, w), lambda i: (i, 0))
    full = lambda a: pl.BlockSpec(a.shape, lambda i: (0, 0))
    return pl.pallas_call(
        functools.partial(_proj_kernel, split_pairs=split_pairs, tiles_per_seq=tiles_per_seq,
                          base_offset=base_offset),
        grid=(rows // tm,),
        in_specs=[row(D_MODEL), full(cr), full(sr), full(cb), full(sb), full(g), full(w_bf16)],
        out_specs=[qkv_spec, qkv_spec, qkv_spec, row(SHIFT_W), row(MEM_W), row(MIX_W)],
        out_shape=[qkv_shape, qkv_shape, qkv_shape,
                   jax.ShapeDtypeStruct((rows, SHIFT_W), F32),
                   jax.ShapeDtypeStruct((rows, MEM_W), act),
                   jax.ShapeDtypeStruct((rows, MIX_W), act)],
        compiler_params=_cparams(("arbitrary",)),
        name="proj",
    )(x2d, cr, sr, cb, sb, g, w_bf16)


def _attn_blocks(npairs, blocks):
    h0 = _half0((BAND, LANES))
    h0k = _half0((2 * BAND, LANES))
    units = [(b, p) for b in range(len(blocks)) for p in range(npairs)]
    heads = [(i, hh) for i in range(len(units)) for hh in range(2)]
    q = [blocks[b][0](p) for b, p in units]
    kcat = [jnp.concatenate([blocks[b][1](p), blocks[b][2](p)], axis=0).astype(BF16) for b, p in units]
    vcat = [jnp.concatenate([blocks[b][3](p), blocks[b][4](p)], axis=0) for b, p in units]
    s = [_dot_nt(jnp.where(h0 if hh == 0 else jnp.logical_not(h0), q[i], 0.0), kcat[i]) + blocks[units[i][0]][5]
         for i, hh in heads]
    m = [jnp.max(s[j], axis=-1, keepdims=True) for j in range(len(heads))]
    pexp = [jnp.exp2(s[j] - m[j]).astype(BF16) for j in range(len(heads))]
    res = [jnp.dot(pexp[j], jnp.where(h0k if hh == 0 else jnp.logical_not(h0k), vcat[i], 1.0).astype(BF16),
                   preferred_element_type=F32) for j, (i, hh) in enumerate(heads)]
    lsum = [pltpu.roll(jnp.where(h0, res[2 * i + 1], res[2 * i]), HEAD_DIM, 1) for i in range(len(units))]
    outs = [[] for _ in blocks]
    for i, (b, p) in enumerate(units):
        outs[b].append((jnp.where(h0, res[2 * i], res[2 * i + 1]) / lsum[i],
                        jnp.where(h0, m[2 * i], m[2 * i + 1]) + jnp.log2(lsum[i])))
    return outs


def _attn_kernel(q_ref, kc_ref, kp_ref, vc_ref, vp_ref, o_ref, o_sc, lse_sc, yq_sc, ykv_sc):
    step = pl.program_id(1)
    has_prev = step > 0
    slot = step & 1
    prev_slot = 1 - slot
    qi = lax.broadcasted_iota(jnp.int32, (BAND, 2 * BAND), 0)
    kj = lax.broadcasted_iota(jnp.int32, (BAND, 2 * BAND), 1)
    in_cur = (kj >= BAND) & (kj - BAND <= qi)
    in_prev = (kj < BAND) & (kj >= qi)
    bias_full = jnp.where(in_cur | in_prev, 0.0, NEG).astype(F32)
    bias_first = jnp.where(in_cur | (in_prev & has_prev), 0.0, NEG).astype(F32)
    npairs = q_ref.shape[0]
    nblk = ATT_TILE // BAND
    per_iter = 2
    seg = ATT_TILE // REGROUP

    @pl.when(step == 0)
    def _():
        ykv_sc[prev_slot] = jnp.zeros(ykv_sc.shape[1:], F32)

    def regroup(c, carry):
        per_res = seg // BAND
        res, part = c >> (per_res.bit_length() - 1), c & (per_res - 1)
        src = pl.ds(res + part * (BAND * REGROUP), BAND, stride=REGROUP)
        dst = pl.ds(pl.multiple_of(c * BAND, BAND), BAND)
        for p in range(npairs):
            yq_sc[p, dst, :] = q_ref[p, src, :]
            ykv_sc[slot, 0, p, dst, :] = kc_ref[p, src, :]
            ykv_sc[slot, 1, p, dst, :] = vc_ref[p, src, :]
        return carry

    lax.fori_loop(0, nblk, regroup, 0)

    def desc(d, u, first):
        span = BAND * d
        bias = bias_first if first else bias_full
        if d == 1:
            qs = u * BAND if isinstance(u, int) else pl.multiple_of(u * BAND, BAND)
            rows = pl.ds(qs, BAND)
            prev = (lambda ref_c, ref_p: (lambda p: ref_p[p])) if first else (
                lambda ref_c, ref_p: (lambda p: ref_c[p, pl.ds(qs - BAND, BAND), :]))
            return (lambda p: q_ref[p, rows, :], prev(kc_ref, kp_ref), lambda p: kc_ref[p, rows, :],
                    prev(vc_ref, vp_ref), lambda p: vc_ref[p, rows, :], bias), qs
        assert d % REGROUP == 0
        sub = d // REGROUP
        sp, r = u >> (d.bit_length() - 1), u & (d - 1)
        qs = sp * span + r
        start = (r & (REGROUP - 1)) * seg + sp * (span // REGROUP) + (r >> (REGROUP.bit_length() - 1))
        rows = pl.ds(start, BAND, stride=sub)
        if first:
            prow, pslot = pl.ds(start + seg - span // REGROUP, BAND, stride=sub), prev_slot
        else:
            prow, pslot = pl.ds(start - span // REGROUP, BAND, stride=sub), slot
        return (lambda p: yq_sc[p, rows, :],
                lambda p: ykv_sc[pslot, 0, p, prow, :], lambda p: ykv_sc[slot, 0, p, rows, :],
                lambda p: ykv_sc[pslot, 1, p, prow, :], lambda p: ykv_sc[slot, 1, p, rows, :], bias), qs

    def merge(outs, start):
        row = pl.ds(start, BAND)
        for p, (o1, lse1) in enumerate(outs):
            os_ = [o1] + [o_sc[j, p, row, :] for j in range(len(DILATIONS) - 1)]
            ls_ = [lse1] + [lse_sc[j, p, row, :] for j in range(len(DILATIONS) - 1)]
            top = functools.reduce(jnp.maximum, ls_)
            ws = [jnp.exp2(l - top) for l in ls_]
            num = functools.reduce(lambda a, b: a + b, [w * o for w, o in zip(ws, os_)])
            o_ref[p, row, :] = (num / functools.reduce(lambda a, b: a + b, ws)).astype(o_ref.dtype)

    def run(j, d, descs):
        for (_, qs), outs in zip(descs, _attn_blocks(npairs, [blk for blk, _ in descs])):
            if j < 0:
                merge(outs, qs)
            else:
                for p, (o, lse) in enumerate(outs):
                    o_sc[j, p, pl.ds(qs, BAND, stride=d), :] = o
                    lse_sc[j, p, pl.ds(qs, BAND, stride=d), :] = lse

    def sweep(j, d):
        n_first = d
        several = lambda first: lambda i, c: (run(j, d, [desc(d, i * per_iter + k, first) for k in range(per_iter)]),
                                              c)[1]
        if n_first % per_iter == 0:
            lax.fori_loop(0, n_first // per_iter, several(True), 0)
            lo = n_first
        else:
            assert n_first == 1 and per_iter == 2
            run(j, d, [desc(d, 0, True), desc(d, 1, False)])
            lo = per_iter
        if lo < nblk:
            lax.fori_loop(lo // per_iter, nblk // per_iter, several(False), 0)

    for j, d in enumerate(DILATIONS[1:]):
        sweep(j, d)
    sweep(-1, DILATIONS[0])


def _attn_prompt(q3, k3, v3, batch, seq):
    nt = seq // ATT_TILE
    per_tile = ATT_TILE // BAND
    cur = pl.BlockSpec((3, ATT_TILE, LANES), lambda b, i: (0, b * nt + i, 0))
    prev = pl.BlockSpec((3, BAND, LANES), lambda b, i: (0, jnp.maximum((b * nt + i) * per_tile - 1, 0), 0))
    return pl.pallas_call(
        _attn_kernel,
        grid=(batch, nt),
        in_specs=[cur, cur, prev, cur, prev],
        out_specs=cur,
        out_shape=jax.ShapeDtypeStruct(q3.shape, BF16),
        scratch_shapes=[pltpu.VMEM((len(DILATIONS) - 1, 3, ATT_TILE, LANES), F32),
                        pltpu.VMEM((len(DILATIONS) - 1, 3, ATT_TILE, LANES), F32),
                        pltpu.VMEM((3, ATT_TILE, LANES), F32),
                        pltpu.VMEM((2, 2, 3, ATT_TILE, LANES), F32)],
        compiler_params=_cparams(("arbitrary", "arbitrary")),
        name="attn_prompt",
    )(q3, k3, k3, v3, v3)


def _attn_sample_kernel(q_ref, kn_ref, vn_ref, kc_ref, vc_ref, o_ref, *, n_past, t_new, nb):
    nh = ATT_W // HEAD_DIM
    head_of_lane = lax.broadcasted_iota(jnp.int32, (t_new, ATT_W), 1) >> 6
    seqs = range(nb)
    qst = [jnp.concatenate([jnp.where(head_of_lane == h, q_ref[b], 0.0) for h in range(nh)], axis=0) for b in seqs]

    def count(delta):
        c = jnp.zeros(delta.shape, F32)
        for d in DILATIONS:
            ok = (delta >= 0) & (delta <= BAND * d) & ((delta & (d - 1)) == 0)
            c = c + jnp.where(ok, 1.0, 0.0)
        return c

    rows = nh * t_new
    t_past = lax.broadcasted_iota(jnp.int32, (rows, n_past), 0) & (t_new - 1)
    cnt_past = count(n_past + t_past - lax.broadcasted_iota(jnp.int32, (rows, n_past), 1))
    t_n = lax.broadcasted_iota(jnp.int32, (rows, t_new), 0) & (t_new - 1)
    cnt_new = count(t_n - lax.broadcasted_iota(jnp.int32, (rows, t_new), 1))

    s_past = [jnp.where(cnt_past > 0, _dot(qst[b], kc_ref[b]), NEG) for b in seqs]
    s_new = [jnp.where(cnt_new > 0, _dot_nt(qst[b], kn_ref[b]), NEG) for b in seqs]
    m = [jnp.maximum(jnp.max(s_past[b], axis=-1, keepdims=True), jnp.max(s_new[b], axis=-1, keepdims=True))
         for b in seqs]
    p_past = [cnt_past * jnp.exp2(s_past[b] - m[b]) for b in seqs]
    p_new = [cnt_new * jnp.exp2(s_new[b] - m[b]) for b in seqs]
    l = [jnp.sum(p_past[b], axis=-1, keepdims=True) + jnp.sum(p_new[b], axis=-1, keepdims=True) for b in seqs]
    o = [(_dot_nt(p_past[b], vc_ref[b]) + _dot(p_new[b], vn_ref[b])) / l[b] for b in seqs]
    for b in seqs:
        out = jnp.zeros((t_new, ATT_W), F32)
        for h in range(nh):
            out = jnp.where(head_of_lane == h, o[b][h * t_new:(h + 1) * t_new, :], out)
        o_ref[b] = out.astype(o_ref.dtype)


def _attn_sample(q, k_new, v_new, k_cache, v_cache, nb):
    db, t_new, _ = q.shape
    n_past = k_cache.shape[2]
    new = pl.BlockSpec((nb, t_new, ATT_W), lambda b: (b, 0, 0))
    cache = pl.BlockSpec((nb, ATT_W, n_past), lambda b: (b, 0, 0))
    return pl.pallas_call(
        functools.partial(_attn_sample_kernel, n_past=n_past, t_new=t_new, nb=nb),
        grid=(db // nb,),
        in_specs=[new, new, new, cache, cache],
        out_specs=new,
        out_shape=jax.ShapeDtypeStruct(q.shape, F32),
        compiler_params=_cparams(("arbitrary",)),
        name="attn_sample",
    )(q, k_new, v_new, k_cache, v_cache)


def _seg_sum(x, h0):
    s0 = jnp.sum(jnp.where(h0, x, 0.0), axis=-1, keepdims=True)
    s1 = jnp.sum(jnp.where(h0, 0.0, x), axis=-1, keepdims=True)
    return jnp.where(h0, s0, s1)


def _split_bf16(x):
    hi = x.astype(BF16)
    return hi, (x - hi.astype(F32)).astype(BF16)


def _rwkv_kernel(z_ref, sh0_ref, st0_ref, mu_ref, w0_ref, a0_ref, whi_ref, wlo_ref, kk_ref, ka_ref, rk_ref,
                 lng_ref, lnb_ref, o_ref, st_ref, st_sc, prev_sc, *, nb, groups, n_valid):
    C = RWKV_CHUNK
    R = groups * C
    npairs = RWKV_W // LANES

    @pl.when(pl.program_id(1) == 0)
    def _():
        st_sc[...] = st0_ref[...]
        prev_sc[...] = sh0_ref[...]

    si = lax.broadcasted_iota(jnp.int32, (2 * C, 2 * C), 0)
    sj = lax.broadcasted_iota(jnp.int32, (2 * C, 2 * C), 1)
    same_head = (si >= C) == (sj >= C)
    ti2 = si & (C - 1)
    tj2 = sj & (C - 1)
    incl = same_head & (tj2 <= ti2)
    strict = same_head & (tj2 < ti2)
    eye = jnp.where(si == sj, 1.0, 0.0).astype(F32)
    levels = []
    s = 2
    while s < min(C, n_valid):
        sh = s.bit_length() - 1
        levels.append(((ti2 >> (sh + 1)) == (tj2 >> (sh + 1))) & (((ti2 >> sh) & 1) == 1) & (((tj2 >> sh) & 1) == 0))
        s *= 2
    first_level = (ti2 >> 1) == (tj2 >> 1)

    ri = lax.broadcasted_iota(jnp.int32, (C, C), 0)
    rj = lax.broadcasted_iota(jnp.int32, (C, C), 1)
    tri = jnp.where(rj <= ri, 1.0, 0.0).astype(BF16)
    rowid = lax.broadcasted_iota(jnp.int32, (C, 1), 0)
    h0 = _half0((C, LANES))
    zero_blk = jnp.zeros((2 * C, LANES), BF16)
    dot = functools.partial(jnp.dot, preferred_element_type=F32)
    pre = {}

    def stack(x):
        return jnp.concatenate([jnp.where(h0, x, 0.0), jnp.where(h0, 0.0, x)], axis=0)

    def prep(b, g):
        if n_valid < C:
            z = jnp.concatenate([z_ref[b], jnp.zeros((C - n_valid, SHIFT_W), F32)], axis=0)
        else:
            z = z_ref[b, g * C:(g + 1) * C, :]
        before = prev_sc[b] if g == 0 else z_ref[b, pl.ds(g * C - 1, 1), :]
        z_prev = jnp.where(rowid == 0, before, pltpu.roll(z, 1, 0))
        zs = z + (z_prev - z) * mu_ref[...]
        valid = rowid < (n_valid - g * C)
        if n_valid < (g + 1) * C:
            zs = jnp.where(valid, zs, 0.0)
        lat = zs[:, 3 * RWKV_W:]
        lat_hi, lat_lo = _split_bf16(jnp.where(_half0(lat.shape), jnp.tanh(lat), lat))
        lora = dot(lat_hi, whi_ref[...]) + dot(lat_lo, whi_ref[...]) + dot(lat_hi, wlo_ref[...])
        w = -jax.nn.softplus(-(w0_ref[...] + lora[:, :RWKV_W])) - 0.5
        lw = -jnp.exp(w)
        if n_valid < (g + 1) * C:
            lw = jnp.where(valid, lw, 0.0)
        a = jax.nn.sigmoid(a0_ref[...] + lora[:, RWKV_W:])
        lw_hi, lw_lo = _split_bf16(lw)
        cum = dot(tri, lw_hi) + dot(tri, lw_lo)
        for p in range(npairs):
            sl = slice(p * LANES, (p + 1) * LANES)
            r = zs[:, sl]
            k = zs[:, RWKV_W + p * LANES:RWKV_W + (p + 1) * LANES]
            v = zs[:, 2 * RWKV_W + p * LANES:2 * RWKV_W + (p + 1) * LANES]
            ap = a[:, sl]
            kk = k * kk_ref[:, sl]
            kk = kk * jnp.minimum(lax.rsqrt(_seg_sum(kk * kk, h0)), 1e12)
            k = k * (1.0 + (ap - 1.0) * ka_ref[:, sl])
            bb = kk * ap
            L = cum[:, sl]
            l_end = L[C - 1:C, :]
            e_out = jnp.exp(-L)
            e_end = jnp.exp(l_end - L)
            rt = stack(r * jnp.exp(L))
            kkh = stack(kk * jnp.exp(L - lw[:, sl])).astype(BF16)
            left = jnp.concatenate([rt.astype(BF16), kkh], axis=0)
            right = jnp.concatenate([stack(k * e_out), stack(-(bb * e_out))], axis=0).astype(BF16)
            ends = jnp.concatenate([stack(k * e_end), stack(-(bb * e_end))], axis=0).astype(BF16)
            pre[(b, g, p)] = dict(left=left, right=right, ends=ends, rt=rt, kkh=kkh, v2=stack(v).astype(BF16),
                                  decay=jnp.exp(l_end), bonus=_seg_sum(r * k * rk_ref[:, sl], h0) * v)

    def transition_stages(chains):
        def scores():
            for c in chains:
                d = pre[c]
                aa = lax.dot_general(d["left"], d["right"], (((1,), (1,)), ((), ())), preferred_element_type=F32)
                d["a_r"] = jnp.concatenate([jnp.where(incl, aa[:2 * C, :2 * C], 0.0),
                                            jnp.where(incl, aa[:2 * C, 2 * C:], 0.0)], axis=1).astype(BF16)
                d["a_kk"] = jnp.where(strict, aa[2 * C:, :2 * C], 0.0).astype(BF16)
                n_kb = jnp.where(strict, aa[2 * C:, 2 * C:], 0.0)
                d["n_kb"] = n_kb.astype(BF16)
                d["t"] = (eye + jnp.where(first_level, n_kb, 0.0)).astype(BF16)

        def odd_rows(x, s):
            return jnp.concatenate([x[i:i + s] for i in range(s, 2 * C, 2 * s)], axis=0) if s >= 16 else x

        def level_a(off, s):
            for c in chains:
                pre[c]["tn"] = dot(odd_rows(pre[c]["t"], s), jnp.where(off, pre[c]["n_kb"], 0.0)).astype(BF16)

        def level_b(s):
            for c in chains:
                t = pre[c]["t"]
                new = odd_rows(t, s) + dot(pre[c]["tn"], t).astype(BF16)
                if s >= 16:
                    parts = []
                    for j, i in enumerate(range(0, 2 * C, 2 * s)):
                        parts += [t[i:i + s], new[j * s:(j + 1) * s]]
                    new = jnp.concatenate(parts, axis=0)
                pre[c]["t"] = new

        def akv():
            for c in chains:
                pre[c]["akv"] = dot(pre[c]["a_kk"], pre[c]["v2"]).astype(BF16)

        def solve():
            for c in chains:
                d = pre[c]
                tr = dot(d["t"], jnp.concatenate([d["kkh"], d["akv"]], axis=1)).astype(BF16)
                kq, u0 = tr[:, :LANES], tr[:, LANES:]
                d["big"] = jnp.concatenate([jnp.concatenate([d["v2"], zero_blk], axis=1),
                                            jnp.concatenate([u0, kq], axis=1)], axis=0)
        def readout():
            for c in chains:
                d = pre[c]
                yr = dot(d["a_r"], d["big"])
                d["y0"] = yr[:, :LANES]
                d["rq"] = (d["rt"] + yr[:, LANES:]).astype(BF16)
        def update():
            for c in chains:
                d = pre[c]
                dg = lax.dot_general(d["big"], d["ends"], (((0,), (0,)), ((), ())), preferred_element_type=F32)
                d["dd"] = dg[:LANES]
                d["gm"] = dg[LANES:].astype(BF16)
        stages = [scores]
        for i, off in enumerate(levels):
            stages += [functools.partial(level_a, off, 2 << i), functools.partial(level_b, 2 << i)]
        return stages + [akv, solve, readout, update]

    state = {(b, p): st_sc[b, p] for b in range(nb) for p in range(npairs)}

    def carried(chains):
        for (b, g, p) in chains:
            d = pre[(b, g, p)]
            st = state[(b, p)]
            st_b = st.astype(BF16)
            d["y2"] = lax.dot_general(d["rq"], st_b, (((1,), (1,)), ((), ())), preferred_element_type=F32) + d["y0"]
            state[(b, p)] = st * d["decay"] + dot(st_b, d["gm"]) + d["dd"]
        for (b, g, p) in chains:
            d = pre.pop((b, g, p))
            y2 = d["y2"]
            y = y2[:C] + y2[C:]
            sl = slice(p * LANES, (p + 1) * LANES)
            mean = _seg_sum(y, h0) * (1.0 / HEAD_DIM)
            yc = y - mean
            var = _seg_sum(yc * yc, h0) * (1.0 / HEAD_DIM)
            yn = yc * lax.rsqrt(var + LNX_EPS) * lng_ref[:, sl] + lnb_ref[:, sl]
            rows_out = min(C, n_valid - g * C)
            o_ref[b, g * C:g * C + rows_out, sl] = (yn + d["bonus"])[:rows_out].astype(o_ref.dtype)

    items = [(b, g) for g in range(groups) for b in range(nb)]
    for item in items:
        prep(*item)
    chains = [(b, g, p) for (b, g) in items for p in range(npairs)]
    for stage in transition_stages(chains):
        stage()
    carried(chains)
    for b in range(nb):
        prev_sc[b] = z_ref[b, pl.ds(min(R, n_valid) - 1, 1), :]
    for (b, p), st in state.items():
        st_sc[b, p] = st

    @pl.when(pl.program_id(1) == pl.num_programs(1) - 1)
    def _():
        st_ref[...] = st_sc[...]


def _rwkv(z, shift0, st0, mu, w0, a0, w_hi, w_lo, k_k, k_a, r_k, ln_g, ln_b, nb, groups, act):
    b, s, _ = z.shape
    rows = groups * RWKV_CHUNK
    if s % rows:
        assert groups == 1 and s < rows and s % 8 == 0
        rows = s
    n_valid = rows
    vec = lambda w: pl.BlockSpec((1, w), lambda i, j: (0, 0))
    st_spec = pl.BlockSpec((nb, 3, LANES, LANES), lambda i, j: (i, 0, 0, 0))
    wspec = pl.BlockSpec((LANES, 2 * RWKV_W), lambda i, j: (0, 0))
    return pl.pallas_call(
        functools.partial(_rwkv_kernel, nb=nb, groups=groups, n_valid=n_valid),
        grid=(b // nb, s // rows),
        in_specs=[pl.BlockSpec((nb, rows, SHIFT_W), lambda i, j: (i, j, 0)),
                  pl.BlockSpec((nb, 1, SHIFT_W), lambda i, j: (i, 0, 0)),
                  st_spec,
                  vec(SHIFT_W), vec(RWKV_W), vec(RWKV_W), wspec, wspec,
                  vec(RWKV_W), vec(RWKV_W), vec(RWKV_W), vec(RWKV_W), vec(RWKV_W)],
        out_specs=[pl.BlockSpec((nb, rows, RWKV_W), lambda i, j: (i, j, 0)), st_spec],
        out_shape=[jax.ShapeDtypeStruct((b, s, RWKV_W), act),
                   jax.ShapeDtypeStruct((b, 3, LANES, LANES), F32)],
        scratch_shapes=[pltpu.VMEM((nb, 3, LANES, LANES), F32), pltpu.VMEM((nb, 1, SHIFT_W), F32)],
        compiler_params=_cparams(("arbitrary", "arbitrary")),
        name="rwkv",
    )(z, shift0, st0, mu, w0, a0, w_hi, w_lo, k_k, k_a, r_k, ln_g, ln_b)


def _state_to_pairs(st):
    b = st.shape[0]
    st = st.reshape(b, 3, 2, HEAD_DIM, HEAD_DIM)
    zero = jnp.zeros_like(st[:, :, 0])
    top = jnp.concatenate([st[:, :, 0], zero], axis=-1)
    bot = jnp.concatenate([zero, st[:, :, 1]], axis=-1)
    return jnp.concatenate([top, bot], axis=-2)


def _pairs_to_state(sp):
    b = sp.shape[0]
    return jnp.stack([sp[:, :, :HEAD_DIM, :HEAD_DIM], sp[:, :, HEAD_DIM:, HEAD_DIM:]], axis=2).reshape(
        b, 6, HEAD_DIM, HEAD_DIM)


def _memkv_kernel(x_ref, g_ref, w_ref, k_ref, v_ref):
    x = x_ref[...]
    ms = jnp.mean(x * x, axis=-1, keepdims=True)
    h = ((x * lax.rsqrt(ms + NORM_EPS)) * g_ref[...]).astype(BF16)
    k_ref[...] = jnp.dot(h, w_ref[:, :MEM_W], preferred_element_type=F32)
    v_ref[...] = jnp.dot(h, w_ref[:, MEM_W:], preferred_element_type=F32)


def _memkv(mem2d, g, w_bf16):
    rows = mem2d.shape[0]
    out = jax.ShapeDtypeStruct((rows, MEM_W), F32)
    return pl.pallas_call(
        _memkv_kernel,
        out_shape=[out, out],
        compiler_params=pltpu.CompilerParams(vmem_limit_bytes=VMEM_LIMIT),
        name="memkv",
    )(mem2d, g, w_bf16)


def _memattn_kernel(q_ref, mk_ref, mv_ref, o_ref, *, nb, mem_minor):
    head_of_lane = lax.broadcasted_iota(jnp.int32, q_ref.shape[1:], 1) >> 6
    qs = [q_ref[b] for b in range(nb)]
    mks = [mk_ref[b].astype(BF16) for b in range(nb)]
    mvs = [mv_ref[b].astype(BF16) for b in range(nb)]
    outs = [jnp.zeros(q_ref.shape[1:], F32) for _ in range(nb)]
    for h in range(MEM_W // HEAD_DIM):
        mine = head_of_lane == h
        qk = _dot if mem_minor else _dot_nt
        pv = _dot_nt if mem_minor else _dot
        s = [qk(jnp.where(mine, qs[b], 0.0), mks[b]) for b in range(nb)]
        pexp = [jnp.exp2(s[b] - jnp.max(s[b], axis=-1, keepdims=True)) for b in range(nb)]
        l = [jnp.sum(pexp[b], axis=-1, keepdims=True) for b in range(nb)]
        outs = [jnp.where(mine, pv(pexp[b], mvs[b]) / l[b], outs[b]) for b in range(nb)]
    for b in range(nb):
        o_ref[b] = outs[b].astype(o_ref.dtype)


def _memattn(q, mk, mv, tm, nb, mem_minor, act):
    b, s, _ = q.shape
    qspec = pl.BlockSpec((nb, tm, MEM_W), lambda i, j: (i, j, 0))
    mspec = pl.BlockSpec((nb,) + mk.shape[1:], lambda i, j: (i, 0, 0))
    return pl.pallas_call(
        functools.partial(_memattn_kernel, nb=nb, mem_minor=mem_minor),
        grid=(b // nb, s // tm),
        in_specs=[qspec, mspec, mspec],
        out_specs=qspec,
        out_shape=jax.ShapeDtypeStruct(q.shape, act),
        compiler_params=_cparams(("arbitrary", "arbitrary")),
        name="memattn",
    )(q, mk, mv)


def _out_kernel(oa_ref, ob_ref, om_ref, gate_ref, x_ref, w_ref, g_ref, y_ref, *, split_pairs):
    acc = x_ref[...]

    def add(acc, o, c0):
        width = o.shape[1]
        gate = gate_ref[:, c0:c0 + width]
        return acc + _dot(o * (gate * jax.nn.sigmoid(gate)), w_ref[c0:c0 + width, :])

    if split_pairs:
        for p in range(ATT_W // LANES):
            acc = add(acc, oa_ref[p], p * LANES)
    else:
        acc = add(acc, oa_ref[...], 0)
    acc = add(acc, ob_ref[...], ATT_W)
    acc = add(acc, om_ref[...], ATT_W + RWKV_W)
    ms = jnp.mean(acc * acc, axis=-1, keepdims=True)
    y_ref[...] = (acc * lax.rsqrt(ms + NORM_EPS)) * g_ref[...]


def _out(oa, ob, om, gate, x2d, w_bf16, g, tm, split_pairs):
    rows = x2d.shape[0]
    row = lambda w: pl.BlockSpec((tm, w), lambda i: (i, 0))
    oa_spec = pl.BlockSpec((3, tm, LANES), lambda i: (0, i, 0)) if split_pairs else row(ATT_W)
    return pl.pallas_call(
        functools.partial(_out_kernel, split_pairs=split_pairs),
        grid=(rows // tm,),
        in_specs=[oa_spec, row(RWKV_W), row(MEM_W), row(MIX_W), row(D_MODEL),
                  pl.BlockSpec((MIX_W, D_MODEL), lambda i: (0, 0)),
                  pl.BlockSpec((1, D_MODEL), lambda i: (0, 0))],
        out_specs=row(D_MODEL),
        out_shape=jax.ShapeDtypeStruct((rows, D_MODEL), F32),
        compiler_params=_cparams(("arbitrary",)),
        name="outproj",
    )(oa, ob, om, gate, x2d, w_bf16, g)


def kernel(x_prompt, x_sample, cache_win_k, cache_win_v, state_rwkv, state_rwkv_shift, cache_mem_k, cache_mem_v, mem_prompt, norm_in, w_in, rwkv_mu, rwkv_w0, rwkv_w2, rwkv_a0, rwkv_a2, rwkv_k_k, rwkv_k_a, rwkv_r_k, rwkv_lnx_g, rwkv_lnx_b, norm_mem, w_mem_kv, w_out, norm_final):
    B, S, _ = x_prompt.shape
    DB, T, _ = x_sample.shape
    depth = w_in.shape[0]
    assert depth == 1 and S % ATT_TILE == 0 and cache_win_k.shape[2] == max(DILATIONS) * BAND
    l = 0
    past_len = S

    half = ROPE_DIM // 2
    lane = jnp.arange(LANES)
    inv_freq = ROPE_THETA ** (-(lane % half).astype(F32) / half)
    invf = jnp.where((lane % HEAD_DIM) < ROPE_DIM, inv_freq, 0.0).reshape(1, LANES)

    row = lambda t: t.reshape(1, -1)
    w_in_b = w_in[l].astype(BF16)
    w_out_b = w_out[l].astype(BF16)
    w_kv_b = w_mem_kv[l].astype(BF16)
    zero = jnp.zeros((LORA_W, RWKV_W), F32)
    w2a2 = jnp.concatenate([jnp.concatenate([rwkv_w2[l], zero], axis=1),
                            jnp.concatenate([zero, rwkv_a2[l]], axis=1)], axis=0)
    w2a2_hi = w2a2.astype(BF16)
    w2a2_lo = (w2a2 - w2a2_hi.astype(F32)).astype(BF16)
    rw = (row(rwkv_mu[l]), row(rwkv_w0[l]), row(rwkv_a0[l]), w2a2_hi, w2a2_lo, row(rwkv_k_k[l]),
          row(rwkv_k_a[l]), row(rwkv_r_k[l]), row(rwkv_lnx_g[l]), row(rwkv_lnx_b[l]))

    tiles = S // PROJ_TILE
    n_base = -(-(tiles + 1) // 8) * 8
    pos_base = jnp.where(jnp.arange(n_base) < tiles, jnp.arange(n_base) * PROJ_TILE, past_len)
    cr, sr, cb, sb = _rope_tables(jnp.arange(PROJ_TILE, dtype=F32).reshape(-1, 1),
                                  pos_base.astype(F32).reshape(-1, 1), invf)

    xp = x_prompt.reshape(B * S, D_MODEL)
    q3, k3, v3, zb, qm, gate = _proj(xp, cr, sr, cb, sb, row(norm_in[l]), w_in_b, PROJ_TILE, True, tiles, 0, BF16)
    oa3 = _attn_prompt(q3, k3, v3, B, S)
    ob, st_p = _rwkv(zb.reshape(B, S, SHIFT_W), jnp.zeros((B, 1, SHIFT_W), F32),
                     jnp.zeros((B, 3, LANES, LANES), F32), *rw, nb=B, groups=2, act=BF16)
    mk, mv = _memkv(mem_prompt.reshape(B * N_MEM, D_MODEL), row(norm_mem[l]), w_kv_b)
    mk = mk.reshape(B, N_MEM, MEM_W)
    mv = mv.reshape(B, N_MEM, MEM_W)
    om = _memattn(qm.reshape(B, S, MEM_W), mk, mv, 1024, 1, False, BF16)
    y_p = _out(oa3, ob.reshape(B * S, RWKV_W), om.reshape(B * S, MEM_W), gate, xp, w_out_b,
               row(norm_final), 1024, True)

    win = min(max(DILATIONS) * BAND, S)
    tail = lambda t3: jnp.transpose(t3.reshape(3, B, S, LANES)[:, :, S - win:], (1, 2, 0, 3)).reshape(
        1, B, win, ATT_W // HEAD_DIM, HEAD_DIM)
    heads = lambda t, n: t.reshape(1, t.shape[0], t.shape[1], n, HEAD_DIM)

    xs = x_sample.reshape(DB * T, D_MODEL)
    qs, ks, vs, zbs, qms, gates = _proj(xs, jnp.tile(cr[:T], (DB, 1)), jnp.tile(sr[:T], (DB, 1)), cb, sb,
                                        row(norm_in[l]), w_in_b, DB * T, False, 1, tiles, F32)
    n_past = cache_win_k.shape[2]
    minor = lambda c: jnp.transpose(c, (0, 2, 3, 1)).reshape(c.shape[0], c.shape[2] * c.shape[3], c.shape[1])
    oas = _attn_sample(qs.reshape(DB, T, ATT_W), ks.reshape(DB, T, ATT_W), vs.reshape(DB, T, ATT_W),
                       minor(cache_win_k[l]), minor(cache_win_v[l]), 2)
    zbs3 = zbs.reshape(DB, T, SHIFT_W)
    obs, st_s = _rwkv(zbs3, state_rwkv_shift[l].reshape(DB, 1, SHIFT_W), _state_to_pairs(state_rwkv[l]),
                      *rw, nb=4, groups=1, act=F32)
    oms = _memattn(qms.reshape(DB, T, MEM_W), minor(cache_mem_k[l]), minor(cache_mem_v[l]), T, 8, True, F32)
    y_s = _out(oas.reshape(DB * T, ATT_W), obs.reshape(DB * T, RWKV_W), oms.reshape(DB * T, MEM_W), gates, xs,
               w_out_b, row(norm_final), DB * T, False)

    return (y_p.reshape(B, S, D_MODEL), y_s.reshape(DB, T, D_MODEL),
            tail(k3), tail(v3),
            _pairs_to_state(st_p)[None], zb.reshape(B, S, SHIFT_W)[:, -1][None],
            heads(mk, MEM_W // HEAD_DIM), heads(mv, MEM_W // HEAD_DIM),
            heads(ks.reshape(DB, T, ATT_W), ATT_W // HEAD_DIM), heads(vs.reshape(DB, T, ATT_W), ATT_W // HEAD_DIM),
            _pairs_to_state(st_s)[None], zbs3[:, -1][None])
```

```python
import functools

import jax
import jax.numpy as jnp
from jax import lax
from jax.experimental import pallas as pl
from jax.experimental.pallas import tpu as pltpu

F32 = jnp.float32
BF16 = jnp.bfloat16

D_MODEL = 1024
HEAD_DIM = 64
ATT_W = 384
RWKV_W = 384
MEM_W = 256
MIX_W = 1024
LORA_W = 64
SHIFT_W = 3 * RWKV_W + 2 * LORA_W
N_MEM = 256
ROPE_DIM = 16
ROPE_THETA = 500000.0
NORM_EPS = 1e-6
LNX_EPS = 64e-5
DILATIONS = (1, 4, 16)
BAND = 128
ATT_TILE = BAND * max(DILATIONS)
REGROUP = 4
LANES = 128
RWKV_CHUNK = 64
PROJ_TILE = 1024
NEG = -1e30
Q_SCALE = HEAD_DIM ** -0.5 * 1.4426950408889634
VMEM_LIMIT = 56 * 1024 * 1024


def _cparams(sem):
    return pltpu.CompilerParams(dimension_semantics=sem, vmem_limit_bytes=VMEM_LIMIT)


def _dot(a, b):
    return jnp.dot(a.astype(BF16), b.astype(BF16), preferred_element_type=F32)


def _dot_nt(a, b):
    return lax.dot_general(a.astype(BF16), b.astype(BF16), (((1,), (1,)), ((), ())),
                           preferred_element_type=F32)


def _dot_tn(a, b):
    return lax.dot_general(a.astype(BF16), b.astype(BF16), (((0,), (0,)), ((), ())),
                           preferred_element_type=F32)


def _dot_f32(a, b):
    return jnp.dot(a, b, preferred_element_type=F32, precision=lax.Precision.HIGHEST)


def _half0(shape):
    return (lax.broadcasted_iota(jnp.int32, shape, len(shape) - 1) & 64) == 0


def _rope_table_kernel(pos_row_ref, pos_base_ref, invf_ref, cr_ref, sr_ref, cb_ref, sb_ref):
    ang_r = pos_row_ref[...] * invf_ref[...]
    cr_ref[...] = jnp.cos(ang_r)
    sr_ref[...] = jnp.sin(ang_r)
    ang_b = pos_base_ref[...] * invf_ref[...]
    cb_ref[...] = jnp.cos(ang_b)
    sb_ref[...] = jnp.sin(ang_b)


def _rope_tables(pos_row, pos_base, invf):
    tab = lambda n: jax.ShapeDtypeStruct((n, LANES), F32)
    return pl.pallas_call(
        _rope_table_kernel,
        out_shape=[tab(pos_row.shape[0]), tab(pos_row.shape[0]), tab(pos_base.shape[0]), tab(pos_base.shape[0])],
        name="rope_tables",
    )(pos_row, pos_base, invf)


def _proj_kernel(x_ref, cr_ref, sr_ref, cb_ref, sb_ref, g_ref, w_ref, q_ref, k_ref, v_ref, zb_ref, qm_ref, gate_ref,
                 *, split_pairs, tiles_per_seq, base_offset):
    tm = x_ref.shape[0]
    base = base_offset + lax.rem(pl.program_id(0), tiles_per_seq)
    cb = cb_ref[pl.ds(base, 1), :]
    sb = sb_ref[pl.ds(base, 1), :]
    second = (lax.broadcasted_iota(jnp.int32, (1, LANES), 1) & 8) != 0

    def put(ref, p, rows, val):
        if split_pairs:
            ref[p, rows, :] = val
        else:
            ref[rows, p * LANES:(p + 1) * LANES] = val

    nsplit = 2 if tm % 16 == 0 else 1
    hm = tm // nsplit
    for part in range(nsplit):
        rows = pl.ds(part * hm, hm)
        x = x_ref[rows, :]
        ms = jnp.mean(x * x, axis=-1, keepdims=True)
        h = ((x * lax.rsqrt(ms + NORM_EPS)) * g_ref[...]).astype(BF16)
        cr = cr_ref[rows, :]
        sr = sr_ref[rows, :]
        cos = cb * cr - sb * sr
        sin = sb * cr + cb * sr
        sin_up = jnp.where(second, sin, 0.0)
        sin_dn = jnp.where(second, 0.0, -sin)

        def rope(t):
            return t * cos + pltpu.roll(t, 8, 1) * sin_up + pltpu.roll(t, LANES - 8, 1) * sin_dn

        qkv = jnp.dot(h, w_ref[:, :3 * ATT_W], preferred_element_type=F32)
        for p in range(ATT_W // LANES):
            put(q_ref, p, rows, rope(qkv[:, p * LANES:(p + 1) * LANES]) * Q_SCALE)
            put(k_ref, p, rows, rope(qkv[:, ATT_W + p * LANES:ATT_W + (p + 1) * LANES]))
            put(v_ref, p, rows, qkv[:, 2 * ATT_W + p * LANES:2 * ATT_W + (p + 1) * LANES])
        c0 = 3 * ATT_W
        zb_ref[rows, :] = jnp.dot(h, w_ref[:, c0:c0 + SHIFT_W], preferred_element_type=F32)
        c0 += SHIFT_W
        qm_ref[rows, :] = (jnp.dot(h, w_ref[:, c0:c0 + MEM_W], preferred_element_type=F32) * Q_SCALE).astype(
            qm_ref.dtype)
        c0 += MEM_W
        gate_ref[rows, :] = jnp.dot(h, w_ref[:, c0:c0 + MIX_W], preferred_element_type=F32).astype(gate_ref.dtype)


def _proj(x2d, cr, sr, cb, sb, g, w_bf16, tm, split_pairs, tiles_per_seq, base_offset, act):
    rows = x2d.shape[0]
    in_w = w_bf16.shape[1]
    if split_pairs:
        qkv_shape = jax.ShapeDtypeStruct((3, rows, LANES), F32)
        qkv_spec = pl.BlockSpec((3, tm, LANES), lambda i: (0, i, 0))
    else:
        qkv_shape = jax.ShapeDtypeStruct((rows, ATT_W), F32)
        qkv_spec = pl.BlockSpec((tm, ATT_W), lambda i: (i, 0))
    row = lambda w: pl.BlockSpec((tm, w), lambda i: (i, 0))
    full = lambda a: pl.BlockSpec(a.shape, lambda i: (0, 0))
    return pl.pallas_call(
        functools.partial(_proj_kernel, split_pairs=split_pairs, tiles_per_seq=tiles_per_seq,
                          base_offset=base_offset),
        grid=(rows // tm,),
        in_specs=[row(D_MODEL), full(cr), full(sr), full(cb), full(sb), full(g), full(w_bf16)],
        out_specs=[qkv_spec, qkv_spec, qkv_spec, row(SHIFT_W), row(MEM_W), row(MIX_W)],
        out_shape=[qkv_shape, qkv_shape, qkv_shape,
                   jax.ShapeDtypeStruct((rows, SHIFT_W), F32),
                   jax.ShapeDtypeStruct((rows, MEM_W), act),
                   jax.ShapeDtypeStruct((rows, MIX_W), act)],
        compiler_params=_cparams(("arbitrary",)),
        name="proj",
    )(x2d, cr, sr, cb, sb, g, w_bf16)


def _attn_blocks(npairs, blocks):
    h0 = _half0((BAND, LANES))
    h0k = _half0((2 * BAND, LANES))
    units = [(b, p) for b in range(len(blocks)) for p in range(npairs)]
    heads = [(i, hh) for i in range(len(units)) for hh in range(2)]
    q = [blocks[b][0](p) for b, p in units]
    kcat = [jnp.concatenate([blocks[b][1](p), blocks[b][2](p)], axis=0).astype(BF16) for b, p in units]
    vcat = [jnp.concatenate([blocks[b][3](p), blocks[b][4](p)], axis=0) for b, p in units]
    s = [_dot_nt(jnp.where(h0 if hh == 0 else jnp.logical_not(h0), q[i], 0.0), kcat[i]) + blocks[units[i][0]][5]
         for i, hh in heads]
    m = [jnp.max(s[j], axis=-1, keepdims=True) for j in range(len(heads))]
    pexp = [jnp.exp2(s[j] - m[j]).astype(BF16) for j in range(len(heads))]
    res = [jnp.dot(pexp[j], jnp.where(h0k if hh == 0 else jnp.logical_not(h0k), vcat[i], 1.0).astype(BF16),
                   preferred_element_type=F32) for j, (i, hh) in enumerate(heads)]
    lsum = [pltpu.roll(jnp.where(h0, res[2 * i + 1], res[2 * i]), HEAD_DIM, 1) for i in range(len(units))]
    outs = [[] for _ in blocks]
    for i, (b, p) in enumerate(units):
        outs[b].append((jnp.where(h0, res[2 * i], res[2 * i + 1]) / lsum[i],
                        jnp.where(h0, m[2 * i], m[2 * i + 1]) + jnp.log2(lsum[i])))
    return outs


def _attn_kernel(q_ref, kc_ref, kp_ref, vc_ref, vp_ref, o_ref, o_sc, lse_sc, yq_sc, ykv_sc):
    step = pl.program_id(1)
    has_prev = step > 0
    slot = step & 1
    prev_slot = 1 - slot
    qi = lax.broadcasted_iota(jnp.int32, (BAND, 2 * BAND), 0)
    kj = lax.broadcasted_iota(jnp.int32, (BAND, 2 * BAND), 1)
    in_cur = (kj >= BAND) & (kj - BAND <= qi)
    in_prev = (kj < BAND) & (kj >= qi)
    bias_full = jnp.where(in_cur | in_prev, 0.0, NEG).astype(F32)
    bias_first = jnp.where(in_cur | (in_prev & has_prev), 0.0, NEG).astype(F32)
    npairs = q_ref.shape[0]
    nblk = ATT_TILE // BAND
    per_iter = 2
    seg = ATT_TILE // REGROUP

    @pl.when(step == 0)
    def _():
        ykv_sc[prev_slot] = jnp.zeros(ykv_sc.shape[1:], F32)

    def regroup(c, carry):
        per_res = seg // BAND
        res, part = c >> (per_res.bit_length() - 1), c & (per_res - 1)
        src = pl.ds(res + part * (BAND * REGROUP), BAND, stride=REGROUP)
        dst = pl.ds(pl.multiple_of(c * BAND, BAND), BAND)
        for p in range(npairs):
            yq_sc[p, dst, :] = q_ref[p, src, :]
            ykv_sc[slot, 0, p, dst, :] = kc_ref[p, src, :]
            ykv_sc[slot, 1, p, dst, :] = vc_ref[p, src, :]
        return carry

    lax.fori_loop(0, nblk, regroup, 0)

    def desc(d, u, first):
        span = BAND * d
        bias = bias_first if first else bias_full
        if d == 1:
            qs = u * BAND if isinstance(u, int) else pl.multiple_of(u * BAND, BAND)
            rows = pl.ds(qs, BAND)
            prev = (lambda ref_c, ref_p: (lambda p: ref_p[p])) if first else (
                lambda ref_c, ref_p: (lambda p: ref_c[p, pl.ds(qs - BAND, BAND), :]))
            return (lambda p: q_ref[p, rows, :], prev(kc_ref, kp_ref), lambda p: kc_ref[p, rows, :],
                    prev(vc_ref, vp_ref), lambda p: vc_ref[p, rows, :], bias), qs
        assert d % REGROUP == 0
        sub = d // REGROUP
        sp, r = u >> (d.bit_length() - 1), u & (d - 1)
        qs = sp * span + r
        start = (r & (REGROUP - 1)) * seg + sp * (span // REGROUP) + (r >> (REGROUP.bit_length() - 1))
        rows = pl.ds(start, BAND, stride=sub)
        if first:
            prow, pslot = pl.ds(start + seg - span // REGROUP, BAND, stride=sub), prev_slot
        else:
            prow, pslot = pl.ds(start - span // REGROUP, BAND, stride=sub), slot
        return (lambda p: yq_sc[p, rows, :],
                lambda p: ykv_sc[pslot, 0, p, prow, :], lambda p: ykv_sc[slot, 0, p, rows, :],
                lambda p: ykv_sc[pslot, 1, p, prow, :], lambda p: ykv_sc[slot, 1, p, rows, :], bias), qs

    def merge(outs, start):
        row = pl.ds(start, BAND)
        for p, (o1, lse1) in enumerate(outs):
            os_ = [o1] + [o_sc[j, p, row, :] for j in range(len(DILATIONS) - 1)]
            ls_ = [lse1] + [lse_sc[j, p, row, :] for j in range(len(DILATIONS) - 1)]
            top = functools.reduce(jnp.maximum, ls_)
            ws = [jnp.exp2(l - top) for l in ls_]
            num = functools.reduce(lambda a, b: a + b, [w * o for w, o in zip(ws, os_)])
            o_ref[p, row, :] = (num / functools.reduce(lambda a, b: a + b, ws)).astype(o_ref.dtype)

    def run(j, d, descs):
        for (_, qs), outs in zip(descs, _attn_blocks(npairs, [blk for blk, _ in descs])):
            if j < 0:
                merge(outs, qs)
            else:
                for p, (o, lse) in enumerate(outs):
                    o_sc[j, p, pl.ds(qs, BAND, stride=d), :] = o
                    lse_sc[j, p, pl.ds(qs, BAND, stride=d), :] = lse

    def sweep(j, d):
        n_first = d
        several = lambda first: lambda i, c: (run(j, d, [desc(d, i * per_iter + k, first) for k in range(per_iter)]),
                                              c)[1]
        if n_first % per_iter == 0:
            lax.fori_loop(0, n_first // per_iter, several(True), 0)
            lo = n_first
        else:
            assert n_first == 1 and per_iter == 2
            run(j, d, [desc(d, 0, True), desc(d, 1, False)])
            lo = per_iter
        if lo < nblk:
            lax.fori_loop(lo // per_iter, nblk // per_iter, several(False), 0)

    for j, d in enumerate(DILATIONS[1:]):
        sweep(j, d)
    sweep(-1, DILATIONS[0])


def _attn_prompt(q3, k3, v3, batch, seq):
    nt = seq // ATT_TILE
    per_tile = ATT_TILE // BAND
    cur = pl.BlockSpec((3, ATT_TILE, LANES), lambda b, i: (0, b * nt + i, 0))
    prev = pl.BlockSpec((3, BAND, LANES), lambda b, i: (0, jnp.maximum((b * nt + i) * per_tile - 1, 0), 0))
    return pl.pallas_call(
        _attn_kernel,
        grid=(batch, nt),
        in_specs=[cur, cur, prev, cur, prev],
        out_specs=cur,
        out_shape=jax.ShapeDtypeStruct(q3.shape, BF16),
        scratch_shapes=[pltpu.VMEM((len(DILATIONS) - 1, 3, ATT_TILE, LANES), F32),
                        pltpu.VMEM((len(DILATIONS) - 1, 3, ATT_TILE, LANES), F32),
                        pltpu.VMEM((3, ATT_TILE, LANES), F32),
                        pltpu.VMEM((2, 2, 3, ATT_TILE, LANES), F32)],
        compiler_params=_cparams(("arbitrary", "arbitrary")),
        name="attn_prompt",
    )(q3, k3, k3, v3, v3)


def _attn_sample_kernel(q_ref, kn_ref, vn_ref, kc_ref, vc_ref, o_ref, *, n_past, t_new, nb):
    nh = ATT_W // HEAD_DIM
    head_of_lane = lax.broadcasted_iota(jnp.int32, (t_new, ATT_W), 1) >> 6
    seqs = range(nb)
    qst = [jnp.concatenate([jnp.where(head_of_lane == h, q_ref[b], 0.0) for h in range(nh)], axis=0) for b in seqs]

    def count(delta):
        c = jnp.zeros(delta.shape, F32)
        for d in DILATIONS:
            ok = (delta >= 0) & (delta <= BAND * d) & ((delta & (d - 1)) == 0)
            c = c + jnp.where(ok, 1.0, 0.0)
        return c

    rows = nh * t_new
    t_past = lax.broadcasted_iota(jnp.int32, (rows, n_past), 0) & (t_new - 1)
    cnt_past = count(n_past + t_past - lax.broadcasted_iota(jnp.int32, (rows, n_past), 1))
    t_n = lax.broadcasted_iota(jnp.int32, (rows, t_new), 0) & (t_new - 1)
    cnt_new = count(t_n - lax.broadcasted_iota(jnp.int32, (rows, t_new), 1))

    s_past = [jnp.where(cnt_past > 0, _dot(qst[b], kc_ref[b]), NEG) for b in seqs]
    s_new = [jnp.where(cnt_new > 0, _dot_nt(qst[b], kn_ref[b]), NEG) for b in seqs]
    m = [jnp.maximum(jnp.max(s_past[b], axis=-1, keepdims=True), jnp.max(s_new[b], axis=-1, keepdims=True))
         for b in seqs]
    p_past = [cnt_past * jnp.exp2(s_past[b] - m[b]) for b in seqs]
    p_new = [cnt_new * jnp.exp2(s_new[b] - m[b]) for b in seqs]
    l = [jnp.sum(p_past[b], axis=-1, keepdims=True) + jnp.sum(p_new[b], axis=-1, keepdims=True) for b in seqs]
    o = [(_dot_nt(p_past[b], vc_ref[b]) + _dot(p_new[b], vn_ref[b])) / l[b] for b in seqs]
    for b in seqs:
        out = jnp.zeros((t_new, ATT_W), F32)
        for h in range(nh):
            out = jnp.where(head_of_lane == h, o[b][h * t_new:(h + 1) * t_new, :], out)
        o_ref[b] = out.astype(o_ref.dtype)


def _attn_sample(q, k_new, v_new, k_cache, v_cache, nb):
    db, t_new, _ = q.shape
    n_past = k_cache.shape[2]
    new = pl.BlockSpec((nb, t_new, ATT_W), lambda b: (b, 0, 0))
    cache = pl.BlockSpec((nb, ATT_W, n_past), lambda b: (b, 0, 0))
    return pl.pallas_call(
        functools.partial(_attn_sample_kernel, n_past=n_past, t_new=t_new, nb=nb),
        grid=(db // nb,),
        in_specs=[new, new, new, cache, cache],
        out_specs=new,
        out_shape=jax.ShapeDtypeStruct(q.shape, F32),
        compiler_params=_cparams(("arbitrary",)),
        name="attn_sample",
    )(q, k_new, v_new, k_cache, v_cache)


def _seg_sum(x, h0):
    s0 = jnp.sum(jnp.where(h0, x, 0.0), axis=-1, keepdims=True)
    s1 = jnp.sum(jnp.where(h0, 0.0, x), axis=-1, keepdims=True)
    return jnp.where(h0, s0, s1)


def _split_bf16(x):
    hi = x.astype(BF16)
    return hi, (x - hi.astype(F32)).astype(BF16)


def _rwkv_kernel(z_ref, sh0_ref, st0_ref, mu_ref, w0_ref, a0_ref, whi_ref, wlo_ref, kk_ref, ka_ref, rk_ref,
                 lng_ref, lnb_ref, o_ref, st_ref, st_sc, prev_sc, *, nb, groups, n_valid):
    C = RWKV_CHUNK
    R = groups * C
    npairs = RWKV_W // LANES

    @pl.when(pl.program_id(1) == 0)
    def _():
        zero = jnp.zeros((HEAD_DIM, HEAD_DIM), F32)
        for b in range(nb):
            for p in range(npairs):
                st_sc[b, p] = jnp.concatenate(
                    [jnp.concatenate([st0_ref[b, 2 * p], zero], axis=1),
                     jnp.concatenate([zero, st0_ref[b, 2 * p + 1]], axis=1)], axis=0)
        prev_sc[...] = sh0_ref[...]

    si = lax.broadcasted_iota(jnp.int32, (2 * C, 2 * C), 0)
    sj = lax.broadcasted_iota(jnp.int32, (2 * C, 2 * C), 1)
    same_head = (si >= C) == (sj >= C)
    ti2 = si & (C - 1)
    tj2 = sj & (C - 1)
    incl = same_head & (tj2 <= ti2)
    strict = same_head & (tj2 < ti2)
    eye = jnp.where(si == sj, 1.0, 0.0).astype(F32)
    levels = []
    s = 2
    while s < min(C, n_valid):
        sh = s.bit_length() - 1
        levels.append(((ti2 >> (sh + 1)) == (tj2 >> (sh + 1))) & (((ti2 >> sh) & 1) == 1) & (((tj2 >> sh) & 1) == 0))
        s *= 2
    first_level = (ti2 >> 1) == (tj2 >> 1)

    ri = lax.broadcasted_iota(jnp.int32, (C, C), 0)
    rj = lax.broadcasted_iota(jnp.int32, (C, C), 1)
    tri = jnp.where(rj <= ri, 1.0, 0.0).astype(BF16)
    rowid = lax.broadcasted_iota(jnp.int32, (C, 1), 0)
    h0 = _half0((C, LANES))
    zero_blk = jnp.zeros((2 * C, LANES), BF16)
    dot = functools.partial(jnp.dot, preferred_element_type=F32)
    pre = {}

    def stack(x):
        return jnp.concatenate([jnp.where(h0, x, 0.0), jnp.where(h0, 0.0, x)], axis=0)

    def prep(b, g):
        if n_valid < C:
            z = jnp.concatenate([z_ref[b], jnp.zeros((C - n_valid, SHIFT_W), F32)], axis=0)
        else:
            z = z_ref[b, g * C:(g + 1) * C, :]
        before = prev_sc[b] if g == 0 else z_ref[b, pl.ds(g * C - 1, 1), :]
        z_prev = jnp.where(rowid == 0, before, pltpu.roll(z, 1, 0))
        zs = z + (z_prev - z) * mu_ref[...]
        valid = rowid < (n_valid - g * C)
        if n_valid < (g + 1) * C:
            zs = jnp.where(valid, zs, 0.0)
        lat = zs[:, 3 * RWKV_W:]
        lat_hi, lat_lo = _split_bf16(jnp.where(_half0(lat.shape), jnp.tanh(lat), lat))
        lora = dot(lat_hi, whi_ref[...]) + dot(lat_lo, whi_ref[...]) + dot(lat_hi, wlo_ref[...])
        w = -jax.nn.softplus(-(w0_ref[...] + lora[:, :RWKV_W])) - 0.5
        lw = -jnp.exp(w)
        if n_valid < (g + 1) * C:
            lw = jnp.where(valid, lw, 0.0)
        a = jax.nn.sigmoid(a0_ref[...] + lora[:, RWKV_W:])
        lw_hi, lw_lo = _split_bf16(lw)
        cum = dot(tri, lw_hi) + dot(tri, lw_lo)
        for p in range(npairs):
            sl = slice(p * LANES, (p + 1) * LANES)
            r = zs[:, sl]
            k = zs[:, RWKV_W + p * LANES:RWKV_W + (p + 1) * LANES]
            v = zs[:, 2 * RWKV_W + p * LANES:2 * RWKV_W + (p + 1) * LANES]
            ap = a[:, sl]
            kk = k * kk_ref[:, sl]
            kk = kk * jnp.minimum(lax.rsqrt(_seg_sum(kk * kk, h0)), 1e12)
            k = k * (1.0 + (ap - 1.0) * ka_ref[:, sl])
            bb = kk * ap
            L = cum[:, sl]
            l_end = L[C - 1:C, :]
            e_out = jnp.exp(-L)
            e_end = jnp.exp(l_end - L)
            rt = stack(r * jnp.exp(L))
            kkh = stack(kk * jnp.exp(L - lw[:, sl])).astype(BF16)
            left = jnp.concatenate([rt.astype(BF16), kkh], axis=0)
            right = jnp.concatenate([stack(k * e_out), stack(-(bb * e_out))], axis=0).astype(BF16)
            ends = jnp.concatenate([stack(k * e_end), stack(-(bb * e_end))], axis=0).astype(BF16)
            pre[(b, g, p)] = dict(left=left, right=right, ends=ends, rt=rt, kkh=kkh, v2=stack(v).astype(BF16),
                                  decay=jnp.exp(l_end), bonus=_seg_sum(r * k * rk_ref[:, sl], h0) * v)

    def transition_stages(chains):
        def scores():
            for c in chains:
                d = pre[c]
                aa = lax.dot_general(d["left"], d["right"], (((1,), (1,)), ((), ())), preferred_element_type=F32)
                d["a_r"] = jnp.concatenate([jnp.where(incl, aa[:2 * C, :2 * C], 0.0),
                                            jnp.where(incl, aa[:2 * C, 2 * C:], 0.0)], axis=1).astype(BF16)
                d["a_kk"] = jnp.where(strict, aa[2 * C:, :2 * C], 0.0).astype(BF16)
                n_kb = jnp.where(strict, aa[2 * C:, 2 * C:], 0.0)
                d["n_kb"] = n_kb.astype(BF16)
                d["t"] = (eye + jnp.where(first_level, n_kb, 0.0)).astype(BF16)

        def odd_rows(x, s):
            return jnp.concatenate([x[i:i + s] for i in range(s, 2 * C, 2 * s)], axis=0) if s >= 16 else x

        def level_a(off, s):
            for c in chains:
                pre[c]["tn"] = dot(odd_rows(pre[c]["t"], s), jnp.where(off, pre[c]["n_kb"], 0.0)).astype(BF16)

        def level_b(s):
            for c in chains:
                t = pre[c]["t"]
                new = odd_rows(t, s) + dot(pre[c]["tn"], t).astype(BF16)
                if s >= 16:
                    parts = []
                    for j, i in enumerate(range(0, 2 * C, 2 * s)):
                        parts += [t[i:i + s], new[j * s:(j + 1) * s]]
                    new = jnp.concatenate(parts, axis=0)
                pre[c]["t"] = new

        def akv():
            for c in chains:
                pre[c]["akv"] = dot(pre[c]["a_kk"], pre[c]["v2"]).astype(BF16)

        def solve():
            for c in chains:
                d = pre[c]
                tr = dot(d["t"], jnp.concatenate([d["kkh"], d["akv"]], axis=1)).astype(BF16)
                kq, u0 = tr[:, :LANES], tr[:, LANES:]
                d["big"] = jnp.concatenate([jnp.concatenate([d["v2"], zero_blk], axis=1),
                                            jnp.concatenate([u0, kq], axis=1)], axis=0)
        def readout():
            for c in chains:
                d = pre[c]
                yr = dot(d["a_r"], d["big"])
                d["y0"] = yr[:, :LANES]
                d["rq"] = (d["rt"] + yr[:, LANES:]).astype(BF16)
        def update():
            for c in chains:
                d = pre[c]
                dg = lax.dot_general(d["big"], d["ends"], (((0,), (0,)), ((), ())), preferred_element_type=F32)
                d["dd"] = dg[:LANES]
                d["gm"] = dg[LANES:].astype(BF16)
        stages = [scores]
        for i, off in enumerate(levels):
            stages += [functools.partial(level_a, off, 2 << i), functools.partial(level_b, 2 << i)]
        return stages + [akv, solve, readout, update]

    state = {(b, p): st_sc[b, p] for b in range(nb) for p in range(npairs)}

    def carried(chains):
        for (b, g, p) in chains:
            d = pre[(b, g, p)]
            st = state[(b, p)]
            st_b = st.astype(BF16)
            d["y2"] = lax.dot_general(d["rq"], st_b, (((1,), (1,)), ((), ())), preferred_element_type=F32) + d["y0"]
            state[(b, p)] = st * d["decay"] + dot(st_b, d["gm"]) + d["dd"]
        for (b, g, p) in chains:
            d = pre.pop((b, g, p))
            y2 = d["y2"]
            y = y2[:C] + y2[C:]
            sl = slice(p * LANES, (p + 1) * LANES)
            mean = _seg_sum(y, h0) * (1.0 / HEAD_DIM)
            yc = y - mean
            var = _seg_sum(yc * yc, h0) * (1.0 / HEAD_DIM)
            yn = yc * lax.rsqrt(var + LNX_EPS) * lng_ref[:, sl] + lnb_ref[:, sl]
            rows_out = min(C, n_valid - g * C)
            o_ref[b, g * C:g * C + rows_out, sl] = (yn + d["bonus"])[:rows_out].astype(o_ref.dtype)

    items = [(b, g) for g in range(groups) for b in range(nb)]
    for item in items:
        prep(*item)
    chains = [(b, g, p) for (b, g) in items for p in range(npairs)]
    for stage in transition_stages(chains):
        stage()
    carried(chains)
    for b in range(nb):
        prev_sc[b] = z_ref[b, pl.ds(min(R, n_valid) - 1, 1), :]
    for (b, p), st in state.items():
        st_sc[b, p] = st

    @pl.when(pl.program_id(1) == pl.num_programs(1) - 1)
    def _():
        for b in range(nb):
            for p in range(npairs):
                st_ref[b, 2 * p] = st_sc[b, p, :HEAD_DIM, :HEAD_DIM]
                st_ref[b, 2 * p + 1] = st_sc[b, p, HEAD_DIM:, HEAD_DIM:]


def _rwkv(z, shift0, st0, mu, w0, a0, w_hi, w_lo, k_k, k_a, r_k, ln_g, ln_b, nb, groups, act):
    b, s, _ = z.shape
    rows = groups * RWKV_CHUNK
    if s % rows:
        assert groups == 1 and s < rows and s % 8 == 0
        rows = s
    n_valid = rows
    vec = lambda w: pl.BlockSpec((1, w), lambda i, j: (0, 0))
    st_spec = pl.BlockSpec((nb, 2 * 3, HEAD_DIM, HEAD_DIM), lambda i, j: (i, 0, 0, 0))
    wspec = pl.BlockSpec((LANES, 2 * RWKV_W), lambda i, j: (0, 0))
    return pl.pallas_call(
        functools.partial(_rwkv_kernel, nb=nb, groups=groups, n_valid=n_valid),
        grid=(b // nb, s // rows),
        in_specs=[pl.BlockSpec((nb, rows, SHIFT_W), lambda i, j: (i, j, 0)),
                  pl.BlockSpec((nb, 1, SHIFT_W), lambda i, j: (i, 0, 0)),
                  st_spec,
                  vec(SHIFT_W), vec(RWKV_W), vec(RWKV_W), wspec, wspec,
                  vec(RWKV_W), vec(RWKV_W), vec(RWKV_W), vec(RWKV_W), vec(RWKV_W)],
        out_specs=[pl.BlockSpec((nb, rows, RWKV_W), lambda i, j: (i, j, 0)), st_spec],
        out_shape=[jax.ShapeDtypeStruct((b, s, RWKV_W), act),
                   jax.ShapeDtypeStruct((b, 2 * 3, HEAD_DIM, HEAD_DIM), F32)],
        scratch_shapes=[pltpu.VMEM((nb, 3, LANES, LANES), F32), pltpu.VMEM((nb, 1, SHIFT_W), F32)],
        compiler_params=_cparams(("arbitrary", "arbitrary")),
        name="rwkv",
    )(z, shift0, st0, mu, w0, a0, w_hi, w_lo, k_k, k_a, r_k, ln_g, ln_b)


def _memkv_kernel(x_ref, g_ref, w_ref, k_ref, v_ref):
    x = x_ref[...]
    ms = jnp.mean(x * x, axis=-1, keepdims=True)
    h = ((x * lax.rsqrt(ms + NORM_EPS)) * g_ref[...]).astype(BF16)
    k_ref[...] = jnp.dot(h, w_ref[:, :MEM_W], preferred_element_type=F32)
    v_ref[...] = jnp.dot(h, w_ref[:, MEM_W:], preferred_element_type=F32)


def _memkv(mem2d, g, w_bf16):
    rows = mem2d.shape[0]
    out = jax.ShapeDtypeStruct((rows, MEM_W), F32)
    return pl.pallas_call(
        _memkv_kernel,
        out_shape=[out, out],
        compiler_params=pltpu.CompilerParams(vmem_limit_bytes=VMEM_LIMIT),
        name="memkv",
    )(mem2d, g, w_bf16)


def _memattn_kernel(q_ref, mk_ref, mv_ref, o_ref, *, nb, mem_minor):
    head_of_lane = lax.broadcasted_iota(jnp.int32, q_ref.shape[1:], 1) >> 6
    qs = [q_ref[b] for b in range(nb)]
    mks = [mk_ref[b].astype(BF16) for b in range(nb)]
    mvs = [mv_ref[b].astype(BF16) for b in range(nb)]
    outs = [jnp.zeros(q_ref.shape[1:], F32) for _ in range(nb)]
    for h in range(MEM_W // HEAD_DIM):
        mine = head_of_lane == h
        qk = _dot if mem_minor else _dot_nt
        pv = _dot_nt if mem_minor else _dot
        s = [qk(jnp.where(mine, qs[b], 0.0), mks[b]) for b in range(nb)]
        pexp = [jnp.exp2(s[b] - jnp.max(s[b], axis=-1, keepdims=True)) for b in range(nb)]
        l = [jnp.sum(pexp[b], axis=-1, keepdims=True) for b in range(nb)]
        outs = [jnp.where(mine, pv(pexp[b], mvs[b]) / l[b], outs[b]) for b in range(nb)]
    for b in range(nb):
        o_ref[b] = outs[b].astype(o_ref.dtype)


def _memattn(q, mk, mv, tm, nb, mem_minor, act):
    b, s, _ = q.shape
    qspec = pl.BlockSpec((nb, tm, MEM_W), lambda i, j: (i, j, 0))
    mspec = pl.BlockSpec((nb,) + mk.shape[1:], lambda i, j: (i, 0, 0))
    return pl.pallas_call(
        functools.partial(_memattn_kernel, nb=nb, mem_minor=mem_minor),
        grid=(b // nb, s // tm),
        in_specs=[qspec, mspec, mspec],
        out_specs=qspec,
        out_shape=jax.ShapeDtypeStruct(q.shape, act),
        compiler_params=_cparams(("arbitrary", "arbitrary")),
        name="memattn",
    )(q, mk, mv)


def _out_kernel(oa_ref, ob_ref, om_ref, gate_ref, x_ref, w_ref, g_ref, y_ref, *, split_pairs):
    acc = x_ref[...]

    def add(acc, o, c0):
        width = o.shape[1]
        gate = gate_ref[:, c0:c0 + width]
        return acc + _dot(o * (gate * jax.nn.sigmoid(gate)), w_ref[c0:c0 + width, :])

    if split_pairs:
        for p in range(ATT_W // LANES):
            acc = add(acc, oa_ref[p], p * LANES)
    else:
        acc = add(acc, oa_ref[...], 0)
    acc = add(acc, ob_ref[...], ATT_W)
    acc = add(acc, om_ref[...], ATT_W + RWKV_W)
    ms = jnp.mean(acc * acc, axis=-1, keepdims=True)
    y_ref[...] = (acc * lax.rsqrt(ms + NORM_EPS)) * g_ref[...]


def _out(oa, ob, om, gate, x2d, w_bf16, g, tm, split_pairs):
    rows = x2d.shape[0]
    row = lambda w: pl.BlockSpec((tm, w), lambda i: (i, 0))
    oa_spec = pl.BlockSpec((3, tm, LANES), lambda i: (0, i, 0)) if split_pairs else row(ATT_W)
    return pl.pallas_call(
        functools.partial(_out_kernel, split_pairs=split_pairs),
        grid=(rows // tm,),
        in_specs=[oa_spec, row(RWKV_W), row(MEM_W), row(MIX_W), row(D_MODEL),
                  pl.BlockSpec((MIX_W, D_MODEL), lambda i: (0, 0)),
                  pl.BlockSpec((1, D_MODEL), lambda i: (0, 0))],
        out_specs=row(D_MODEL),
        out_shape=jax.ShapeDtypeStruct((rows, D_MODEL), F32),
        compiler_params=_cparams(("arbitrary",)),
        name="outproj",
    )(oa, ob, om, gate, x2d, w_bf16, g)


def kernel(x_prompt, x_sample, cache_win_k, cache_win_v, state_rwkv, state_rwkv_shift, cache_mem_k, cache_mem_v, mem_prompt, norm_in, w_in, rwkv_mu, rwkv_w0, rwkv_w2, rwkv_a0, rwkv_a2, rwkv_k_k, rwkv_k_a, rwkv_r_k, rwkv_lnx_g, rwkv_lnx_b, norm_mem, w_mem_kv, w_out, norm_final):
    B, S, _ = x_prompt.shape
    DB, T, _ = x_sample.shape
    depth = w_in.shape[0]
    assert depth == 1 and S % ATT_TILE == 0 and cache_win_k.shape[2] == max(DILATIONS) * BAND
    l = 0
    past_len = S

    half = ROPE_DIM // 2
    lane = jnp.arange(LANES)
    inv_freq = ROPE_THETA ** (-(lane % half).astype(F32) / half)
    invf = jnp.where((lane % HEAD_DIM) < ROPE_DIM, inv_freq, 0.0).reshape(1, LANES)

    row = lambda t: t.reshape(1, -1)
    w_in_b = w_in[l].astype(BF16)
    w_out_b = w_out[l].astype(BF16)
    w_kv_b = w_mem_kv[l].astype(BF16)
    zero = jnp.zeros((LORA_W, RWKV_W), F32)
    w2a2 = jnp.concatenate([jnp.concatenate([rwkv_w2[l], zero], axis=1),
                            jnp.concatenate([zero, rwkv_a2[l]], axis=1)], axis=0)
    w2a2_hi = w2a2.astype(BF16)
    w2a2_lo = (w2a2 - w2a2_hi.astype(F32)).astype(BF16)
    rw = (row(rwkv_mu[l]), row(rwkv_w0[l]), row(rwkv_a0[l]), w2a2_hi, w2a2_lo, row(rwkv_k_k[l]),
          row(rwkv_k_a[l]), row(rwkv_r_k[l]), row(rwkv_lnx_g[l]), row(rwkv_lnx_b[l]))

    tiles = S // PROJ_TILE
    n_base = -(-(tiles + 1) // 8) * 8
    pos_base = jnp.where(jnp.arange(n_base) < tiles, jnp.arange(n_base) * PROJ_TILE, past_len)
    cr, sr, cb, sb = _rope_tables(jnp.arange(PROJ_TILE, dtype=F32).reshape(-1, 1),
                                  pos_base.astype(F32).reshape(-1, 1), invf)

    xp = x_prompt.reshape(B * S, D_MODEL)
    q3, k3, v3, zb, qm, gate = _proj(xp, cr, sr, cb, sb, row(norm_in[l]), w_in_b, PROJ_TILE, True, tiles, 0, BF16)
    oa3 = _attn_prompt(q3, k3, v3, B, S)
    ob, st_p = _rwkv(zb.reshape(B, S, SHIFT_W), jnp.zeros((B, 1, SHIFT_W), F32),
                     jnp.zeros((B, RWKV_W // HEAD_DIM, HEAD_DIM, HEAD_DIM), F32), *rw, nb=B, groups=2, act=BF16)
    mk, mv = _memkv(mem_prompt.reshape(B * N_MEM, D_MODEL), row(norm_mem[l]), w_kv_b)
    mk = mk.reshape(B, N_MEM, MEM_W)
    mv = mv.reshape(B, N_MEM, MEM_W)
    om = _memattn(qm.reshape(B, S, MEM_W), mk, mv, 1024, 1, False, BF16)
    y_p = _out(oa3, ob.reshape(B * S, RWKV_W), om.reshape(B * S, MEM_W), gate, xp, w_out_b,
               row(norm_final), 1024, True)

    win = min(max(DILATIONS) * BAND, S)
    tail = lambda t3: jnp.transpose(t3.reshape(3, B, S, LANES)[:, :, S - win:], (1, 2, 0, 3)).reshape(
        1, B, win, ATT_W // HEAD_DIM, HEAD_DIM)
    heads = lambda t, n: t.reshape(1, t.shape[0], t.shape[1], n, HEAD_DIM)

    xs = x_sample.reshape(DB * T, D_MODEL)
    qs, ks, vs, zbs, qms, gates = _proj(xs, jnp.tile(cr[:T], (DB, 1)), jnp.tile(sr[:T], (DB, 1)), cb, sb,
                                        row(norm_in[l]), w_in_b, DB * T, False, 1, tiles, F32)
    n_past = cache_win_k.shape[2]
    minor = lambda c: jnp.transpose(c, (0, 2, 3, 1)).reshape(c.shape[0], c.shape[2] * c.shape[3], c.shape[1])
    oas = _attn_sample(qs.reshape(DB, T, ATT_W), ks.reshape(DB, T, ATT_W), vs.reshape(DB, T, ATT_W),
                       minor(cache_win_k[l]), minor(cache_win_v[l]), 2)
    zbs3 = zbs.reshape(DB, T, SHIFT_W)
    obs, st_s = _rwkv(zbs3, state_rwkv_shift[l].reshape(DB, 1, SHIFT_W), state_rwkv[l],
                      *rw, nb=4, groups=1, act=F32)
    oms = _memattn(qms.reshape(DB, T, MEM_W), minor(cache_mem_k[l]), minor(cache_mem_v[l]), T, 8, True, F32)
    y_s = _out(oas.reshape(DB * T, ATT_W), obs.reshape(DB * T, RWKV_W), oms.reshape(DB * T, MEM_W), gates, xs,
               w_out_b, row(norm_final), DB * T, False)

    return (y_p.reshape(B, S, D_MODEL), y_s.reshape(DB, T, D_MODEL),
            tail(k3), tail(v3),
            st_p[None], zb.reshape(B, S, SHIFT_W)[:, -1][None],
            heads(mk, MEM_W // HEAD_DIM), heads(mv, MEM_W // HEAD_DIM),
            heads(ks.reshape(DB, T, ATT_W), ATT_W // HEAD_DIM), heads(vs.reshape(DB, T, ATT_W), ATT_W // HEAD_DIM),
            st_s[None], zbs3[:, -1][None])
```

```python
import functools

import jax
import jax.numpy as jnp
from jax import lax
from jax.experimental import pallas as pl
from jax.experimental.pallas import tpu as pltpu

F32 = jnp.float32
BF16 = jnp.bfloat16

D_MODEL = 1024
HEAD_DIM = 64
ATT_W = 384
RWKV_W = 384
MEM_W = 256
MIX_W = 1024
LORA_W = 64
SHIFT_W = 3 * RWKV_W + 2 * LORA_W
N_MEM = 256
ROPE_DIM = 16
ROPE_THETA = 500000.0
NORM_EPS = 1e-6
LNX_EPS = 64e-5
DILATIONS = (1, 4, 16)
BAND = 128
ATT_TILE = BAND * max(DILATIONS)
REGROUP = 4
LANES = 128
RWKV_CHUNK = 64
PROJ_TILE = 1024
NEG = -1e30
Q_SCALE = HEAD_DIM ** -0.5 * 1.4426950408889634
VMEM_LIMIT = 56 * 1024 * 1024


def _cparams(sem):
    return pltpu.CompilerParams(dimension_semantics=sem, vmem_limit_bytes=VMEM_LIMIT)


def _dot(a, b):
    return jnp.dot(a.astype(BF16), b.astype(BF16), preferred_element_type=F32)


def _dot_nt(a, b):
    return lax.dot_general(a.astype(BF16), b.astype(BF16), (((1,), (1,)), ((), ())),
                           preferred_element_type=F32)


def _dot_tn(a, b):
    return lax.dot_general(a.astype(BF16), b.astype(BF16), (((0,), (0,)), ((), ())),
                           preferred_element_type=F32)


def _dot_f32(a, b):
    return jnp.dot(a, b, preferred_element_type=F32, precision=lax.Precision.HIGHEST)


def _half0(shape):
    return (lax.broadcasted_iota(jnp.int32, shape, len(shape) - 1) & 64) == 0


def _rope_table_kernel(pos_row_ref, pos_base_ref, invf_ref, cr_ref, sr_ref, cb_ref, sb_ref):
    ang_r = pos_row_ref[...] * invf_ref[...]
    cr_ref[...] = jnp.cos(ang_r)
    sr_ref[...] = jnp.sin(ang_r)
    ang_b = pos_base_ref[...] * invf_ref[...]
    cb_ref[...] = jnp.cos(ang_b)
    sb_ref[...] = jnp.sin(ang_b)


def _rope_tables(pos_row, pos_base, invf):
    tab = lambda n: jax.ShapeDtypeStruct((n, LANES), F32)
    return pl.pallas_call(
        _rope_table_kernel,
        out_shape=[tab(pos_row.shape[0]), tab(pos_row.shape[0]), tab(pos_base.shape[0]), tab(pos_base.shape[0])],
        name="rope_tables",
    )(pos_row, pos_base, invf)


def _proj_kernel(x_ref, cr_ref, sr_ref, cb_ref, sb_ref, g_ref, w_ref, q_ref, k_ref, v_ref, zb_ref, qm_ref, gate_ref,
                 *, split_pairs, tiles_per_seq, base_offset):
    tm = x_ref.shape[0]
    base = base_offset + lax.rem(pl.program_id(0), tiles_per_seq)
    cb = cb_ref[pl.ds(base, 1), :]
    sb = sb_ref[pl.ds(base, 1), :]
    second = (lax.broadcasted_iota(jnp.int32, (1, LANES), 1) & 8) != 0

    def put(ref, p, rows, val):
        if split_pairs:
            ref[p, rows, :] = val
        else:
            ref[rows, p * LANES:(p + 1) * LANES] = val

    nsplit = 2 if tm % 16 == 0 else 1
    hm = tm // nsplit
    for part in range(nsplit):
        rows = pl.ds(part * hm, hm)
        x = x_ref[rows, :]
        ms = jnp.mean(x * x, axis=-1, keepdims=True)
        h = ((x * lax.rsqrt(ms + NORM_EPS)) * g_ref[...]).astype(BF16)
        cr = cr_ref[rows, :]
        sr = sr_ref[rows, :]
        cos = cb * cr - sb * sr
        sin = sb * cr + cb * sr
        sin_up = jnp.where(second, sin, 0.0)
        sin_dn = jnp.where(second, 0.0, -sin)

        def rope(t):
            return t * cos + pltpu.roll(t, 8, 1) * sin_up + pltpu.roll(t, LANES - 8, 1) * sin_dn

        qkv = jnp.dot(h, w_ref[:, :3 * ATT_W], preferred_element_type=F32)
        for p in range(ATT_W // LANES):
            put(q_ref, p, rows, rope(qkv[:, p * LANES:(p + 1) * LANES]) * Q_SCALE)
            put(k_ref, p, rows, rope(qkv[:, ATT_W + p * LANES:ATT_W + (p + 1) * LANES]))
            put(v_ref, p, rows, qkv[:, 2 * ATT_W + p * LANES:2 * ATT_W + (p + 1) * LANES])
        c0 = 3 * ATT_W
        zb_ref[rows, :] = jnp.dot(h, w_ref[:, c0:c0 + SHIFT_W], preferred_element_type=F32)
        c0 += SHIFT_W
        qm_ref[rows, :] = (jnp.dot(h, w_ref[:, c0:c0 + MEM_W], preferred_element_type=F32) * Q_SCALE).astype(
            qm_ref.dtype)
        c0 += MEM_W
        gate_ref[rows, :] = jnp.dot(h, w_ref[:, c0:c0 + MIX_W], preferred_element_type=F32).astype(gate_ref.dtype)


def _proj(x2d, cr, sr, cb, sb, g, w_bf16, tm, split_pairs, tiles_per_seq, base_offset, act):
    rows = x2d.shape[0]
    in_w = w_bf16.shape[1]
    if split_pairs:
        qkv_shape = jax.ShapeDtypeStruct((3, rows, LANES), F32)
        qkv_spec = pl.BlockSpec((3, tm, LANES), lambda i: (0, i, 0))
    else:
        qkv_shape = jax.ShapeDtypeStruct((rows, ATT_W), F32)
        qkv_spec = pl.BlockSpec((tm, ATT_W), lambda i: (i, 0))
    row = lambda w: pl.BlockSpec((tm, w), lambda i: (i, 0))
    full = lambda a: pl.BlockSpec(a.shape, lambda i: (0, 0))
    return pl.pallas_call(
        functools.partial(_proj_kernel, split_pairs=split_pairs, tiles_per_seq=tiles_per_seq,
                          base_offset=base_offset),
        grid=(rows // tm,),
        in_specs=[row(D_MODEL), full(cr), full(sr), full(cb), full(sb), full(g), full(w_bf16)],
        out_specs=[qkv_spec, qkv_spec, qkv_spec, row(SHIFT_W), row(MEM_W), row(MIX_W)],
        out_shape=[qkv_shape, qkv_shape, qkv_shape,
                   jax.ShapeDtypeStruct((rows, SHIFT_W), F32),
                   jax.ShapeDtypeStruct((rows, MEM_W), act),
                   jax.ShapeDtypeStruct((rows, MIX_W), act)],
        compiler_params=_cparams(("arbitrary",)),
        name="proj",
    )(x2d, cr, sr, cb, sb, g, w_bf16)


def _attn_blocks(npairs, blocks):
    h0 = _half0((BAND, LANES))
    h0k = _half0((2 * BAND, LANES))
    units = [(b, p) for b in range(len(blocks)) for p in range(npairs)]
    heads = [(i, hh) for i in range(len(units)) for hh in range(2)]
    q = [blocks[b][0](p) for b, p in units]
    kcat = [jnp.concatenate([blocks[b][1](p), blocks[b][2](p)], axis=0).astype(BF16) for b, p in units]
    vcat = [jnp.concatenate([blocks[b][3](p), blocks[b][4](p)], axis=0) for b, p in units]
    s = [_dot_nt(jnp.where(h0 if hh == 0 else jnp.logical_not(h0), q[i], 0.0), kcat[i]) + blocks[units[i][0]][5]
         for i, hh in heads]
    m = [jnp.max(s[j], axis=-1, keepdims=True) for j in range(len(heads))]
    pexp = [jnp.exp2(s[j] - m[j]).astype(BF16) for j in range(len(heads))]
    res = [jnp.dot(pexp[j], jnp.where(h0k if hh == 0 else jnp.logical_not(h0k), vcat[i], 1.0).astype(BF16),
                   preferred_element_type=F32) for j, (i, hh) in enumerate(heads)]
    lsum = [pltpu.roll(jnp.where(h0, res[2 * i + 1], res[2 * i]), HEAD_DIM, 1) for i in range(len(units))]
    outs = [[] for _ in blocks]
    for i, (b, p) in enumerate(units):
        outs[b].append((jnp.where(h0, res[2 * i], res[2 * i + 1]) / lsum[i],
                        jnp.where(h0, m[2 * i], m[2 * i + 1]) + jnp.log2(lsum[i])))
    return outs


def _attn_kernel(q_ref, kc_ref, kp_ref, vc_ref, vp_ref, o_ref, o_sc, lse_sc, yq_sc, ykv_sc):
    step = pl.program_id(1)
    has_prev = step > 0
    slot = step & 1
    prev_slot = 1 - slot
    qi = lax.broadcasted_iota(jnp.int32, (BAND, 2 * BAND), 0)
    kj = lax.broadcasted_iota(jnp.int32, (BAND, 2 * BAND), 1)
    in_cur = (kj >= BAND) & (kj - BAND <= qi)
    in_prev = (kj < BAND) & (kj >= qi)
    bias_full = jnp.where(in_cur | in_prev, 0.0, NEG).astype(F32)
    bias_first = jnp.where(in_cur | (in_prev & has_prev), 0.0, NEG).astype(F32)
    npairs = q_ref.shape[0]
    nblk = ATT_TILE // BAND
    per_iter = 2
    seg = ATT_TILE // REGROUP

    @pl.when(step == 0)
    def _():
        ykv_sc[prev_slot] = jnp.zeros(ykv_sc.shape[1:], F32)

    def regroup(c, carry):
        per_res = seg // BAND
        res, part = c >> (per_res.bit_length() - 1), c & (per_res - 1)
        src = pl.ds(res + part * (BAND * REGROUP), BAND, stride=REGROUP)
        dst = pl.ds(pl.multiple_of(c * BAND, BAND), BAND)
        for p in range(npairs):
            yq_sc[p, dst, :] = q_ref[p, src, :]
            ykv_sc[slot, 0, p, dst, :] = kc_ref[p, src, :]
            ykv_sc[slot, 1, p, dst, :] = vc_ref[p, src, :]
        return carry

    lax.fori_loop(0, nblk, regroup, 0)

    def desc(d, u, first):
        span = BAND * d
        bias = bias_first if first else bias_full
        if d == 1:
            qs = u * BAND if isinstance(u, int) else pl.multiple_of(u * BAND, BAND)
            rows = pl.ds(qs, BAND)
            prev = (lambda ref_c, ref_p: (lambda p: ref_p[p])) if first else (
                lambda ref_c, ref_p: (lambda p: ref_c[p, pl.ds(qs - BAND, BAND), :]))
            return (lambda p: q_ref[p, rows, :], prev(kc_ref, kp_ref), lambda p: kc_ref[p, rows, :],
                    prev(vc_ref, vp_ref), lambda p: vc_ref[p, rows, :], bias), qs
        assert d % REGROUP == 0
        sub = d // REGROUP
        sp, r = u >> (d.bit_length() - 1), u & (d - 1)
        qs = sp * span + r
        start = (r & (REGROUP - 1)) * seg + sp * (span // REGROUP) + (r >> (REGROUP.bit_length() - 1))
        rows = pl.ds(start, BAND, stride=sub)
        if first:
            prow, pslot = pl.ds(start + seg - span // REGROUP, BAND, stride=sub), prev_slot
        else:
            prow, pslot = pl.ds(start - span // REGROUP, BAND, stride=sub), slot
        return (lambda p: yq_sc[p, rows, :],
                lambda p: ykv_sc[pslot, 0, p, prow, :], lambda p: ykv_sc[slot, 0, p, rows, :],
                lambda p: ykv_sc[pslot, 1, p, prow, :], lambda p: ykv_sc[slot, 1, p, rows, :], bias), qs

    def merge(outs, start):
        row = pl.ds(start, BAND)
        for p, (o1, lse1) in enumerate(outs):
            os_ = [o1] + [o_sc[j, p, row, :] for j in range(len(DILATIONS) - 1)]
            ls_ = [lse1] + [lse_sc[j, p, row, :] for j in range(len(DILATIONS) - 1)]
            top = functools.reduce(jnp.maximum, ls_)
            ws = [jnp.exp2(l - top) for l in ls_]
            num = functools.reduce(lambda a, b: a + b, [w * o for w, o in zip(ws, os_)])
            o_ref[p, row, :] = (num / functools.reduce(lambda a, b: a + b, ws)).astype(o_ref.dtype)

    def run(j, d, descs):
        for (_, qs), outs in zip(descs, _attn_blocks(npairs, [blk for blk, _ in descs])):
            if j < 0:
                merge(outs, qs)
            else:
                for p, (o, lse) in enumerate(outs):
                    o_sc[j, p, pl.ds(qs, BAND, stride=d), :] = o
                    lse_sc[j, p, pl.ds(qs, BAND, stride=d), :] = lse

    def sweep(j, d):
        n_first = d
        several = lambda first: lambda i, c: (run(j, d, [desc(d, i * per_iter + k, first) for k in range(per_iter)]),
                                              c)[1]
        if n_first % per_iter == 0:
            lax.fori_loop(0, n_first // per_iter, several(True), 0)
            lo = n_first
        else:
            assert n_first == 1 and per_iter == 2
            run(j, d, [desc(d, 0, True), desc(d, 1, False)])
            lo = per_iter
        if lo < nblk:
            lax.fori_loop(lo // per_iter, nblk // per_iter, several(False), 0)

    for j, d in enumerate(DILATIONS[1:]):
        sweep(j, d)
    sweep(-1, DILATIONS[0])


def _attn_prompt(q3, k3, v3, batch, seq):
    nt = seq // ATT_TILE
    per_tile = ATT_TILE // BAND
    cur = pl.BlockSpec((3, ATT_TILE, LANES), lambda b, i: (0, b * nt + i, 0))
    prev = pl.BlockSpec((3, BAND, LANES), lambda b, i: (0, jnp.maximum((b * nt + i) * per_tile - 1, 0), 0))
    return pl.pallas_call(
        _attn_kernel,
        grid=(batch, nt),
        in_specs=[cur, cur, prev, cur, prev],
        out_specs=cur,
        out_shape=jax.ShapeDtypeStruct(q3.shape, BF16),
        scratch_shapes=[pltpu.VMEM((len(DILATIONS) - 1, 3, ATT_TILE, LANES), F32),
                        pltpu.VMEM((len(DILATIONS) - 1, 3, ATT_TILE, LANES), F32),
                        pltpu.VMEM((3, ATT_TILE, LANES), F32),
                        pltpu.VMEM((2, 2, 3, ATT_TILE, LANES), F32)],
        compiler_params=_cparams(("arbitrary", "arbitrary")),
        name="attn_prompt",
    )(q3, k3, k3, v3, v3)


def _tails_kernel(k_ref, v_ref, kt_ref, vt_ref):
    kt_ref[0] = k_ref[0].T
    vt_ref[0] = v_ref[0].T


def _tails(k3, v3, batch, seq, win):
    assert seq % win == 0
    nt = seq // win
    src = pl.BlockSpec((1, win, LANES), lambda b, p: (p, b * nt + nt - 1, 0))
    dst = pl.BlockSpec((1, LANES, win), lambda b, p: (b, p, 0))
    out = jax.ShapeDtypeStruct((batch, ATT_W, win), F32)
    return pl.pallas_call(
        _tails_kernel,
        grid=(batch, ATT_W // LANES),
        in_specs=[src, src],
        out_specs=[dst, dst],
        out_shape=[out, out],
        compiler_params=_cparams(("arbitrary", "arbitrary")),
        name="tails",
    )(k3, v3)


def _attn_sample_kernel(q_ref, kn_ref, vn_ref, kc_ref, vc_ref, o_ref, *, n_past, t_new, nb):
    nh = ATT_W // HEAD_DIM
    head_of_lane = lax.broadcasted_iota(jnp.int32, (t_new, ATT_W), 1) >> 6
    seqs = range(nb)
    qst = [jnp.concatenate([jnp.where(head_of_lane == h, q_ref[b], 0.0) for h in range(nh)], axis=0) for b in seqs]

    def count(delta):
        c = jnp.zeros(delta.shape, F32)
        for d in DILATIONS:
            ok = (delta >= 0) & (delta <= BAND * d) & ((delta & (d - 1)) == 0)
            c = c + jnp.where(ok, 1.0, 0.0)
        return c

    rows = nh * t_new
    t_past = lax.broadcasted_iota(jnp.int32, (rows, n_past), 0) & (t_new - 1)
    cnt_past = count(n_past + t_past - lax.broadcasted_iota(jnp.int32, (rows, n_past), 1))
    t_n = lax.broadcasted_iota(jnp.int32, (rows, t_new), 0) & (t_new - 1)
    cnt_new = count(t_n - lax.broadcasted_iota(jnp.int32, (rows, t_new), 1))

    s_past = [jnp.where(cnt_past > 0, _dot(qst[b], kc_ref[b]), NEG) for b in seqs]
    s_new = [jnp.where(cnt_new > 0, _dot_nt(qst[b], kn_ref[b]), NEG) for b in seqs]
    m = [jnp.maximum(jnp.max(s_past[b], axis=-1, keepdims=True), jnp.max(s_new[b], axis=-1, keepdims=True))
         for b in seqs]
    p_past = [cnt_past * jnp.exp2(s_past[b] - m[b]) for b in seqs]
    p_new = [cnt_new * jnp.exp2(s_new[b] - m[b]) for b in seqs]
    l = [jnp.sum(p_past[b], axis=-1, keepdims=True) + jnp.sum(p_new[b], axis=-1, keepdims=True) for b in seqs]
    o = [(_dot_nt(p_past[b], vc_ref[b]) + _dot(p_new[b], vn_ref[b])) / l[b] for b in seqs]
    for b in seqs:
        out = jnp.zeros((t_new, ATT_W), F32)
        for h in range(nh):
            out = jnp.where(head_of_lane == h, o[b][h * t_new:(h + 1) * t_new, :], out)
        o_ref[b] = out.astype(o_ref.dtype)


def _attn_sample(q, k_new, v_new, k_cache, v_cache, nb):
    db, t_new, _ = q.shape
    n_past = k_cache.shape[2]
    new = pl.BlockSpec((nb, t_new, ATT_W), lambda b: (b, 0, 0))
    cache = pl.BlockSpec((nb, ATT_W, n_past), lambda b: (b, 0, 0))
    return pl.pallas_call(
        functools.partial(_attn_sample_kernel, n_past=n_past, t_new=t_new, nb=nb),
        grid=(db // nb,),
        in_specs=[new, new, new, cache, cache],
        out_specs=new,
        out_shape=jax.ShapeDtypeStruct(q.shape, F32),
        compiler_params=_cparams(("arbitrary",)),
        name="attn_sample",
    )(q, k_new, v_new, k_cache, v_cache)


def _seg_sum(x, h0):
    s0 = jnp.sum(jnp.where(h0, x, 0.0), axis=-1, keepdims=True)
    s1 = jnp.sum(jnp.where(h0, 0.0, x), axis=-1, keepdims=True)
    return jnp.where(h0, s0, s1)


def _split_bf16(x):
    hi = x.astype(BF16)
    return hi, (x - hi.astype(F32)).astype(BF16)


def _rwkv_kernel(z_ref, sh0_ref, st0_ref, mu_ref, w0_ref, a0_ref, whi_ref, wlo_ref, kk_ref, ka_ref, rk_ref,
                 lng_ref, lnb_ref, o_ref, st_ref, st_sc, prev_sc, *, nb, groups, n_valid):
    C = RWKV_CHUNK
    R = groups * C
    npairs = RWKV_W // LANES

    @pl.when(pl.program_id(1) == 0)
    def _():
        zero = jnp.zeros((HEAD_DIM, HEAD_DIM), F32)
        for b in range(nb):
            for p in range(npairs):
                st_sc[b, p] = jnp.concatenate(
                    [jnp.concatenate([st0_ref[b, 2 * p], zero], axis=1),
                     jnp.concatenate([zero, st0_ref[b, 2 * p + 1]], axis=1)], axis=0)
        prev_sc[...] = sh0_ref[...]

    si = lax.broadcasted_iota(jnp.int32, (2 * C, 2 * C), 0)
    sj = lax.broadcasted_iota(jnp.int32, (2 * C, 2 * C), 1)
    same_head = (si >= C) == (sj >= C)
    ti2 = si & (C - 1)
    tj2 = sj & (C - 1)
    incl = same_head & (tj2 <= ti2)
    strict = same_head & (tj2 < ti2)
    eye = jnp.where(si == sj, 1.0, 0.0).astype(F32)
    levels = []
    s = 2
    while s < min(C, n_valid):
        sh = s.bit_length() - 1
        levels.append(((ti2 >> (sh + 1)) == (tj2 >> (sh + 1))) & (((ti2 >> sh) & 1) == 1) & (((tj2 >> sh) & 1) == 0))
        s *= 2
    first_level = (ti2 >> 1) == (tj2 >> 1)

    ri = lax.broadcasted_iota(jnp.int32, (C, C), 0)
    rj = lax.broadcasted_iota(jnp.int32, (C, C), 1)
    tri = jnp.where(rj <= ri, 1.0, 0.0).astype(BF16)
    rowid = lax.broadcasted_iota(jnp.int32, (C, 1), 0)
    h0 = _half0((C, LANES))
    zero_blk = jnp.zeros((2 * C, LANES), BF16)
    dot = functools.partial(jnp.dot, preferred_element_type=F32)
    pre = {}

    def stack(x):
        return jnp.concatenate([jnp.where(h0, x, 0.0), jnp.where(h0, 0.0, x)], axis=0)

    def prep(b, g):
        if n_valid < C:
            z = jnp.concatenate([z_ref[b], jnp.zeros((C - n_valid, SHIFT_W), F32)], axis=0)
        else:
            z = z_ref[b, g * C:(g + 1) * C, :]
        before = prev_sc[b] if g == 0 else z_ref[b, pl.ds(g * C - 1, 1), :]
        z_prev = jnp.where(rowid == 0, before, pltpu.roll(z, 1, 0))
        zs = z + (z_prev - z) * mu_ref[...]
        valid = rowid < (n_valid - g * C)
        if n_valid < (g + 1) * C:
            zs = jnp.where(valid, zs, 0.0)
        lat = zs[:, 3 * RWKV_W:]
        lat_hi, lat_lo = _split_bf16(jnp.where(_half0(lat.shape), jnp.tanh(lat), lat))
        lora = dot(lat_hi, whi_ref[...]) + dot(lat_lo, whi_ref[...]) + dot(lat_hi, wlo_ref[...])
        w = -jax.nn.softplus(-(w0_ref[...] + lora[:, :RWKV_W])) - 0.5
        lw = -jnp.exp(w)
        if n_valid < (g + 1) * C:
            lw = jnp.where(valid, lw, 0.0)
        a = jax.nn.sigmoid(a0_ref[...] + lora[:, RWKV_W:])
        lw_hi, lw_lo = _split_bf16(lw)
        cum = dot(tri, lw_hi) + dot(tri, lw_lo)
        for p in range(npairs):
            sl = slice(p * LANES, (p + 1) * LANES)
            r = zs[:, sl]
            k = zs[:, RWKV_W + p * LANES:RWKV_W + (p + 1) * LANES]
            v = zs[:, 2 * RWKV_W + p * LANES:2 * RWKV_W + (p + 1) * LANES]
            ap = a[:, sl]
            kk = k * kk_ref[:, sl]
            kk = kk * jnp.minimum(lax.rsqrt(_seg_sum(kk * kk, h0)), 1e12)
            k = k * (1.0 + (ap - 1.0) * ka_ref[:, sl])
            bb = kk * ap
            L = cum[:, sl]
            l_end = L[C - 1:C, :]
            e_out = jnp.exp(-L)
            e_end = jnp.exp(l_end - L)
            rt = stack(r * jnp.exp(L))
            kkh = stack(kk * jnp.exp(L - lw[:, sl])).astype(BF16)
            left = jnp.concatenate([rt.astype(BF16), kkh], axis=0)
            right = jnp.concatenate([stack(k * e_out), stack(-(bb * e_out))], axis=0).astype(BF16)
            ends = jnp.concatenate([stack(k * e_end), stack(-(bb * e_end))], axis=0).astype(BF16)
            pre[(b, g, p)] = dict(left=left, right=right, ends=ends, rt=rt, kkh=kkh, v2=stack(v).astype(BF16),
                                  decay=jnp.exp(l_end), bonus=_seg_sum(r * k * rk_ref[:, sl], h0) * v)

    def transition_stages(chains):
        def scores():
            for c in chains:
                d = pre[c]
                aa = lax.dot_general(d["left"], d["right"], (((1,), (1,)), ((), ())), preferred_element_type=F32)
                d["a_r"] = jnp.concatenate([jnp.where(incl, aa[:2 * C, :2 * C], 0.0),
                                            jnp.where(incl, aa[:2 * C, 2 * C:], 0.0)], axis=1).astype(BF16)
                d["a_kk"] = jnp.where(strict, aa[2 * C:, :2 * C], 0.0).astype(BF16)
                n_kb = jnp.where(strict, aa[2 * C:, 2 * C:], 0.0)
                d["n_kb"] = n_kb.astype(BF16)
                d["t"] = (eye + jnp.where(first_level, n_kb, 0.0)).astype(BF16)

        def odd_rows(x, s):
            return jnp.concatenate([x[i:i + s] for i in range(s, 2 * C, 2 * s)], axis=0) if s >= 16 else x

        def level_a(off, s):
            for c in chains:
                pre[c]["tn"] = dot(odd_rows(pre[c]["t"], s), jnp.where(off, pre[c]["n_kb"], 0.0)).astype(BF16)

        def level_b(s):
            for c in chains:
                t = pre[c]["t"]
                new = odd_rows(t, s) + dot(pre[c]["tn"], t).astype(BF16)
                if s >= 16:
                    parts = []
                    for j, i in enumerate(range(0, 2 * C, 2 * s)):
                        parts += [t[i:i + s], new[j * s:(j + 1) * s]]
                    new = jnp.concatenate(parts, axis=0)
                pre[c]["t"] = new

        def akv():
            for c in chains:
                pre[c]["akv"] = dot(pre[c]["a_kk"], pre[c]["v2"]).astype(BF16)

        def solve():
            for c in chains:
                d = pre[c]
                tr = dot(d["t"], jnp.concatenate([d["kkh"], d["akv"]], axis=1)).astype(BF16)
                kq, u0 = tr[:, :LANES], tr[:, LANES:]
                d["big"] = jnp.concatenate([jnp.concatenate([d["v2"], zero_blk], axis=1),
                                            jnp.concatenate([u0, kq], axis=1)], axis=0)
        def readout():
            for c in chains:
                d = pre[c]
                yr = dot(d["a_r"], d["big"])
                d["y0"] = yr[:, :LANES]
                d["rq"] = (d["rt"] + yr[:, LANES:]).astype(BF16)
        def update():
            for c in chains:
                d = pre[c]
                dg = lax.dot_general(d["big"], d["ends"], (((0,), (0,)), ((), ())), preferred_element_type=F32)
                d["dd"] = dg[:LANES]
                d["gm"] = dg[LANES:].astype(BF16)
        stages = [scores]
        for i, off in enumerate(levels):
            stages += [functools.partial(level_a, off, 2 << i), functools.partial(level_b, 2 << i)]
        return stages + [akv, solve, readout, update]

    state = {(b, p): st_sc[b, p] for b in range(nb) for p in range(npairs)}

    def carried(chains):
        for (b, g, p) in chains:
            d = pre[(b, g, p)]
            st = state[(b, p)]
            st_b = st.astype(BF16)
            d["y2"] = lax.dot_general(d["rq"], st_b, (((1,), (1,)), ((), ())), preferred_element_type=F32) + d["y0"]
            state[(b, p)] = st * d["decay"] + dot(st_b, d["gm"]) + d["dd"]
        for (b, g, p) in chains:
            d = pre.pop((b, g, p))
            y2 = d["y2"]
            y = y2[:C] + y2[C:]
            sl = slice(p * LANES, (p + 1) * LANES)
            mean = _seg_sum(y, h0) * (1.0 / HEAD_DIM)
            yc = y - mean
            var = _seg_sum(yc * yc, h0) * (1.0 / HEAD_DIM)
            yn = yc * lax.rsqrt(var + LNX_EPS) * lng_ref[:, sl] + lnb_ref[:, sl]
            rows_out = min(C, n_valid - g * C)
            o_ref[b, g * C:g * C + rows_out, sl] = (yn + d["bonus"])[:rows_out].astype(o_ref.dtype)

    items = [(b, g) for g in range(groups) for b in range(nb)]
    for item in items:
        prep(*item)
    chains = [(b, g, p) for (b, g) in items for p in range(npairs)]
    for stage in transition_stages(chains):
        stage()
    carried(chains)
    for b in range(nb):
        prev_sc[b] = z_ref[b, pl.ds(min(R, n_valid) - 1, 1), :]
    for (b, p), st in state.items():
        st_sc[b, p] = st

    @pl.when(pl.program_id(1) == pl.num_programs(1) - 1)
    def _():
        for b in range(nb):
            for p in range(npairs):
                st_ref[b, 2 * p] = st_sc[b, p, :HEAD_DIM, :HEAD_DIM]
                st_ref[b, 2 * p + 1] = st_sc[b, p, HEAD_DIM:, HEAD_DIM:]


def _rwkv(z, shift0, st0, mu, w0, a0, w_hi, w_lo, k_k, k_a, r_k, ln_g, ln_b, nb, groups, act):
    b, s, _ = z.shape
    rows = groups * RWKV_CHUNK
    if s % rows:
        assert groups == 1 and s < rows and s % 8 == 0
        rows = s
    n_valid = rows
    vec = lambda w: pl.BlockSpec((1, w), lambda i, j: (0, 0))
    st_spec = pl.BlockSpec((nb, 2 * 3, HEAD_DIM, HEAD_DIM), lambda i, j: (i, 0, 0, 0))
    wspec = pl.BlockSpec((LANES, 2 * RWKV_W), lambda i, j: (0, 0))
    return pl.pallas_call(
        functools.partial(_rwkv_kernel, nb=nb, groups=groups, n_valid=n_valid),
        grid=(b // nb, s // rows),
        in_specs=[pl.BlockSpec((nb, rows, SHIFT_W), lambda i, j: (i, j, 0)),
                  pl.BlockSpec((nb, 1, SHIFT_W), lambda i, j: (i, 0, 0)),
                  st_spec,
                  vec(SHIFT_W), vec(RWKV_W), vec(RWKV_W), wspec, wspec,
                  vec(RWKV_W), vec(RWKV_W), vec(RWKV_W), vec(RWKV_W), vec(RWKV_W)],
        out_specs=[pl.BlockSpec((nb, rows, RWKV_W), lambda i, j: (i, j, 0)), st_spec],
        out_shape=[jax.ShapeDtypeStruct((b, s, RWKV_W), act),
                   jax.ShapeDtypeStruct((b, 2 * 3, HEAD_DIM, HEAD_DIM), F32)],
        scratch_shapes=[pltpu.VMEM((nb, 3, LANES, LANES), F32), pltpu.VMEM((nb, 1, SHIFT_W), F32)],
        compiler_params=_cparams(("arbitrary", "arbitrary")),
        name="rwkv",
    )(z, shift0, st0, mu, w0, a0, w_hi, w_lo, k_k, k_a, r_k, ln_g, ln_b)


def _memkv_kernel(x_ref, g_ref, w_ref, k_ref, v_ref):
    x = x_ref[...]
    ms = jnp.mean(x * x, axis=-1, keepdims=True)
    h = ((x * lax.rsqrt(ms + NORM_EPS)) * g_ref[...]).astype(BF16)
    k = jnp.dot(h, w_ref[:, :MEM_W], preferred_element_type=F32)
    v = jnp.dot(h, w_ref[:, MEM_W:], preferred_element_type=F32)
    for b in range(k_ref.shape[0]):
        k_ref[b] = k[b * N_MEM:(b + 1) * N_MEM].T
        v_ref[b] = v[b * N_MEM:(b + 1) * N_MEM].T


def _memkv(mem2d, g, w_bf16):
    rows = mem2d.shape[0]
    out = jax.ShapeDtypeStruct((rows // N_MEM, MEM_W, N_MEM), F32)
    return pl.pallas_call(
        _memkv_kernel,
        out_shape=[out, out],
        compiler_params=pltpu.CompilerParams(vmem_limit_bytes=VMEM_LIMIT),
        name="memkv",
    )(mem2d, g, w_bf16)


def _memattn_kernel(q_ref, mk_ref, mv_ref, o_ref, *, nb):
    head_of_lane = lax.broadcasted_iota(jnp.int32, q_ref.shape[1:], 1) >> 6
    qs = [q_ref[b] for b in range(nb)]
    mks = [mk_ref[b].astype(BF16) for b in range(nb)]
    mvs = [mv_ref[b].astype(BF16) for b in range(nb)]
    outs = [jnp.zeros(q_ref.shape[1:], F32) for _ in range(nb)]
    for h in range(MEM_W // HEAD_DIM):
        mine = head_of_lane == h
        s = [_dot(jnp.where(mine, qs[b], 0.0), mks[b]) for b in range(nb)]
        pexp = [jnp.exp2(s[b] - jnp.max(s[b], axis=-1, keepdims=True)) for b in range(nb)]
        l = [jnp.sum(pexp[b], axis=-1, keepdims=True) for b in range(nb)]
        outs = [jnp.where(mine, _dot_nt(pexp[b], mvs[b]) / l[b], outs[b]) for b in range(nb)]
    for b in range(nb):
        o_ref[b] = outs[b].astype(o_ref.dtype)


def _memattn(q, mk, mv, tm, nb, act):
    b, s, _ = q.shape
    qspec = pl.BlockSpec((nb, tm, MEM_W), lambda i, j: (i, j, 0))
    mspec = pl.BlockSpec((nb,) + mk.shape[1:], lambda i, j: (i, 0, 0))
    return pl.pallas_call(
        functools.partial(_memattn_kernel, nb=nb),
        grid=(b // nb, s // tm),
        in_specs=[qspec, mspec, mspec],
        out_specs=qspec,
        out_shape=jax.ShapeDtypeStruct(q.shape, act),
        compiler_params=_cparams(("arbitrary", "arbitrary")),
        name="memattn",
    )(q, mk, mv)


def _out_kernel(oa_ref, ob_ref, om_ref, gate_ref, x_ref, w_ref, g_ref, y_ref, *, split_pairs):
    acc = x_ref[...]

    def add(acc, o, c0):
        width = o.shape[1]
        gate = gate_ref[:, c0:c0 + width]
        return acc + _dot(o * (gate * jax.nn.sigmoid(gate)), w_ref[c0:c0 + width, :])

    if split_pairs:
        for p in range(ATT_W // LANES):
            acc = add(acc, oa_ref[p], p * LANES)
    else:
        acc = add(acc, oa_ref[...], 0)
    acc = add(acc, ob_ref[...], ATT_W)
    acc = add(acc, om_ref[...], ATT_W + RWKV_W)
    ms = jnp.mean(acc * acc, axis=-1, keepdims=True)
    y_ref[...] = (acc * lax.rsqrt(ms + NORM_EPS)) * g_ref[...]


def _out(oa, ob, om, gate, x2d, w_bf16, g, tm, split_pairs):
    rows = x2d.shape[0]
    row = lambda w: pl.BlockSpec((tm, w), lambda i: (i, 0))
    oa_spec = pl.BlockSpec((3, tm, LANES), lambda i: (0, i, 0)) if split_pairs else row(ATT_W)
    return pl.pallas_call(
        functools.partial(_out_kernel, split_pairs=split_pairs),
        grid=(rows // tm,),
        in_specs=[oa_spec, row(RWKV_W), row(MEM_W), row(MIX_W), row(D_MODEL),
                  pl.BlockSpec((MIX_W, D_MODEL), lambda i: (0, 0)),
                  pl.BlockSpec((1, D_MODEL), lambda i: (0, 0))],
        out_specs=row(D_MODEL),
        out_shape=jax.ShapeDtypeStruct((rows, D_MODEL), F32),
        compiler_params=_cparams(("arbitrary",)),
        name="outproj",
    )(oa, ob, om, gate, x2d, w_bf16, g)


def kernel(x_prompt, x_sample, cache_win_k, cache_win_v, state_rwkv, state_rwkv_shift, cache_mem_k, cache_mem_v, mem_prompt, norm_in, w_in, rwkv_mu, rwkv_w0, rwkv_w2, rwkv_a0, rwkv_a2, rwkv_k_k, rwkv_k_a, rwkv_r_k, rwkv_lnx_g, rwkv_lnx_b, norm_mem, w_mem_kv, w_out, norm_final):
    B, S, _ = x_prompt.shape
    DB, T, _ = x_sample.shape
    depth = w_in.shape[0]
    assert depth == 1 and S % ATT_TILE == 0 and cache_win_k.shape[2] == max(DILATIONS) * BAND
    l = 0
    past_len = S

    half = ROPE_DIM // 2
    lane = jnp.arange(LANES)
    inv_freq = ROPE_THETA ** (-(lane % half).astype(F32) / half)
    invf = jnp.where((lane % HEAD_DIM) < ROPE_DIM, inv_freq, 0.0).reshape(1, LANES)

    row = lambda t: t.reshape(1, -1)
    w_in_b = w_in[l].astype(BF16)
    w_out_b = w_out[l].astype(BF16)
    w_kv_b = w_mem_kv[l].astype(BF16)
    zero = jnp.zeros((LORA_W, RWKV_W), F32)
    w2a2 = jnp.concatenate([jnp.concatenate([rwkv_w2[l], zero], axis=1),
                            jnp.concatenate([zero, rwkv_a2[l]], axis=1)], axis=0)
    w2a2_hi = w2a2.astype(BF16)
    w2a2_lo = (w2a2 - w2a2_hi.astype(F32)).astype(BF16)
    rw = (row(rwkv_mu[l]), row(rwkv_w0[l]), row(rwkv_a0[l]), w2a2_hi, w2a2_lo, row(rwkv_k_k[l]),
          row(rwkv_k_a[l]), row(rwkv_r_k[l]), row(rwkv_lnx_g[l]), row(rwkv_lnx_b[l]))

    tiles = S // PROJ_TILE
    n_base = -(-(tiles + 1) // 8) * 8
    pos_base = jnp.where(jnp.arange(n_base) < tiles, jnp.arange(n_base) * PROJ_TILE, past_len)
    cr, sr, cb, sb = _rope_tables(jnp.arange(PROJ_TILE, dtype=F32).reshape(-1, 1),
                                  pos_base.astype(F32).reshape(-1, 1), invf)

    xp = x_prompt.reshape(B * S, D_MODEL)
    q3, k3, v3, zb, qm, gate = _proj(xp, cr, sr, cb, sb, row(norm_in[l]), w_in_b, PROJ_TILE, True, tiles, 0, BF16)
    oa3 = _attn_prompt(q3, k3, v3, B, S)
    ob, st_p = _rwkv(zb.reshape(B, S, SHIFT_W), jnp.zeros((B, 1, SHIFT_W), F32),
                     jnp.zeros((B, RWKV_W // HEAD_DIM, HEAD_DIM, HEAD_DIM), F32), *rw, nb=B, groups=2, act=BF16)
    mk, mv = _memkv(mem_prompt.reshape(B * N_MEM, D_MODEL), row(norm_mem[l]), w_kv_b)
    om = _memattn(qm.reshape(B, S, MEM_W), mk, mv, 1024, 1, BF16)
    y_p = _out(oa3, ob.reshape(B * S, RWKV_W), om.reshape(B * S, MEM_W), gate, xp, w_out_b,
               row(norm_final), 1024, True)

    win = min(max(DILATIONS) * BAND, S)
    kt, vt = _tails(k3, v3, B, S, win)
    tail = lambda t: jnp.transpose(t.reshape(B, ATT_W // HEAD_DIM, HEAD_DIM, win), (0, 3, 1, 2))[None]
    heads = lambda t, n: t.reshape(1, t.shape[0], t.shape[1], n, HEAD_DIM)
    mem_heads = lambda t: jnp.transpose(t.reshape(B, MEM_W // HEAD_DIM, HEAD_DIM, N_MEM), (0, 3, 1, 2))[None]

    xs = x_sample.reshape(DB * T, D_MODEL)
    qs, ks, vs, zbs, qms, gates = _proj(xs, jnp.tile(cr[:T], (DB, 1)), jnp.tile(sr[:T], (DB, 1)), cb, sb,
                                        row(norm_in[l]), w_in_b, DB * T, False, 1, tiles, F32)
    n_past = cache_win_k.shape[2]
    minor = lambda c: jnp.transpose(c, (0, 2, 3, 1)).reshape(c.shape[0], c.shape[2] * c.shape[3], c.shape[1])
    oas = _attn_sample(qs.reshape(DB, T, ATT_W), ks.reshape(DB, T, ATT_W), vs.reshape(DB, T, ATT_W),
                       minor(cache_win_k[l]), minor(cache_win_v[l]), 2)
    zbs3 = zbs.reshape(DB, T, SHIFT_W)
    obs, st_s = _rwkv(zbs3, state_rwkv_shift[l].reshape(DB, 1, SHIFT_W), state_rwkv[l],
                      *rw, nb=4, groups=1, act=F32)
    oms = _memattn(qms.reshape(DB, T, MEM_W), minor(cache_mem_k[l]), minor(cache_mem_v[l]), T, 8, F32)
    y_s = _out(oas.reshape(DB * T, ATT_W), obs.reshape(DB * T, RWKV_W), oms.reshape(DB * T, MEM_W), gates, xs,
               w_out_b, row(norm_final), DB * T, False)

    return (y_p.reshape(B, S, D_MODEL), y_s.reshape(DB, T, D_MODEL),
            tail(kt), tail(vt),
            st_p[None], zb.reshape(B, S, SHIFT_W)[:, -1][None],
            mem_heads(mk), mem_heads(mv),
            heads(ks.reshape(DB, T, ATT_W), ATT_W // HEAD_DIM), heads(vs.reshape(DB, T, ATT_W), ATT_W // HEAD_DIM),
            st_s[None], zbs3[:, -1][None])
```

```python
import functools

import jax
import jax.numpy as jnp
from jax import lax
from jax.experimental import pallas as pl
from jax.experimental.pallas import tpu as pltpu

F32 = jnp.float32
BF16 = jnp.bfloat16

D_MODEL = 1024
HEAD_DIM = 64
ATT_W = 384
RWKV_W = 384
MEM_W = 256
MIX_W = 1024
LORA_W = 64
SHIFT_W = 3 * RWKV_W + 2 * LORA_W
N_MEM = 256
ROPE_DIM = 16
ROPE_THETA = 500000.0
NORM_EPS = 1e-6
LNX_EPS = 64e-5
DILATIONS = (1, 4, 16)
BAND = 128
ATT_TILE = BAND * max(DILATIONS)
REGROUP = 4
LANES = 128
RWKV_CHUNK = 64
PROJ_TILE = 1024
NEG = -1e30
Q_SCALE = HEAD_DIM ** -0.5 * 1.4426950408889634
VMEM_LIMIT = 56 * 1024 * 1024


def _cparams(sem):
    return pltpu.CompilerParams(dimension_semantics=sem, vmem_limit_bytes=VMEM_LIMIT)


def _dot(a, b):
    return jnp.dot(a.astype(BF16), b.astype(BF16), preferred_element_type=F32)


def _dot_nt(a, b):
    return lax.dot_general(a.astype(BF16), b.astype(BF16), (((1,), (1,)), ((), ())),
                           preferred_element_type=F32)


def _dot_tn(a, b):
    return lax.dot_general(a.astype(BF16), b.astype(BF16), (((0,), (0,)), ((), ())),
                           preferred_element_type=F32)


def _dot_f32(a, b):
    return jnp.dot(a, b, preferred_element_type=F32, precision=lax.Precision.HIGHEST)


def _half0(shape):
    return (lax.broadcasted_iota(jnp.int32, shape, len(shape) - 1) & 64) == 0


def _rope_table_kernel(pos_row_ref, pos_base_ref, invf_ref, cr_ref, sr_ref, cb_ref, sb_ref):
    ang_r = pos_row_ref[...] * invf_ref[...]
    cr_ref[...] = jnp.cos(ang_r)
    sr_ref[...] = jnp.sin(ang_r)
    ang_b = pos_base_ref[...] * invf_ref[...]
    cb_ref[...] = jnp.cos(ang_b)
    sb_ref[...] = jnp.sin(ang_b)


def _rope_tables(pos_row, pos_base, invf):
    tab = lambda n: jax.ShapeDtypeStruct((n, LANES), F32)
    return pl.pallas_call(
        _rope_table_kernel,
        out_shape=[tab(pos_row.shape[0]), tab(pos_row.shape[0]), tab(pos_base.shape[0]), tab(pos_base.shape[0])],
        name="rope_tables",
    )(pos_row, pos_base, invf)


def _proj_kernel(x_ref, cr_ref, sr_ref, cb_ref, sb_ref, g_ref, w_ref, q_ref, k_ref, v_ref, zb_ref, qm_ref, gate_ref,
                 *, split_pairs, tiles_per_seq, base_offset):
    tm = x_ref.shape[0]
    base = base_offset + lax.rem(pl.program_id(0), tiles_per_seq)
    cb = cb_ref[pl.ds(base, 1), :]
    sb = sb_ref[pl.ds(base, 1), :]
    second = (lax.broadcasted_iota(jnp.int32, (1, LANES), 1) & 8) != 0

    def put(ref, p, rows, val):
        if split_pairs:
            ref[p, rows, :] = val
        else:
            ref[rows, p * LANES:(p + 1) * LANES] = val

    nsplit = 2 if tm % 16 == 0 else 1
    hm = tm // nsplit
    for part in range(nsplit):
        rows = pl.ds(part * hm, hm)
        x = x_ref[rows, :]
        ms = jnp.mean(x * x, axis=-1, keepdims=True)
        h = ((x * lax.rsqrt(ms + NORM_EPS)) * g_ref[...]).astype(BF16)
        cr = cr_ref[rows, :]
        sr = sr_ref[rows, :]
        cos = cb * cr - sb * sr
        sin = sb * cr + cb * sr
        sin_up = jnp.where(second, sin, 0.0)
        sin_dn = jnp.where(second, 0.0, -sin)

        def rope(t):
            return t * cos + pltpu.roll(t, 8, 1) * sin_up + pltpu.roll(t, LANES - 8, 1) * sin_dn

        qkv = jnp.dot(h, w_ref[:, :3 * ATT_W], preferred_element_type=F32)
        for p in range(ATT_W // LANES):
            put(q_ref, p, rows, rope(qkv[:, p * LANES:(p + 1) * LANES]) * Q_SCALE)
            put(k_ref, p, rows, rope(qkv[:, ATT_W + p * LANES:ATT_W + (p + 1) * LANES]))
            put(v_ref, p, rows, qkv[:, 2 * ATT_W + p * LANES:2 * ATT_W + (p + 1) * LANES])
        c0 = 3 * ATT_W
        zb_ref[rows, :] = jnp.dot(h, w_ref[:, c0:c0 + SHIFT_W], preferred_element_type=F32)
        c0 += SHIFT_W
        qm_ref[rows, :] = (jnp.dot(h, w_ref[:, c0:c0 + MEM_W], preferred_element_type=F32) * Q_SCALE).astype(
            qm_ref.dtype)
        c0 += MEM_W
        gate_ref[rows, :] = jnp.dot(h, w_ref[:, c0:c0 + MIX_W], preferred_element_type=F32).astype(gate_ref.dtype)


def _proj(x2d, cr, sr, cb, sb, g, w_bf16, tm, split_pairs, tiles_per_seq, base_offset, act):
    rows = x2d.shape[0]
    in_w = w_bf16.shape[1]
    if split_pairs:
        qkv_shape = jax.ShapeDtypeStruct((3, rows, LANES), F32)
        qkv_spec = pl.BlockSpec((3, tm, LANES), lambda i: (0, i, 0))
    else:
        qkv_shape = jax.ShapeDtypeStruct((rows, ATT_W), F32)
        qkv_spec = pl.BlockSpec((tm, ATT_W), lambda i: (i, 0))
    row = lambda w: pl.BlockSpec((tm, w), lambda i: (i, 0))
    full = lambda a: pl.BlockSpec(a.shape, lambda i: (0, 0))
    return pl.pallas_call(
        functools.partial(_proj_kernel, split_pairs=split_pairs, tiles_per_seq=tiles_per_seq,
                          base_offset=base_offset),
        grid=(rows // tm,),
        in_specs=[row(D_MODEL), full(cr), full(sr), full(cb), full(sb), full(g), full(w_bf16)],
        out_specs=[qkv_spec, qkv_spec, qkv_spec, row(SHIFT_W), row(MEM_W), row(MIX_W)],
        out_shape=[qkv_shape, qkv_shape, qkv_shape,
                   jax.ShapeDtypeStruct((rows, SHIFT_W), F32),
                   jax.ShapeDtypeStruct((rows, MEM_W), act),
                   jax.ShapeDtypeStruct((rows, MIX_W), act)],
        compiler_params=_cparams(("arbitrary",)),
        name="proj",
    )(x2d, cr, sr, cb, sb, g, w_bf16)


def _attn_blocks(npairs, blocks):
    h0 = _half0((BAND, LANES))
    h0k = _half0((2 * BAND, LANES))
    units = [(b, p) for b in range(len(blocks)) for p in range(npairs)]
    heads = [(i, hh) for i in range(len(units)) for hh in range(2)]
    q = [blocks[b][0](p) for b, p in units]
    kcat = [jnp.concatenate([blocks[b][1](p), blocks[b][2](p)], axis=0).astype(BF16) for b, p in units]
    vcat = [jnp.concatenate([blocks[b][3](p), blocks[b][4](p)], axis=0) for b, p in units]
    s = [_dot_nt(jnp.where(h0 if hh == 0 else jnp.logical_not(h0), q[i], 0.0), kcat[i]) + blocks[units[i][0]][5]
         for i, hh in heads]
    m = [jnp.max(s[j], axis=-1, keepdims=True) for j in range(len(heads))]
    pexp = [jnp.exp2(s[j] - m[j]).astype(BF16) for j in range(len(heads))]
    res = [jnp.dot(pexp[j], jnp.where(h0k if hh == 0 else jnp.logical_not(h0k), vcat[i], 1.0).astype(BF16),
                   preferred_element_type=F32) for j, (i, hh) in enumerate(heads)]
    lsum = [pltpu.roll(jnp.where(h0, res[2 * i + 1], res[2 * i]), HEAD_DIM, 1) for i in range(len(units))]
    outs = [[] for _ in blocks]
    for i, (b, p) in enumerate(units):
        outs[b].append((jnp.where(h0, res[2 * i], res[2 * i + 1]) / lsum[i],
                        jnp.where(h0, m[2 * i], m[2 * i + 1]) + jnp.log2(lsum[i])))
    return outs


def _attn_kernel(q_ref, kc_ref, kp_ref, vc_ref, vp_ref, o_ref, o_sc, lse_sc, yq_sc, ykv_sc):
    step = pl.program_id(1)
    has_prev = step > 0
    slot = step & 1
    prev_slot = 1 - slot
    qi = lax.broadcasted_iota(jnp.int32, (BAND, 2 * BAND), 0)
    kj = lax.broadcasted_iota(jnp.int32, (BAND, 2 * BAND), 1)
    in_cur = (kj >= BAND) & (kj - BAND <= qi)
    in_prev = (kj < BAND) & (kj >= qi)
    bias_full = jnp.where(in_cur | in_prev, 0.0, NEG).astype(F32)
    bias_first = jnp.where(in_cur | (in_prev & has_prev), 0.0, NEG).astype(F32)
    npairs = q_ref.shape[0]
    nblk = ATT_TILE // BAND
    per_iter = 2
    seg = ATT_TILE // REGROUP

    @pl.when(step == 0)
    def _():
        ykv_sc[prev_slot] = jnp.zeros(ykv_sc.shape[1:], F32)

    def regroup(c, carry):
        per_res = seg // BAND
        res, part = c >> (per_res.bit_length() - 1), c & (per_res - 1)
        src = pl.ds(res + part * (BAND * REGROUP), BAND, stride=REGROUP)
        dst = pl.ds(pl.multiple_of(c * BAND, BAND), BAND)
        for p in range(npairs):
            yq_sc[p, dst, :] = q_ref[p, src, :]
            ykv_sc[slot, 0, p, dst, :] = kc_ref[p, src, :]
            ykv_sc[slot, 1, p, dst, :] = vc_ref[p, src, :]
        return carry

    lax.fori_loop(0, nblk, regroup, 0)

    def desc(d, u, first):
        span = BAND * d
        bias = bias_first if first else bias_full
        if d == 1:
            qs = u * BAND if isinstance(u, int) else pl.multiple_of(u * BAND, BAND)
            rows = pl.ds(qs, BAND)
            prev = (lambda ref_c, ref_p: (lambda p: ref_p[p])) if first else (
                lambda ref_c, ref_p: (lambda p: ref_c[p, pl.ds(qs - BAND, BAND), :]))
            return (lambda p: q_ref[p, rows, :], prev(kc_ref, kp_ref), lambda p: kc_ref[p, rows, :],
                    prev(vc_ref, vp_ref), lambda p: vc_ref[p, rows, :], bias), qs
        assert d % REGROUP == 0
        sub = d // REGROUP
        sp, r = u >> (d.bit_length() - 1), u & (d - 1)
        qs = sp * span + r
        start = (r & (REGROUP - 1)) * seg + sp * (span // REGROUP) + (r >> (REGROUP.bit_length() - 1))
        rows = pl.ds(start, BAND, stride=sub)
        if first:
            prow, pslot = pl.ds(start + seg - span // REGROUP, BAND, stride=sub), prev_slot
        else:
            prow, pslot = pl.ds(start - span // REGROUP, BAND, stride=sub), slot
        return (lambda p: yq_sc[p, rows, :],
                lambda p: ykv_sc[pslot, 0, p, prow, :], lambda p: ykv_sc[slot, 0, p, rows, :],
                lambda p: ykv_sc[pslot, 1, p, prow, :], lambda p: ykv_sc[slot, 1, p, rows, :], bias), qs

    def merge(outs, start):
        row = pl.ds(start, BAND)
        for p, (o1, lse1) in enumerate(outs):
            os_ = [o1] + [o_sc[j, p, row, :] for j in range(len(DILATIONS) - 1)]
            ls_ = [lse1] + [lse_sc[j, p, row, :] for j in range(len(DILATIONS) - 1)]
            top = functools.reduce(jnp.maximum, ls_)
            ws = [jnp.exp2(l - top) for l in ls_]
            num = functools.reduce(lambda a, b: a + b, [w * o for w, o in zip(ws, os_)])
            o_ref[p, row, :] = (num / functools.reduce(lambda a, b: a + b, ws)).astype(o_ref.dtype)

    def run(j, d, descs):
        for (_, qs), outs in zip(descs, _attn_blocks(npairs, [blk for blk, _ in descs])):
            if j < 0:
                merge(outs, qs)
            else:
                for p, (o, lse) in enumerate(outs):
                    o_sc[j, p, pl.ds(qs, BAND, stride=d), :] = o
                    lse_sc[j, p, pl.ds(qs, BAND, stride=d), :] = lse

    def sweep(j, d):
        n_first = d
        several = lambda first: lambda i, c: (run(j, d, [desc(d, i * per_iter + k, first) for k in range(per_iter)]),
                                              c)[1]
        if n_first % per_iter == 0:
            lax.fori_loop(0, n_first // per_iter, several(True), 0)
            lo = n_first
        else:
            assert n_first == 1 and per_iter == 2
            run(j, d, [desc(d, 0, True), desc(d, 1, False)])
            lo = per_iter
        if lo < nblk:
            lax.fori_loop(lo // per_iter, nblk // per_iter, several(False), 0)

    for j, d in enumerate(DILATIONS[1:]):
        sweep(j, d)
    sweep(-1, DILATIONS[0])


def _attn_prompt(q3, k3, v3, batch, seq):
    nt = seq // ATT_TILE
    per_tile = ATT_TILE // BAND
    cur = pl.BlockSpec((3, ATT_TILE, LANES), lambda b, i: (0, b * nt + i, 0))
    prev = pl.BlockSpec((3, BAND, LANES), lambda b, i: (0, jnp.maximum((b * nt + i) * per_tile - 1, 0), 0))
    return pl.pallas_call(
        _attn_kernel,
        grid=(batch, nt),
        in_specs=[cur, cur, prev, cur, prev],
        out_specs=cur,
        out_shape=jax.ShapeDtypeStruct(q3.shape, BF16),
        scratch_shapes=[pltpu.VMEM((len(DILATIONS) - 1, 3, ATT_TILE, LANES), F32),
                        pltpu.VMEM((len(DILATIONS) - 1, 3, ATT_TILE, LANES), F32),
                        pltpu.VMEM((3, ATT_TILE, LANES), F32),
                        pltpu.VMEM((2, 2, 3, ATT_TILE, LANES), F32)],
        compiler_params=_cparams(("arbitrary", "arbitrary")),
        name="attn_prompt",
    )(q3, k3, k3, v3, v3)


def _tails_kernel(k_ref, v_ref, kt_ref, vt_ref):
    kt_ref[0] = k_ref[0].T
    vt_ref[0] = v_ref[0].T


def _tails(k3, v3, batch, seq, win):
    rows = 512
    assert seq % win == 0 and win % rows == 0
    per_win, last = win // rows, seq // win - 1
    src = pl.BlockSpec((1, rows, LANES), lambda b, p, c: (p, (b * (last + 1) + last) * per_win + c, 0))
    dst = pl.BlockSpec((1, LANES, rows), lambda b, p, c: (b, p, c))
    out = jax.ShapeDtypeStruct((batch, ATT_W, win), F32)
    return pl.pallas_call(
        _tails_kernel,
        grid=(batch, ATT_W // LANES, per_win),
        in_specs=[src, src],
        out_specs=[dst, dst],
        out_shape=[out, out],
        compiler_params=_cparams(("arbitrary", "arbitrary", "arbitrary")),
        name="tails",
    )(k3, v3)


def _attn_sample_kernel(q_ref, kn_ref, vn_ref, kc_ref, vc_ref, o_ref, *, n_past, t_new, nb):
    nh = ATT_W // HEAD_DIM
    head_of_lane = lax.broadcasted_iota(jnp.int32, (t_new, ATT_W), 1) >> 6
    seqs = range(nb)
    qst = [jnp.concatenate([jnp.where(head_of_lane == h, q_ref[b], 0.0) for h in range(nh)], axis=0) for b in seqs]

    def count(delta):
        c = jnp.zeros(delta.shape, F32)
        for d in DILATIONS:
            ok = (delta >= 0) & (delta <= BAND * d) & ((delta & (d - 1)) == 0)
            c = c + jnp.where(ok, 1.0, 0.0)
        return c

    rows = nh * t_new
    t_past = lax.broadcasted_iota(jnp.int32, (rows, n_past), 0) & (t_new - 1)
    cnt_past = count(n_past + t_past - lax.broadcasted_iota(jnp.int32, (rows, n_past), 1))
    t_n = lax.broadcasted_iota(jnp.int32, (rows, t_new), 0) & (t_new - 1)
    cnt_new = count(t_n - lax.broadcasted_iota(jnp.int32, (rows, t_new), 1))

    s_past = [jnp.where(cnt_past > 0, _dot(qst[b], kc_ref[b]), NEG) for b in seqs]
    s_new = [jnp.where(cnt_new > 0, _dot_nt(qst[b], kn_ref[b]), NEG) for b in seqs]
    m = [jnp.maximum(jnp.max(s_past[b], axis=-1, keepdims=True), jnp.max(s_new[b], axis=-1, keepdims=True))
         for b in seqs]
    p_past = [cnt_past * jnp.exp2(s_past[b] - m[b]) for b in seqs]
    p_new = [cnt_new * jnp.exp2(s_new[b] - m[b]) for b in seqs]
    l = [jnp.sum(p_past[b], axis=-1, keepdims=True) + jnp.sum(p_new[b], axis=-1, keepdims=True) for b in seqs]
    o = [(_dot_nt(p_past[b], vc_ref[b]) + _dot(p_new[b], vn_ref[b])) / l[b] for b in seqs]
    for b in seqs:
        out = jnp.zeros((t_new, ATT_W), F32)
        for h in range(nh):
            out = jnp.where(head_of_lane == h, o[b][h * t_new:(h + 1) * t_new, :], out)
        o_ref[b] = out.astype(o_ref.dtype)


def _attn_sample(q, k_new, v_new, k_cache, v_cache, nb):
    db, t_new, _ = q.shape
    n_past = k_cache.shape[2]
    new = pl.BlockSpec((nb, t_new, ATT_W), lambda b: (b, 0, 0))
    cache = pl.BlockSpec((nb, ATT_W, n_past), lambda b: (b, 0, 0))
    return pl.pallas_call(
        functools.partial(_attn_sample_kernel, n_past=n_past, t_new=t_new, nb=nb),
        grid=(db // nb,),
        in_specs=[new, new, new, cache, cache],
        out_specs=new,
        out_shape=jax.ShapeDtypeStruct(q.shape, F32),
        compiler_params=_cparams(("arbitrary",)),
        name="attn_sample",
    )(q, k_new, v_new, k_cache, v_cache)


def _seg_sum(x, h0):
    s0 = jnp.sum(jnp.where(h0, x, 0.0), axis=-1, keepdims=True)
    s1 = jnp.sum(jnp.where(h0, 0.0, x), axis=-1, keepdims=True)
    return jnp.where(h0, s0, s1)


def _split_bf16(x):
    hi = x.astype(BF16)
    return hi, (x - hi.astype(F32)).astype(BF16)


def _rwkv_kernel(z_ref, sh0_ref, st0_ref, mu_ref, w0_ref, a0_ref, whi_ref, wlo_ref, kk_ref, ka_ref, rk_ref,
                 lng_ref, lnb_ref, o_ref, st_ref, st_sc, prev_sc, *, nb, groups, n_valid):
    C = RWKV_CHUNK
    R = groups * C
    npairs = RWKV_W // LANES

    @pl.when(pl.program_id(1) == 0)
    def _():
        zero = jnp.zeros((HEAD_DIM, HEAD_DIM), F32)
        for b in range(nb):
            for p in range(npairs):
                st_sc[b, p] = jnp.concatenate(
                    [jnp.concatenate([st0_ref[b, 2 * p], zero], axis=1),
                     jnp.concatenate([zero, st0_ref[b, 2 * p + 1]], axis=1)], axis=0)
        prev_sc[...] = sh0_ref[...]

    si = lax.broadcasted_iota(jnp.int32, (2 * C, 2 * C), 0)
    sj = lax.broadcasted_iota(jnp.int32, (2 * C, 2 * C), 1)
    same_head = (si >= C) == (sj >= C)
    ti2 = si & (C - 1)
    tj2 = sj & (C - 1)
    incl = same_head & (tj2 <= ti2)
    strict = same_head & (tj2 < ti2)
    eye = jnp.where(si == sj, 1.0, 0.0).astype(F32)
    levels = []
    s = 2
    while s < min(C, n_valid):
        sh = s.bit_length() - 1
        levels.append(((ti2 >> (sh + 1)) == (tj2 >> (sh + 1))) & (((ti2 >> sh) & 1) == 1) & (((tj2 >> sh) & 1) == 0))
        s *= 2
    first_level = (ti2 >> 1) == (tj2 >> 1)

    ri = lax.broadcasted_iota(jnp.int32, (C, C), 0)
    rj = lax.broadcasted_iota(jnp.int32, (C, C), 1)
    tri = jnp.where(rj <= ri, 1.0, 0.0).astype(BF16)
    rowid = lax.broadcasted_iota(jnp.int32, (C, 1), 0)
    h0 = _half0((C, LANES))
    zero_blk = jnp.zeros((2 * C, LANES), BF16)
    dot = functools.partial(jnp.dot, preferred_element_type=F32)
    pre = {}

    def stack(x):
        return jnp.concatenate([jnp.where(h0, x, 0.0), jnp.where(h0, 0.0, x)], axis=0)

    def prep(b, g):
        if n_valid < C:
            z = jnp.concatenate([z_ref[b], jnp.zeros((C - n_valid, SHIFT_W), F32)], axis=0)
        else:
            z = z_ref[b, g * C:(g + 1) * C, :]
        before = prev_sc[b] if g == 0 else z_ref[b, pl.ds(g * C - 1, 1), :]
        z_prev = jnp.where(rowid == 0, before, pltpu.roll(z, 1, 0))
        zs = z + (z_prev - z) * mu_ref[...]
        valid = rowid < (n_valid - g * C)
        if n_valid < (g + 1) * C:
            zs = jnp.where(valid, zs, 0.0)
        lat = zs[:, 3 * RWKV_W:]
        lat_hi, lat_lo = _split_bf16(jnp.where(_half0(lat.shape), jnp.tanh(lat), lat))
        lora = dot(lat_hi, whi_ref[...]) + dot(lat_lo, whi_ref[...]) + dot(lat_hi, wlo_ref[...])
        w = -jax.nn.softplus(-(w0_ref[...] + lora[:, :RWKV_W])) - 0.5
        lw = -jnp.exp(w)
        if n_valid < (g + 1) * C:
            lw = jnp.where(valid, lw, 0.0)
        a = jax.nn.sigmoid(a0_ref[...] + lora[:, RWKV_W:])
        lw_hi, lw_lo = _split_bf16(lw)
        cum = dot(tri, lw_hi) + dot(tri, lw_lo)
        for p in range(npairs):
            sl = slice(p * LANES, (p + 1) * LANES)
            r = zs[:, sl]
            k = zs[:, RWKV_W + p * LANES:RWKV_W + (p + 1) * LANES]
            v = zs[:, 2 * RWKV_W + p * LANES:2 * RWKV_W + (p + 1) * LANES]
            ap = a[:, sl]
            kk = k * kk_ref[:, sl]
            kk = kk * jnp.minimum(lax.rsqrt(_seg_sum(kk * kk, h0)), 1e12)
            k = k * (1.0 + (ap - 1.0) * ka_ref[:, sl])
            bb = kk * ap
            L = cum[:, sl]
            l_end = L[C - 1:C, :]
            e_out = jnp.exp(-L)
            e_end = jnp.exp(l_end - L)
            rt = stack(r * jnp.exp(L))
            kkh = stack(kk * jnp.exp(L - lw[:, sl])).astype(BF16)
            left = jnp.concatenate([rt.astype(BF16), kkh], axis=0)
            right = jnp.concatenate([stack(k * e_out), stack(-(bb * e_out))], axis=0).astype(BF16)
            ends = jnp.concatenate([stack(k * e_end), stack(-(bb * e_end))], axis=0).astype(BF16)
            pre[(b, g, p)] = dict(left=left, right=right, ends=ends, rt=rt, kkh=kkh, v2=stack(v).astype(BF16),
                                  decay=jnp.exp(l_end), bonus=_seg_sum(r * k * rk_ref[:, sl], h0) * v)

    def transition_stages(chains):
        def scores():
            for c in chains:
                d = pre[c]
                aa = lax.dot_general(d["left"], d["right"], (((1,), (1,)), ((), ())), preferred_element_type=F32)
                d["a_r"] = jnp.concatenate([jnp.where(incl, aa[:2 * C, :2 * C], 0.0),
                                            jnp.where(incl, aa[:2 * C, 2 * C:], 0.0)], axis=1).astype(BF16)
                d["a_kk"] = jnp.where(strict, aa[2 * C:, :2 * C], 0.0).astype(BF16)
                n_kb = jnp.where(strict, aa[2 * C:, 2 * C:], 0.0)
                d["n_kb"] = n_kb.astype(BF16)
                d["t"] = (eye + jnp.where(first_level, n_kb, 0.0)).astype(BF16)

        def odd_rows(x, s):
            return jnp.concatenate([x[i:i + s] for i in range(s, 2 * C, 2 * s)], axis=0) if s >= 16 else x

        def level_a(off, s):
            for c in chains:
                pre[c]["tn"] = dot(odd_rows(pre[c]["t"], s), jnp.where(off, pre[c]["n_kb"], 0.0)).astype(BF16)

        def level_b(s):
            for c in chains:
                t = pre[c]["t"]
                new = odd_rows(t, s) + dot(pre[c]["tn"], t).astype(BF16)
                if s >= 16:
                    parts = []
                    for j, i in enumerate(range(0, 2 * C, 2 * s)):
                        parts += [t[i:i + s], new[j * s:(j + 1) * s]]
                    new = jnp.concatenate(parts, axis=0)
                pre[c]["t"] = new

        def akv():
            for c in chains:
                pre[c]["akv"] = dot(pre[c]["a_kk"], pre[c]["v2"]).astype(BF16)

        def solve():
            for c in chains:
                d = pre[c]
                tr = dot(d["t"], jnp.concatenate([d["kkh"], d["akv"]], axis=1)).astype(BF16)
                kq, u0 = tr[:, :LANES], tr[:, LANES:]
                d["big"] = jnp.concatenate([jnp.concatenate([d["v2"], zero_blk], axis=1),
                                            jnp.concatenate([u0, kq], axis=1)], axis=0)
        def readout():
            for c in chains:
                d = pre[c]
                yr = dot(d["a_r"], d["big"])
                d["y0"] = yr[:, :LANES]
                d["rq"] = (d["rt"] + yr[:, LANES:]).astype(BF16)
        def update():
            for c in chains:
                d = pre[c]
                dg = lax.dot_general(d["big"], d["ends"], (((0,), (0,)), ((), ())), preferred_element_type=F32)
                d["dd"] = dg[:LANES]
                d["gm"] = dg[LANES:].astype(BF16)
        stages = [scores]
        for i, off in enumerate(levels):
            stages += [functools.partial(level_a, off, 2 << i), functools.partial(level_b, 2 << i)]
        return stages + [akv, solve, readout, update]

    state = {(b, p): st_sc[b, p] for b in range(nb) for p in range(npairs)}

    def carried(chains):
        for (b, g, p) in chains:
            d = pre[(b, g, p)]
            st = state[(b, p)]
            st_b = st.astype(BF16)
            d["y2"] = lax.dot_general(d["rq"], st_b, (((1,), (1,)), ((), ())), preferred_element_type=F32) + d["y0"]
            state[(b, p)] = st * d["decay"] + dot(st_b, d["gm"]) + d["dd"]
        for (b, g, p) in chains:
            d = pre.pop((b, g, p))
            y2 = d["y2"]
            y = y2[:C] + y2[C:]
            sl = slice(p * LANES, (p + 1) * LANES)
            mean = _seg_sum(y, h0) * (1.0 / HEAD_DIM)
            yc = y - mean
            var = _seg_sum(yc * yc, h0) * (1.0 / HEAD_DIM)
            yn = yc * lax.rsqrt(var + LNX_EPS) * lng_ref[:, sl] + lnb_ref[:, sl]
            rows_out = min(C, n_valid - g * C)
            o_ref[b, g * C:g * C + rows_out, sl] = (yn + d["bonus"])[:rows_out].astype(o_ref.dtype)

    items = [(b, g) for g in range(groups) for b in range(nb)]
    for item in items:
        prep(*item)
    chains = [(b, g, p) for (b, g) in items for p in range(npairs)]
    for stage in transition_stages(chains):
        stage()
    carried(chains)
    for b in range(nb):
        prev_sc[b] = z_ref[b, pl.ds(min(R, n_valid) - 1, 1), :]
    for (b, p), st in state.items():
        st_sc[b, p] = st

    @pl.when(pl.program_id(1) == pl.num_programs(1) - 1)
    def _():
        for b in range(nb):
            for p in range(npairs):
                st_ref[b, 2 * p] = st_sc[b, p, :HEAD_DIM, :HEAD_DIM]
                st_ref[b, 2 * p + 1] = st_sc[b, p, HEAD_DIM:, HEAD_DIM:]


def _rwkv(z, shift0, st0, mu, w0, a0, w_hi, w_lo, k_k, k_a, r_k, ln_g, ln_b, nb, groups, act):
    b, s, _ = z.shape
    rows = groups * RWKV_CHUNK
    if s % rows:
        assert groups == 1 and s < rows and s % 8 == 0
        rows = s
    n_valid = rows
    vec = lambda w: pl.BlockSpec((1, w), lambda i, j: (0, 0))
    st_spec = pl.BlockSpec((nb, 2 * 3, HEAD_DIM, HEAD_DIM), lambda i, j: (i, 0, 0, 0))
    wspec = pl.BlockSpec((LANES, 2 * RWKV_W), lambda i, j: (0, 0))
    return pl.pallas_call(
        functools.partial(_rwkv_kernel, nb=nb, groups=groups, n_valid=n_valid),
        grid=(b // nb, s // rows),
        in_specs=[pl.BlockSpec((nb, rows, SHIFT_W), lambda i, j: (i, j, 0)),
                  pl.BlockSpec((nb, 1, SHIFT_W), lambda i, j: (i, 0, 0)),
                  st_spec,
                  vec(SHIFT_W), vec(RWKV_W), vec(RWKV_W), wspec, wspec,
                  vec(RWKV_W), vec(RWKV_W), vec(RWKV_W), vec(RWKV_W), vec(RWKV_W)],
        out_specs=[pl.BlockSpec((nb, rows, RWKV_W), lambda i, j: (i, j, 0)), st_spec],
        out_shape=[jax.ShapeDtypeStruct((b, s, RWKV_W), act),
                   jax.ShapeDtypeStruct((b, 2 * 3, HEAD_DIM, HEAD_DIM), F32)],
        scratch_shapes=[pltpu.VMEM((nb, 3, LANES, LANES), F32), pltpu.VMEM((nb, 1, SHIFT_W), F32)],
        compiler_params=_cparams(("arbitrary", "arbitrary")),
        name="rwkv",
    )(z, shift0, st0, mu, w0, a0, w_hi, w_lo, k_k, k_a, r_k, ln_g, ln_b)


def _memkv_kernel(x_ref, g_ref, w_ref, k_ref, v_ref):
    x = x_ref[...]
    ms = jnp.mean(x * x, axis=-1, keepdims=True)
    h = ((x * lax.rsqrt(ms + NORM_EPS)) * g_ref[...]).astype(BF16)
    k = jnp.dot(h, w_ref[:, :MEM_W], preferred_element_type=F32)
    v = jnp.dot(h, w_ref[:, MEM_W:], preferred_element_type=F32)
    for b in range(k_ref.shape[0]):
        k_ref[b] = k[b * N_MEM:(b + 1) * N_MEM].T
        v_ref[b] = v[b * N_MEM:(b + 1) * N_MEM].T


def _memkv(mem2d, g, w_bf16):
    rows = mem2d.shape[0]
    out = jax.ShapeDtypeStruct((rows // N_MEM, MEM_W, N_MEM), F32)
    return pl.pallas_call(
        _memkv_kernel,
        out_shape=[out, out],
        compiler_params=pltpu.CompilerParams(vmem_limit_bytes=VMEM_LIMIT),
        name="memkv",
    )(mem2d, g, w_bf16)


def _memattn_kernel(q_ref, mk_ref, mv_ref, o_ref, *, nb):
    head_of_lane = lax.broadcasted_iota(jnp.int32, q_ref.shape[1:], 1) >> 6
    qs = [q_ref[b] for b in range(nb)]
    mks = [mk_ref[b].astype(BF16) for b in range(nb)]
    mvs = [mv_ref[b].astype(BF16) for b in range(nb)]
    outs = [jnp.zeros(q_ref.shape[1:], F32) for _ in range(nb)]
    for h in range(MEM_W // HEAD_DIM):
        mine = head_of_lane == h
        s = [_dot(jnp.where(mine, qs[b], 0.0), mks[b]) for b in range(nb)]
        pexp = [jnp.exp2(s[b] - jnp.max(s[b], axis=-1, keepdims=True)) for b in range(nb)]
        l = [jnp.sum(pexp[b], axis=-1, keepdims=True) for b in range(nb)]
        outs = [jnp.where(mine, _dot_nt(pexp[b], mvs[b]) / l[b], outs[b]) for b in range(nb)]
    for b in range(nb):
        o_ref[b] = outs[b].astype(o_ref.dtype)


def _memattn(q, mk, mv, tm, nb, act):
    b, s, _ = q.shape
    qspec = pl.BlockSpec((nb, tm, MEM_W), lambda i, j: (i, j, 0))
    mspec = pl.BlockSpec((nb,) + mk.shape[1:], lambda i, j: (i, 0, 0))
    return pl.pallas_call(
        functools.partial(_memattn_kernel, nb=nb),
        grid=(b // nb, s // tm),
        in_specs=[qspec, mspec, mspec],
        out_specs=qspec,
        out_shape=jax.ShapeDtypeStruct(q.shape, act),
        compiler_params=_cparams(("arbitrary", "arbitrary")),
        name="memattn",
    )(q, mk, mv)


def _out_kernel(oa_ref, ob_ref, om_ref, gate_ref, x_ref, w_ref, g_ref, y_ref, *, split_pairs):
    acc = x_ref[...]

    def add(acc, o, c0):
        width = o.shape[1]
        gate = gate_ref[:, c0:c0 + width]
        return acc + _dot(o * (gate * jax.nn.sigmoid(gate)), w_ref[c0:c0 + width, :])

    if split_pairs:
        for p in range(ATT_W // LANES):
            acc = add(acc, oa_ref[p], p * LANES)
    else:
        acc = add(acc, oa_ref[...], 0)
    acc = add(acc, ob_ref[...], ATT_W)
    acc = add(acc, om_ref[...], ATT_W + RWKV_W)
    ms = jnp.mean(acc * acc, axis=-1, keepdims=True)
    y_ref[...] = (acc * lax.rsqrt(ms + NORM_EPS)) * g_ref[...]


def _out(oa, ob, om, gate, x2d, w_bf16, g, tm, split_pairs):
    rows = x2d.shape[0]
    row = lambda w: pl.BlockSpec((tm, w), lambda i: (i, 0))
    oa_spec = pl.BlockSpec((3, tm, LANES), lambda i: (0, i, 0)) if split_pairs else row(ATT_W)
    return pl.pallas_call(
        functools.partial(_out_kernel, split_pairs=split_pairs),
        grid=(rows // tm,),
        in_specs=[oa_spec, row(RWKV_W), row(MEM_W), row(MIX_W), row(D_MODEL),
                  pl.BlockSpec((MIX_W, D_MODEL), lambda i: (0, 0)),
                  pl.BlockSpec((1, D_MODEL), lambda i: (0, 0))],
        out_specs=row(D_MODEL),
        out_shape=jax.ShapeDtypeStruct((rows, D_MODEL), F32),
        compiler_params=_cparams(("arbitrary",)),
        name="outproj",
    )(oa, ob, om, gate, x2d, w_bf16, g)


def kernel(x_prompt, x_sample, cache_win_k, cache_win_v, state_rwkv, state_rwkv_shift, cache_mem_k, cache_mem_v, mem_prompt, norm_in, w_in, rwkv_mu, rwkv_w0, rwkv_w2, rwkv_a0, rwkv_a2, rwkv_k_k, rwkv_k_a, rwkv_r_k, rwkv_lnx_g, rwkv_lnx_b, norm_mem, w_mem_kv, w_out, norm_final):
    B, S, _ = x_prompt.shape
    DB, T, _ = x_sample.shape
    depth = w_in.shape[0]
    assert depth == 1 and S % ATT_TILE == 0 and cache_win_k.shape[2] == max(DILATIONS) * BAND
    l = 0
    past_len = S

    half = ROPE_DIM // 2
    lane = jnp.arange(LANES)
    inv_freq = ROPE_THETA ** (-(lane % half).astype(F32) / half)
    invf = jnp.where((lane % HEAD_DIM) < ROPE_DIM, inv_freq, 0.0).reshape(1, LANES)

    row = lambda t: t.reshape(1, -1)
    w_in_b = w_in[l].astype(BF16)
    w_out_b = w_out[l].astype(BF16)
    w_kv_b = w_mem_kv[l].astype(BF16)
    zero = jnp.zeros((LORA_W, RWKV_W), F32)
    w2a2 = jnp.concatenate([jnp.concatenate([rwkv_w2[l], zero], axis=1),
                            jnp.concatenate([zero, rwkv_a2[l]], axis=1)], axis=0)
    w2a2_hi = w2a2.astype(BF16)
    w2a2_lo = (w2a2 - w2a2_hi.astype(F32)).astype(BF16)
    rw = (row(rwkv_mu[l]), row(rwkv_w0[l]), row(rwkv_a0[l]), w2a2_hi, w2a2_lo, row(rwkv_k_k[l]),
          row(rwkv_k_a[l]), row(rwkv_r_k[l]), row(rwkv_lnx_g[l]), row(rwkv_lnx_b[l]))

    tiles = S // PROJ_TILE
    n_base = -(-(tiles + 1) // 8) * 8
    pos_base = jnp.where(jnp.arange(n_base) < tiles, jnp.arange(n_base) * PROJ_TILE, past_len)
    cr, sr, cb, sb = _rope_tables(jnp.arange(PROJ_TILE, dtype=F32).reshape(-1, 1),
                                  pos_base.astype(F32).reshape(-1, 1), invf)

    xp = x_prompt.reshape(B * S, D_MODEL)
    q3, k3, v3, zb, qm, gate = _proj(xp, cr, sr, cb, sb, row(norm_in[l]), w_in_b, PROJ_TILE, True, tiles, 0, BF16)
    oa3 = _attn_prompt(q3, k3, v3, B, S)
    ob, st_p = _rwkv(zb.reshape(B, S, SHIFT_W), jnp.zeros((B, 1, SHIFT_W), F32),
                     jnp.zeros((B, RWKV_W // HEAD_DIM, HEAD_DIM, HEAD_DIM), F32), *rw, nb=B, groups=2, act=BF16)
    mk, mv = _memkv(mem_prompt.reshape(B * N_MEM, D_MODEL), row(norm_mem[l]), w_kv_b)
    om = _memattn(qm.reshape(B, S, MEM_W), mk, mv, 1024, 1, BF16)
    y_p = _out(oa3, ob.reshape(B * S, RWKV_W), om.reshape(B * S, MEM_W), gate, xp, w_out_b,
               row(norm_final), 1024, True)

    win = min(max(DILATIONS) * BAND, S)
    kt, vt = _tails(k3, v3, B, S, win)
    tail = lambda t: jnp.transpose(t.reshape(B, ATT_W // HEAD_DIM, HEAD_DIM, win), (0, 3, 1, 2))[None]
    heads = lambda t, n: t.reshape(1, t.shape[0], t.shape[1], n, HEAD_DIM)
    mem_heads = lambda t: jnp.transpose(t.reshape(B, MEM_W // HEAD_DIM, HEAD_DIM, N_MEM), (0, 3, 1, 2))[None]

    xs = x_sample.reshape(DB * T, D_MODEL)
    qs, ks, vs, zbs, qms, gates = _proj(xs, jnp.tile(cr[:T], (DB, 1)), jnp.tile(sr[:T], (DB, 1)), cb, sb,
                                        row(norm_in[l]), w_in_b, DB * T, False, 1, tiles, F32)
    n_past = cache_win_k.shape[2]
    minor = lambda c: jnp.transpose(c, (0, 2, 3, 1)).reshape(c.shape[0], c.shape[2] * c.shape[3], c.shape[1])
    oas = _attn_sample(qs.reshape(DB, T, ATT_W), ks.reshape(DB, T, ATT_W), vs.reshape(DB, T, ATT_W),
                       minor(cache_win_k[l]), minor(cache_win_v[l]), 2)
    zbs3 = zbs.reshape(DB, T, SHIFT_W)
    obs, st_s = _rwkv(zbs3, state_rwkv_shift[l].reshape(DB, 1, SHIFT_W), state_rwkv[l],
                      *rw, nb=8, groups=1, act=F32)
    oms = _memattn(qms.reshape(DB, T, MEM_W), minor(cache_mem_k[l]), minor(cache_mem_v[l]), T, 8, F32)
    y_s = _out(oas.reshape(DB * T, ATT_W), obs.reshape(DB * T, RWKV_W), oms.reshape(DB * T, MEM_W), gates, xs,
               w_out_b, row(norm_final), DB * T, False)

    return (y_p.reshape(B, S, D_MODEL), y_s.reshape(DB, T, D_MODEL),
            tail(kt), tail(vt),
            st_p[None], zb.reshape(B, S, SHIFT_W)[:, -1][None],
            mem_heads(mk), mem_heads(mv),
            heads(ks.reshape(DB, T, ATT_W), ATT_W // HEAD_DIM), heads(vs.reshape(DB, T, ATT_W), ATT_W // HEAD_DIM),
            st_s[None], zbs3[:, -1][None])
```

```python
import functools

import jax
import jax.numpy as jnp
from jax import lax
from jax.experimental import pallas as pl
from jax.experimental.pallas import tpu as pltpu

F32 = jnp.float32
BF16 = jnp.bfloat16

D_MODEL = 1024
HEAD_DIM = 64
ATT_W = 384
RWKV_W = 384
MEM_W = 256
MIX_W = 1024
LORA_W = 64
SHIFT_W = 3 * RWKV_W + 2 * LORA_W
N_MEM = 256
ROPE_DIM = 16
ROPE_THETA = 500000.0
NORM_EPS = 1e-6
LNX_EPS = 64e-5
DILATIONS = (1, 4, 16)
BAND = 128
ATT_TILE = BAND * max(DILATIONS)
REGROUP = 4
LANES = 128
RWKV_CHUNK = 64
PROJ_TILE = 1024
NEG = -1e30
Q_SCALE = HEAD_DIM ** -0.5 * 1.4426950408889634
VMEM_LIMIT = 56 * 1024 * 1024


def _cparams(sem):
    return pltpu.CompilerParams(dimension_semantics=sem, vmem_limit_bytes=VMEM_LIMIT)


def _dot(a, b):
    return jnp.dot(a.astype(BF16), b.astype(BF16), preferred_element_type=F32)


def _dot_nt(a, b):
    return lax.dot_general(a.astype(BF16), b.astype(BF16), (((1,), (1,)), ((), ())),
                           preferred_element_type=F32)


def _dot_tn(a, b):
    return lax.dot_general(a.astype(BF16), b.astype(BF16), (((0,), (0,)), ((), ())),
                           preferred_element_type=F32)


def _dot_f32(a, b):
    return jnp.dot(a, b, preferred_element_type=F32, precision=lax.Precision.HIGHEST)


def _half0(shape):
    return (lax.broadcasted_iota(jnp.int32, shape, len(shape) - 1) & 64) == 0


def _rope_table_kernel(pos_row_ref, pos_base_ref, invf_ref, cr_ref, sr_ref, cb_ref, sb_ref):
    ang_r = pos_row_ref[...] * invf_ref[...]
    cr_ref[...] = jnp.cos(ang_r)
    sr_ref[...] = jnp.sin(ang_r)
    ang_b = pos_base_ref[...] * invf_ref[...]
    cb_ref[...] = jnp.cos(ang_b)
    sb_ref[...] = jnp.sin(ang_b)


def _rope_tables(pos_row, pos_base, invf):
    tab = lambda n: jax.ShapeDtypeStruct((n, LANES), F32)
    return pl.pallas_call(
        _rope_table_kernel,
        out_shape=[tab(pos_row.shape[0]), tab(pos_row.shape[0]), tab(pos_base.shape[0]), tab(pos_base.shape[0])],
        name="rope_tables",
    )(pos_row, pos_base, invf)


def _proj_kernel(x_ref, cr_ref, sr_ref, cb_ref, sb_ref, g_ref, w_ref, q_ref, k_ref, v_ref, zb_ref, qm_ref, gate_ref,
                 *, split_pairs, tiles_per_seq, base_offset):
    tm = x_ref.shape[0]
    base = base_offset + lax.rem(pl.program_id(0), tiles_per_seq)
    cb = cb_ref[pl.ds(base, 1), :]
    sb = sb_ref[pl.ds(base, 1), :]
    second = (lax.broadcasted_iota(jnp.int32, (1, LANES), 1) & 8) != 0

    def put(ref, p, rows, val):
        if split_pairs:
            ref[p, rows, :] = val
        else:
            ref[rows, p * LANES:(p + 1) * LANES] = val

    nsplit = 2 if tm % 16 == 0 else 1
    hm = tm // nsplit
    for part in range(nsplit):
        rows = pl.ds(part * hm, hm)
        x = x_ref[rows, :]
        ms = jnp.mean(x * x, axis=-1, keepdims=True)
        h = ((x * lax.rsqrt(ms + NORM_EPS)) * g_ref[...]).astype(BF16)
        cr = cr_ref[rows, :]
        sr = sr_ref[rows, :]
        cos = cb * cr - sb * sr
        sin = sb * cr + cb * sr
        sin_up = jnp.where(second, sin, 0.0)
        sin_dn = jnp.where(second, 0.0, -sin)

        def rope(t):
            return t * cos + pltpu.roll(t, 8, 1) * sin_up + pltpu.roll(t, LANES - 8, 1) * sin_dn

        qkv = jnp.dot(h, w_ref[:, :3 * ATT_W], preferred_element_type=F32)
        for p in range(ATT_W // LANES):
            put(q_ref, p, rows, rope(qkv[:, p * LANES:(p + 1) * LANES]) * Q_SCALE)
            put(k_ref, p, rows, rope(qkv[:, ATT_W + p * LANES:ATT_W + (p + 1) * LANES]))
            put(v_ref, p, rows, qkv[:, 2 * ATT_W + p * LANES:2 * ATT_W + (p + 1) * LANES])
        c0 = 3 * ATT_W
        zb_ref[rows, :] = jnp.dot(h, w_ref[:, c0:c0 + SHIFT_W], preferred_element_type=F32)
        c0 += SHIFT_W
        qm_ref[rows, :] = (jnp.dot(h, w_ref[:, c0:c0 + MEM_W], preferred_element_type=F32) * Q_SCALE).astype(
            qm_ref.dtype)
        c0 += MEM_W
        gate_ref[rows, :] = jnp.dot(h, w_ref[:, c0:c0 + MIX_W], preferred_element_type=F32).astype(gate_ref.dtype)


def _proj(x2d, cr, sr, cb, sb, g, w_bf16, tm, split_pairs, tiles_per_seq, base_offset, act):
    rows = x2d.shape[0]
    in_w = w_bf16.shape[1]
    if split_pairs:
        qkv_shape = jax.ShapeDtypeStruct((3, rows, LANES), F32)
        qkv_spec = pl.BlockSpec((3, tm, LANES), lambda i: (0, i, 0))
    else:
        qkv_shape = jax.ShapeDtypeStruct((rows, ATT_W), F32)
        qkv_spec = pl.BlockSpec((tm, ATT_W), lambda i: (i, 0))
    row = lambda w: pl.BlockSpec((tm, w), lambda i: (i, 0))
    full = lambda a: pl.BlockSpec(a.shape, lambda i: (0, 0))
    return pl.pallas_call(
        functools.partial(_proj_kernel, split_pairs=split_pairs, tiles_per_seq=tiles_per_seq,
                          base_offset=base_offset),
        grid=(rows // tm,),
        in_specs=[row(D_MODEL), full(cr), full(sr), full(cb), full(sb), full(g), full(w_bf16)],
        out_specs=[qkv_spec, qkv_spec, qkv_spec, row(SHIFT_W), row(MEM_W), row(MIX_W)],
        out_shape=[qkv_shape, qkv_shape, qkv_shape,
                   jax.ShapeDtypeStruct((rows, SHIFT_W), F32),
                   jax.ShapeDtypeStruct((rows, MEM_W), act),
                   jax.ShapeDtypeStruct((rows, MIX_W), act)],
        compiler_params=_cparams(("arbitrary",)),
        name="proj",
    )(x2d, cr, sr, cb, sb, g, w_bf16)


def _attn_blocks(npairs, blocks):
    h0 = _half0((BAND, LANES))
    h0k = _half0((2 * BAND, LANES))
    units = [(b, p) for b in range(len(blocks)) for p in range(npairs)]
    heads = [(i, hh) for i in range(len(units)) for hh in range(2)]
    q = [blocks[b][0](p) for b, p in units]
    kcat = [jnp.concatenate([blocks[b][1](p), blocks[b][2](p)], axis=0).astype(BF16) for b, p in units]
    vcat = [jnp.concatenate([blocks[b][3](p), blocks[b][4](p)], axis=0) for b, p in units]
    s = [_dot_nt(jnp.where(h0 if hh == 0 else jnp.logical_not(h0), q[i], 0.0), kcat[i]) + blocks[units[i][0]][5]
         for i, hh in heads]
    m = [jnp.max(s[j], axis=-1, keepdims=True) for j in range(len(heads))]
    pexp = [jnp.exp2(s[j] - m[j]).astype(BF16) for j in range(len(heads))]
    res = [jnp.dot(pexp[j], jnp.where(h0k if hh == 0 else jnp.logical_not(h0k), vcat[i], 1.0).astype(BF16),
                   preferred_element_type=F32) for j, (i, hh) in enumerate(heads)]
    lsum = [pltpu.roll(jnp.where(h0, res[2 * i + 1], res[2 * i]), HEAD_DIM, 1) for i in range(len(units))]
    outs = [[] for _ in blocks]
    for i, (b, p) in enumerate(units):
        outs[b].append((jnp.where(h0, res[2 * i], res[2 * i + 1]) / lsum[i],
                        jnp.where(h0, m[2 * i], m[2 * i + 1]) + jnp.log2(lsum[i])))
    return outs


def _attn_kernel(q_ref, kc_ref, kp_ref, vc_ref, vp_ref, o_ref, o_sc, lse_sc, yq_sc, ykv_sc):
    step = pl.program_id(1)
    has_prev = step > 0
    slot = step & 1
    prev_slot = 1 - slot
    qi = lax.broadcasted_iota(jnp.int32, (BAND, 2 * BAND), 0)
    kj = lax.broadcasted_iota(jnp.int32, (BAND, 2 * BAND), 1)
    in_cur = (kj >= BAND) & (kj - BAND <= qi)
    in_prev = (kj < BAND) & (kj >= qi)
    bias_full = jnp.where(in_cur | in_prev, 0.0, NEG).astype(F32)
    bias_first = jnp.where(in_cur | (in_prev & has_prev), 0.0, NEG).astype(F32)
    npairs = q_ref.shape[0]
    nblk = ATT_TILE // BAND
    per_iter = 2
    seg = ATT_TILE // REGROUP

    @pl.when(step == 0)
    def _():
        ykv_sc[prev_slot] = jnp.zeros(ykv_sc.shape[1:], F32)

    def regroup(c, carry):
        per_res = seg // BAND
        res, part = c >> (per_res.bit_length() - 1), c & (per_res - 1)
        src = pl.ds(res + part * (BAND * REGROUP), BAND, stride=REGROUP)
        dst = pl.ds(pl.multiple_of(c * BAND, BAND), BAND)
        for p in range(npairs):
            yq_sc[p, dst, :] = q_ref[p, src, :]
            ykv_sc[slot, 0, p, dst, :] = kc_ref[p, src, :]
            ykv_sc[slot, 1, p, dst, :] = vc_ref[p, src, :]
        return carry

    lax.fori_loop(0, nblk, regroup, 0)

    def desc(d, u, first):
        span = BAND * d
        bias = bias_first if first else bias_full
        if d == 1:
            qs = u * BAND if isinstance(u, int) else pl.multiple_of(u * BAND, BAND)
            rows = pl.ds(qs, BAND)
            prev = (lambda ref_c, ref_p: (lambda p: ref_p[p])) if first else (
                lambda ref_c, ref_p: (lambda p: ref_c[p, pl.ds(qs - BAND, BAND), :]))
            return (lambda p: q_ref[p, rows, :], prev(kc_ref, kp_ref), lambda p: kc_ref[p, rows, :],
                    prev(vc_ref, vp_ref), lambda p: vc_ref[p, rows, :], bias), qs
        assert d % REGROUP == 0
        sub = d // REGROUP
        sp, r = u >> (d.bit_length() - 1), u & (d - 1)
        qs = sp * span + r
        start = (r & (REGROUP - 1)) * seg + sp * (span // REGROUP) + (r >> (REGROUP.bit_length() - 1))
        rows = pl.ds(start, BAND, stride=sub)
        if first:
            prow, pslot = pl.ds(start + seg - span // REGROUP, BAND, stride=sub), prev_slot
        else:
            prow, pslot = pl.ds(start - span // REGROUP, BAND, stride=sub), slot
        return (lambda p: yq_sc[p, rows, :],
                lambda p: ykv_sc[pslot, 0, p, prow, :], lambda p: ykv_sc[slot, 0, p, rows, :],
                lambda p: ykv_sc[pslot, 1, p, prow, :], lambda p: ykv_sc[slot, 1, p, rows, :], bias), qs

    def merge(outs, start):
        row = pl.ds(start, BAND)
        for p, (o1, lse1) in enumerate(outs):
            os_ = [o1] + [o_sc[j, p, row, :] for j in range(len(DILATIONS) - 1)]
            ls_ = [lse1] + [lse_sc[j, p, row, :] for j in range(len(DILATIONS) - 1)]
            top = functools.reduce(jnp.maximum, ls_)
            ws = [jnp.exp2(l - top) for l in ls_]
            num = functools.reduce(lambda a, b: a + b, [w * o for w, o in zip(ws, os_)])
            o_ref[p, row, :] = (num / functools.reduce(lambda a, b: a + b, ws)).astype(o_ref.dtype)

    def run(j, d, descs):
        for (_, qs), outs in zip(descs, _attn_blocks(npairs, [blk for blk, _ in descs])):
            if j < 0:
                merge(outs, qs)
            else:
                for p, (o, lse) in enumerate(outs):
                    o_sc[j, p, pl.ds(qs, BAND, stride=d), :] = o
                    lse_sc[j, p, pl.ds(qs, BAND, stride=d), :] = lse

    def sweep(j, d):
        n_first = d
        several = lambda first: lambda i, c: (run(j, d, [desc(d, i * per_iter + k, first) for k in range(per_iter)]),
                                              c)[1]
        if n_first % per_iter == 0:
            lax.fori_loop(0, n_first // per_iter, several(True), 0)
            lo = n_first
        else:
            assert n_first == 1 and per_iter == 2
            run(j, d, [desc(d, 0, True), desc(d, 1, False)])
            lo = per_iter
        if lo < nblk:
            lax.fori_loop(lo // per_iter, nblk // per_iter, several(False), 0)

    for j, d in enumerate(DILATIONS[1:]):
        sweep(j, d)
    sweep(-1, DILATIONS[0])


def _attn_prompt(q3, k3, v3, batch, seq):
    nt = seq // ATT_TILE
    per_tile = ATT_TILE // BAND
    cur = pl.BlockSpec((3, ATT_TILE, LANES), lambda b, i: (0, b * nt + i, 0))
    prev = pl.BlockSpec((3, BAND, LANES), lambda b, i: (0, jnp.maximum((b * nt + i) * per_tile - 1, 0), 0))
    return pl.pallas_call(
        _attn_kernel,
        grid=(batch, nt),
        in_specs=[cur, cur, prev, cur, prev],
        out_specs=cur,
        out_shape=jax.ShapeDtypeStruct(q3.shape, BF16),
        scratch_shapes=[pltpu.VMEM((len(DILATIONS) - 1, 3, ATT_TILE, LANES), F32),
                        pltpu.VMEM((len(DILATIONS) - 1, 3, ATT_TILE, LANES), F32),
                        pltpu.VMEM((3, ATT_TILE, LANES), F32),
                        pltpu.VMEM((2, 2, 3, ATT_TILE, LANES), F32)],
        compiler_params=_cparams(("arbitrary", "arbitrary")),
        name="attn_prompt",
    )(q3, k3, k3, v3, v3)


def _tails_kernel(k_ref, v_ref, kt_ref, vt_ref):
    kt_ref[0] = k_ref[0].T
    vt_ref[0] = v_ref[0].T


def _tails(k3, v3, batch, seq, win):
    rows = win
    assert seq % win == 0 and win % rows == 0
    per_win, last = win // rows, seq // win - 1
    src = pl.BlockSpec((1, rows, LANES), lambda b, p, c: (p, (b * (last + 1) + last) * per_win + c, 0))
    dst = pl.BlockSpec((1, LANES, rows), lambda b, p, c: (b, p, c))
    out = jax.ShapeDtypeStruct((batch, ATT_W, win), F32)
    return pl.pallas_call(
        _tails_kernel,
        grid=(batch, ATT_W // LANES, per_win),
        in_specs=[src, src],
        out_specs=[dst, dst],
        out_shape=[out, out],
        compiler_params=_cparams(("arbitrary", "arbitrary", "arbitrary")),
        name="tails",
    )(k3, v3)


def _attn_sample_kernel(q_ref, kn_ref, vn_ref, kc_ref, vc_ref, o_ref, *, n_past, t_new, nb):
    nh = ATT_W // HEAD_DIM
    head_of_lane = lax.broadcasted_iota(jnp.int32, (t_new, ATT_W), 1) >> 6
    seqs = range(nb)
    qst = [jnp.concatenate([jnp.where(head_of_lane == h, q_ref[b], 0.0) for h in range(nh)], axis=0) for b in seqs]

    def count(delta):
        c = jnp.zeros(delta.shape, F32)
        for d in DILATIONS:
            ok = (delta >= 0) & (delta <= BAND * d) & ((delta & (d - 1)) == 0)
            c = c + jnp.where(ok, 1.0, 0.0)
        return c

    rows = nh * t_new
    t_past = lax.broadcasted_iota(jnp.int32, (rows, n_past), 0) & (t_new - 1)
    cnt_past = count(n_past + t_past - lax.broadcasted_iota(jnp.int32, (rows, n_past), 1))
    t_n = lax.broadcasted_iota(jnp.int32, (rows, t_new), 0) & (t_new - 1)
    cnt_new = count(t_n - lax.broadcasted_iota(jnp.int32, (rows, t_new), 1))

    s_past = [jnp.where(cnt_past > 0, _dot(qst[b], kc_ref[b]), NEG) for b in seqs]
    s_new = [jnp.where(cnt_new > 0, _dot_nt(qst[b], kn_ref[b]), NEG) for b in seqs]
    m = [jnp.maximum(jnp.max(s_past[b], axis=-1, keepdims=True), jnp.max(s_new[b], axis=-1, keepdims=True))
         for b in seqs]
    p_past = [cnt_past * jnp.exp2(s_past[b] - m[b]) for b in seqs]
    p_new = [cnt_new * jnp.exp2(s_new[b] - m[b]) for b in seqs]
    l = [jnp.sum(p_past[b], axis=-1, keepdims=True) + jnp.sum(p_new[b], axis=-1, keepdims=True) for b in seqs]
    o = [(_dot_nt(p_past[b], vc_ref[b]) + _dot(p_new[b], vn_ref[b])) / l[b] for b in seqs]
    for b in seqs:
        out = jnp.zeros((t_new, ATT_W), F32)
        for h in range(nh):
            out = jnp.where(head_of_lane == h, o[b][h * t_new:(h + 1) * t_new, :], out)
        o_ref[b] = out.astype(o_ref.dtype)


def _attn_sample(q, k_new, v_new, k_cache, v_cache, nb):
    db, t_new, _ = q.shape
    n_past = k_cache.shape[2]
    new = pl.BlockSpec((nb, t_new, ATT_W), lambda b: (b, 0, 0))
    cache = pl.BlockSpec((nb, ATT_W, n_past), lambda b: (b, 0, 0))
    return pl.pallas_call(
        functools.partial(_attn_sample_kernel, n_past=n_past, t_new=t_new, nb=nb),
        grid=(db // nb,),
        in_specs=[new, new, new, cache, cache],
        out_specs=new,
        out_shape=jax.ShapeDtypeStruct(q.shape, F32),
        compiler_params=_cparams(("arbitrary",)),
        name="attn_sample",
    )(q, k_new, v_new, k_cache, v_cache)


def _seg_sum(x, h0):
    s0 = jnp.sum(jnp.where(h0, x, 0.0), axis=-1, keepdims=True)
    s1 = jnp.sum(jnp.where(h0, 0.0, x), axis=-1, keepdims=True)
    return jnp.where(h0, s0, s1)


def _split_bf16(x):
    hi = x.astype(BF16)
    return hi, (x - hi.astype(F32)).astype(BF16)


def _rwkv_kernel(z_ref, sh0_ref, st0_ref, mu_ref, w0_ref, a0_ref, whi_ref, wlo_ref, kk_ref, ka_ref, rk_ref,
                 lng_ref, lnb_ref, o_ref, st_ref, st_sc, prev_sc, *, nb, groups, n_valid):
    C = RWKV_CHUNK
    R = groups * C
    npairs = RWKV_W // LANES

    @pl.when(pl.program_id(1) == 0)
    def _():
        zero = jnp.zeros((HEAD_DIM, HEAD_DIM), F32)
        for b in range(nb):
            for p in range(npairs):
                st_sc[b, p] = jnp.concatenate(
                    [jnp.concatenate([st0_ref[b, 2 * p], zero], axis=1),
                     jnp.concatenate([zero, st0_ref[b, 2 * p + 1]], axis=1)], axis=0)
        prev_sc[...] = sh0_ref[...]

    si = lax.broadcasted_iota(jnp.int32, (2 * C, 2 * C), 0)
    sj = lax.broadcasted_iota(jnp.int32, (2 * C, 2 * C), 1)
    same_head = (si >= C) == (sj >= C)
    ti2 = si & (C - 1)
    tj2 = sj & (C - 1)
    incl = same_head & (tj2 <= ti2)
    strict = same_head & (tj2 < ti2)
    eye = jnp.where(si == sj, 1.0, 0.0).astype(F32)
    levels = []
    s = 2
    while s < min(C, n_valid):
        sh = s.bit_length() - 1
        levels.append(((ti2 >> (sh + 1)) == (tj2 >> (sh + 1))) & (((ti2 >> sh) & 1) == 1) & (((tj2 >> sh) & 1) == 0))
        s *= 2
    first_level = (ti2 >> 1) == (tj2 >> 1)

    ri = lax.broadcasted_iota(jnp.int32, (C, C), 0)
    rj = lax.broadcasted_iota(jnp.int32, (C, C), 1)
    tri = jnp.where(rj <= ri, 1.0, 0.0).astype(BF16)
    rowid = lax.broadcasted_iota(jnp.int32, (C, 1), 0)
    h0 = _half0((C, LANES))
    zero_blk = jnp.zeros((2 * C, LANES), BF16)
    dot = functools.partial(jnp.dot, preferred_element_type=F32)
    pre = {}

    def stack(x):
        return jnp.concatenate([jnp.where(h0, x, 0.0), jnp.where(h0, 0.0, x)], axis=0)

    def prep(b, g):
        if n_valid < C:
            z = jnp.concatenate([z_ref[b], jnp.zeros((C - n_valid, SHIFT_W), F32)], axis=0)
        else:
            z = z_ref[b, g * C:(g + 1) * C, :]
        before = prev_sc[b] if g == 0 else z_ref[b, pl.ds(g * C - 1, 1), :]
        z_prev = jnp.where(rowid == 0, before, pltpu.roll(z, 1, 0))
        zs = z + (z_prev - z) * mu_ref[...]
        valid = rowid < (n_valid - g * C)
        if n_valid < (g + 1) * C:
            zs = jnp.where(valid, zs, 0.0)
        lat = zs[:, 3 * RWKV_W:]
        lat_hi, lat_lo = _split_bf16(jnp.where(_half0(lat.shape), jnp.tanh(lat), lat))
        lora = dot(lat_hi, whi_ref[...]) + dot(lat_lo, whi_ref[...]) + dot(lat_hi, wlo_ref[...])
        w = -jax.nn.softplus(-(w0_ref[...] + lora[:, :RWKV_W])) - 0.5
        lw = -jnp.exp(w)
        if n_valid < (g + 1) * C:
            lw = jnp.where(valid, lw, 0.0)
        a = jax.nn.sigmoid(a0_ref[...] + lora[:, RWKV_W:])
        lw_hi, lw_lo = _split_bf16(lw)
        cum = dot(tri, lw_hi) + dot(tri, lw_lo)
        for p in range(npairs):
            sl = slice(p * LANES, (p + 1) * LANES)
            r = zs[:, sl]
            k = zs[:, RWKV_W + p * LANES:RWKV_W + (p + 1) * LANES]
            v = zs[:, 2 * RWKV_W + p * LANES:2 * RWKV_W + (p + 1) * LANES]
            ap = a[:, sl]
            kk = k * kk_ref[:, sl]
            kk = kk * jnp.minimum(lax.rsqrt(_seg_sum(kk * kk, h0)), 1e12)
            k = k * (1.0 + (ap - 1.0) * ka_ref[:, sl])
            bb = kk * ap
            L = cum[:, sl]
            l_end = L[C - 1:C, :]
            e_out = jnp.exp(-L)
            e_end = jnp.exp(l_end - L)
            rt = stack(r * jnp.exp(L))
            kkh = stack(kk * jnp.exp(L - lw[:, sl])).astype(BF16)
            left = jnp.concatenate([rt.astype(BF16), kkh], axis=0)
            right = jnp.concatenate([stack(k * e_out), stack(-(bb * e_out))], axis=0).astype(BF16)
            ends = jnp.concatenate([stack(k * e_end), stack(-(bb * e_end))], axis=0).astype(BF16)
            pre[(b, g, p)] = dict(left=left, right=right, ends=ends, rt=rt, kkh=kkh, v2=stack(v).astype(BF16),
                                  decay=jnp.exp(l_end), bonus=_seg_sum(r * k * rk_ref[:, sl], h0) * v)

    def transition_stages(chains):
        def scores():
            for c in chains:
                d = pre[c]
                aa = lax.dot_general(d["left"], d["right"], (((1,), (1,)), ((), ())), preferred_element_type=F32)
                d["a_r"] = jnp.concatenate([jnp.where(incl, aa[:2 * C, :2 * C], 0.0),
                                            jnp.where(incl, aa[:2 * C, 2 * C:], 0.0)], axis=1).astype(BF16)
                d["a_kk"] = jnp.where(strict, aa[2 * C:, :2 * C], 0.0).astype(BF16)
                n_kb = jnp.where(strict, aa[2 * C:, 2 * C:], 0.0)
                d["n_kb"] = n_kb.astype(BF16)
                d["t"] = (eye + jnp.where(first_level, n_kb, 0.0)).astype(BF16)

        def odd_rows(x, s):
            return jnp.concatenate([x[i:i + s] for i in range(s, 2 * C, 2 * s)], axis=0) if s >= 16 else x

        def level_a(off, s):
            for c in chains:
                pre[c]["tn"] = dot(odd_rows(pre[c]["t"], s), jnp.where(off, pre[c]["n_kb"], 0.0)).astype(BF16)

        def level_b(s):
            for c in chains:
                t = pre[c]["t"]
                new = odd_rows(t, s) + dot(pre[c]["tn"], t).astype(BF16)
                if s >= 16:
                    parts = []
                    for j, i in enumerate(range(0, 2 * C, 2 * s)):
                        parts += [t[i:i + s], new[j * s:(j + 1) * s]]
                    new = jnp.concatenate(parts, axis=0)
                pre[c]["t"] = new

        def akv():
            for c in chains:
                pre[c]["akv"] = dot(pre[c]["a_kk"], pre[c]["v2"]).astype(BF16)

        def solve():
            for c in chains:
                d = pre[c]
                tr = dot(d["t"], jnp.concatenate([d["kkh"], d["akv"]], axis=1)).astype(BF16)
                kq, u0 = tr[:, :LANES], tr[:, LANES:]
                d["big"] = jnp.concatenate([jnp.concatenate([d["v2"], zero_blk], axis=1),
                                            jnp.concatenate([u0, kq], axis=1)], axis=0)
        def readout():
            for c in chains:
                d = pre[c]
                yr = dot(d["a_r"], d["big"])
                d["y0"] = yr[:, :LANES]
                d["rq"] = (d["rt"] + yr[:, LANES:]).astype(BF16)
        def update():
            for c in chains:
                d = pre[c]
                dg = lax.dot_general(d["big"], d["ends"], (((0,), (0,)), ((), ())), preferred_element_type=F32)
                d["dd"] = dg[:LANES]
                d["gm"] = dg[LANES:].astype(BF16)
        stages = [scores]
        for i, off in enumerate(levels):
            stages += [functools.partial(level_a, off, 2 << i), functools.partial(level_b, 2 << i)]
        return stages + [akv, solve, readout, update]

    state = {(b, p): st_sc[b, p] for b in range(nb) for p in range(npairs)}

    def carried(chains):
        for (b, g, p) in chains:
            d = pre[(b, g, p)]
            st = state[(b, p)]
            st_b = st.astype(BF16)
            d["y2"] = lax.dot_general(d["rq"], st_b, (((1,), (1,)), ((), ())), preferred_element_type=F32) + d["y0"]
            state[(b, p)] = st * d["decay"] + dot(st_b, d["gm"]) + d["dd"]
        for (b, g, p) in chains:
            d = pre.pop((b, g, p))
            y2 = d["y2"]
            y = y2[:C] + y2[C:]
            sl = slice(p * LANES, (p + 1) * LANES)
            mean = _seg_sum(y, h0) * (1.0 / HEAD_DIM)
            yc = y - mean
            var = _seg_sum(yc * yc, h0) * (1.0 / HEAD_DIM)
            yn = yc * lax.rsqrt(var + LNX_EPS) * lng_ref[:, sl] + lnb_ref[:, sl]
            rows_out = min(C, n_valid - g * C)
            o_ref[b, g * C:g * C + rows_out, sl] = (yn + d["bonus"])[:rows_out].astype(o_ref.dtype)

    items = [(b, g) for g in range(groups) for b in range(nb)]
    for item in items:
        prep(*item)
    chains = [(b, g, p) for (b, g) in items for p in range(npairs)]
    for stage in transition_stages(chains):
        stage()
    carried(chains)
    for b in range(nb):
        prev_sc[b] = z_ref[b, pl.ds(min(R, n_valid) - 1, 1), :]
    for (b, p), st in state.items():
        st_sc[b, p] = st

    @pl.when(pl.program_id(1) == pl.num_programs(1) - 1)
    def _():
        for b in range(nb):
            for p in range(npairs):
                st_ref[b, 2 * p] = st_sc[b, p, :HEAD_DIM, :HEAD_DIM]
                st_ref[b, 2 * p + 1] = st_sc[b, p, HEAD_DIM:, HEAD_DIM:]


def _rwkv(z, shift0, st0, mu, w0, a0, w_hi, w_lo, k_k, k_a, r_k, ln_g, ln_b, nb, groups, act):
    b, s, _ = z.shape
    rows = groups * RWKV_CHUNK
    if s % rows:
        assert groups == 1 and s < rows and s % 8 == 0
        rows = s
    n_valid = rows
    vec = lambda w: pl.BlockSpec((1, w), lambda i, j: (0, 0))
    st_spec = pl.BlockSpec((nb, 2 * 3, HEAD_DIM, HEAD_DIM), lambda i, j: (i, 0, 0, 0))
    wspec = pl.BlockSpec((LANES, 2 * RWKV_W), lambda i, j: (0, 0))
    return pl.pallas_call(
        functools.partial(_rwkv_kernel, nb=nb, groups=groups, n_valid=n_valid),
        grid=(b // nb, s // rows),
        in_specs=[pl.BlockSpec((nb, rows, SHIFT_W), lambda i, j: (i, j, 0)),
                  pl.BlockSpec((nb, 1, SHIFT_W), lambda i, j: (i, 0, 0)),
                  st_spec,
                  vec(SHIFT_W), vec(RWKV_W), vec(RWKV_W), wspec, wspec,
                  vec(RWKV_W), vec(RWKV_W), vec(RWKV_W), vec(RWKV_W), vec(RWKV_W)],
        out_specs=[pl.BlockSpec((nb, rows, RWKV_W), lambda i, j: (i, j, 0)), st_spec],
        out_shape=[jax.ShapeDtypeStruct((b, s, RWKV_W), act),
                   jax.ShapeDtypeStruct((b, 2 * 3, HEAD_DIM, HEAD_DIM), F32)],
        scratch_shapes=[pltpu.VMEM((nb, 3, LANES, LANES), F32), pltpu.VMEM((nb, 1, SHIFT_W), F32)],
        compiler_params=_cparams(("arbitrary", "arbitrary")),
        name="rwkv",
    )(z, shift0, st0, mu, w0, a0, w_hi, w_lo, k_k, k_a, r_k, ln_g, ln_b)


def _memkv_kernel(x_ref, g_ref, w_ref, k_ref, v_ref):
    x = x_ref[...]
    ms = jnp.mean(x * x, axis=-1, keepdims=True)
    h = ((x * lax.rsqrt(ms + NORM_EPS)) * g_ref[...]).astype(BF16)
    k = jnp.dot(h, w_ref[:, :MEM_W], preferred_element_type=F32)
    v = jnp.dot(h, w_ref[:, MEM_W:], preferred_element_type=F32)
    for b in range(k_ref.shape[0]):
        k_ref[b] = k[b * N_MEM:(b + 1) * N_MEM].T
        v_ref[b] = v[b * N_MEM:(b + 1) * N_MEM].T


def _memkv(mem2d, g, w_bf16):
    rows = mem2d.shape[0]
    out = jax.ShapeDtypeStruct((rows // N_MEM, MEM_W, N_MEM), F32)
    return pl.pallas_call(
        _memkv_kernel,
        out_shape=[out, out],
        compiler_params=pltpu.CompilerParams(vmem_limit_bytes=VMEM_LIMIT),
        name="memkv",
    )(mem2d, g, w_bf16)


def _memattn_kernel(q_ref, mk_ref, mv_ref, o_ref, *, nb):
    head_of_lane = lax.broadcasted_iota(jnp.int32, q_ref.shape[1:], 1) >> 6
    qs = [q_ref[b] for b in range(nb)]
    mks = [mk_ref[b].astype(BF16) for b in range(nb)]
    mvs = [mv_ref[b].astype(BF16) for b in range(nb)]
    outs = [jnp.zeros(q_ref.shape[1:], F32) for _ in range(nb)]
    for h in range(MEM_W // HEAD_DIM):
        mine = head_of_lane == h
        s = [_dot(jnp.where(mine, qs[b], 0.0), mks[b]) for b in range(nb)]
        pexp = [jnp.exp2(s[b] - jnp.max(s[b], axis=-1, keepdims=True)) for b in range(nb)]
        l = [jnp.sum(pexp[b], axis=-1, keepdims=True) for b in range(nb)]
        outs = [jnp.where(mine, _dot_nt(pexp[b], mvs[b]) / l[b], outs[b]) for b in range(nb)]
    for b in range(nb):
        o_ref[b] = outs[b].astype(o_ref.dtype)


def _memattn(q, mk, mv, tm, nb, act):
    b, s, _ = q.shape
    qspec = pl.BlockSpec((nb, tm, MEM_W), lambda i, j: (i, j, 0))
    mspec = pl.BlockSpec((nb,) + mk.shape[1:], lambda i, j: (i, 0, 0))
    return pl.pallas_call(
        functools.partial(_memattn_kernel, nb=nb),
        grid=(b // nb, s // tm),
        in_specs=[qspec, mspec, mspec],
        out_specs=qspec,
        out_shape=jax.ShapeDtypeStruct(q.shape, act),
        compiler_params=_cparams(("arbitrary", "arbitrary")),
        name="memattn",
    )(q, mk, mv)


def _out_kernel(oa_ref, ob_ref, om_ref, gate_ref, x_ref, w_ref, g_ref, y_ref, *, split_pairs):
    acc = x_ref[...]

    def add(acc, o, c0):
        width = o.shape[1]
        gate = gate_ref[:, c0:c0 + width]
        return acc + _dot(o * (gate * jax.nn.sigmoid(gate)), w_ref[c0:c0 + width, :])

    if split_pairs:
        for p in range(ATT_W // LANES):
            acc = add(acc, oa_ref[p], p * LANES)
    else:
        acc = add(acc, oa_ref[...], 0)
    acc = add(acc, ob_ref[...], ATT_W)
    acc = add(acc, om_ref[...], ATT_W + RWKV_W)
    ms = jnp.mean(acc * acc, axis=-1, keepdims=True)
    y_ref[...] = (acc * lax.rsqrt(ms + NORM_EPS)) * g_ref[...]


def _out(oa, ob, om, gate, x2d, w_bf16, g, tm, split_pairs):
    rows = x2d.shape[0]
    row = lambda w: pl.BlockSpec((tm, w), lambda i: (i, 0))
    oa_spec = pl.BlockSpec((3, tm, LANES), lambda i: (0, i, 0)) if split_pairs else row(ATT_W)
    return pl.pallas_call(
        functools.partial(_out_kernel, split_pairs=split_pairs),
        grid=(rows // tm,),
        in_specs=[oa_spec, row(RWKV_W), row(MEM_W), row(MIX_W), row(D_MODEL),
                  pl.BlockSpec((MIX_W, D_MODEL), lambda i: (0, 0)),
                  pl.BlockSpec((1, D_MODEL), lambda i: (0, 0))],
        out_specs=row(D_MODEL),
        out_shape=jax.ShapeDtypeStruct((rows, D_MODEL), F32),
        compiler_params=_cparams(("arbitrary",)),
        name="outproj",
    )(oa, ob, om, gate, x2d, w_bf16, g)


def kernel(x_prompt, x_sample, cache_win_k, cache_win_v, state_rwkv, state_rwkv_shift, cache_mem_k, cache_mem_v, mem_prompt, norm_in, w_in, rwkv_mu, rwkv_w0, rwkv_w2, rwkv_a0, rwkv_a2, rwkv_k_k, rwkv_k_a, rwkv_r_k, rwkv_lnx_g, rwkv_lnx_b, norm_mem, w_mem_kv, w_out, norm_final):
    B, S, _ = x_prompt.shape
    DB, T, _ = x_sample.shape
    depth = w_in.shape[0]
    assert depth == 1 and S % ATT_TILE == 0 and cache_win_k.shape[2] == max(DILATIONS) * BAND
    l = 0
    past_len = S

    half = ROPE_DIM // 2
    lane = jnp.arange(LANES)
    inv_freq = ROPE_THETA ** (-(lane % half).astype(F32) / half)
    invf = jnp.where((lane % HEAD_DIM) < ROPE_DIM, inv_freq, 0.0).reshape(1, LANES)

    row = lambda t: t.reshape(1, -1)
    w_in_b = w_in[l].astype(BF16)
    w_out_b = w_out[l].astype(BF16)
    w_kv_b = w_mem_kv[l].astype(BF16)
    zero = jnp.zeros((LORA_W, RWKV_W), F32)
    w2a2 = jnp.concatenate([jnp.concatenate([rwkv_w2[l], zero], axis=1),
                            jnp.concatenate([zero, rwkv_a2[l]], axis=1)], axis=0)
    w2a2_hi = w2a2.astype(BF16)
    w2a2_lo = (w2a2 - w2a2_hi.astype(F32)).astype(BF16)
    rw = (row(rwkv_mu[l]), row(rwkv_w0[l]), row(rwkv_a0[l]), w2a2_hi, w2a2_lo, row(rwkv_k_k[l]),
          row(rwkv_k_a[l]), row(rwkv_r_k[l]), row(rwkv_lnx_g[l]), row(rwkv_lnx_b[l]))

    tiles = S // PROJ_TILE
    n_base = -(-(tiles + 1) // 8) * 8
    pos_base = jnp.where(jnp.arange(n_base) < tiles, jnp.arange(n_base) * PROJ_TILE, past_len)
    cr, sr, cb, sb = _rope_tables(jnp.arange(PROJ_TILE, dtype=F32).reshape(-1, 1),
                                  pos_base.astype(F32).reshape(-1, 1), invf)

    xp = x_prompt.reshape(B * S, D_MODEL)
    q3, k3, v3, zb, qm, gate = _proj(xp, cr, sr, cb, sb, row(norm_in[l]), w_in_b, PROJ_TILE, True, tiles, 0, BF16)
    oa3 = _attn_prompt(q3, k3, v3, B, S)
    ob, st_p = _rwkv(zb.reshape(B, S, SHIFT_W), jnp.zeros((B, 1, SHIFT_W), F32),
                     jnp.zeros((B, RWKV_W // HEAD_DIM, HEAD_DIM, HEAD_DIM), F32), *rw, nb=B, groups=2, act=BF16)
    mk, mv = _memkv(mem_prompt.reshape(B * N_MEM, D_MODEL), row(norm_mem[l]), w_kv_b)
    om = _memattn(qm.reshape(B, S, MEM_W), mk, mv, 2048, 1, BF16)
    y_p = _out(oa3, ob.reshape(B * S, RWKV_W), om.reshape(B * S, MEM_W), gate, xp, w_out_b,
               row(norm_final), 1024, True)

    win = min(max(DILATIONS) * BAND, S)
    kt, vt = _tails(k3, v3, B, S, win)
    tail = lambda t: jnp.transpose(t.reshape(B, ATT_W // HEAD_DIM, HEAD_DIM, win), (0, 3, 1, 2))[None]
    heads = lambda t, n: t.reshape(1, t.shape[0], t.shape[1], n, HEAD_DIM)
    mem_heads = lambda t: jnp.transpose(t.reshape(B, MEM_W // HEAD_DIM, HEAD_DIM, N_MEM), (0, 3, 1, 2))[None]

    xs = x_sample.reshape(DB * T, D_MODEL)
    qs, ks, vs, zbs, qms, gates = _proj(xs, jnp.tile(cr[:T], (DB, 1)), jnp.tile(sr[:T], (DB, 1)), cb, sb,
                                        row(norm_in[l]), w_in_b, DB * T, False, 1, tiles, F32)
    n_past = cache_win_k.shape[2]
    minor = lambda c: jnp.transpose(c, (0, 2, 3, 1)).reshape(c.shape[0], c.shape[2] * c.shape[3], c.shape[1])
    oas = _attn_sample(qs.reshape(DB, T, ATT_W), ks.reshape(DB, T, ATT_W), vs.reshape(DB, T, ATT_W),
                       minor(cache_win_k[l]), minor(cache_win_v[l]), 2)
    zbs3 = zbs.reshape(DB, T, SHIFT_W)
    obs, st_s = _rwkv(zbs3, state_rwkv_shift[l].reshape(DB, 1, SHIFT_W), state_rwkv[l],
                      *rw, nb=8, groups=1, act=F32)
    oms = _memattn(qms.reshape(DB, T, MEM_W), minor(cache_mem_k[l]), minor(cache_mem_v[l]), T, 8, F32)
    y_s = _out(oas.reshape(DB * T, ATT_W), obs.reshape(DB * T, RWKV_W), oms.reshape(DB * T, MEM_W), gates, xs,
               w_out_b, row(norm_final), DB * T, False)

    return (y_p.reshape(B, S, D_MODEL), y_s.reshape(DB, T, D_MODEL),
            tail(kt), tail(vt),
            st_p[None], zb.reshape(B, S, SHIFT_W)[:, -1][None],
            mem_heads(mk), mem_heads(mv),
            heads(ks.reshape(DB, T, ATT_W), ATT_W // HEAD_DIM), heads(vs.reshape(DB, T, ATT_W), ATT_W // HEAD_DIM),
            st_s[None], zbs3[:, -1][None])
```

```python
import functools

import jax
import jax.numpy as jnp
from jax import lax
from jax.experimental import pallas as pl
from jax.experimental.pallas import tpu as pltpu

F32 = jnp.float32
BF16 = jnp.bfloat16

D_MODEL = 1024
HEAD_DIM = 64
ATT_W = 384
RWKV_W = 384
MEM_W = 256
MIX_W = 1024
LORA_W = 64
SHIFT_W = 3 * RWKV_W + 2 * LORA_W
N_MEM = 256
ROPE_DIM = 16
ROPE_THETA = 500000.0
NORM_EPS = 1e-6
LNX_EPS = 64e-5
DILATIONS = (1, 4, 16)
BAND = 128
ATT_TILE = BAND * max(DILATIONS)
REGROUP = 4
LANES = 128
RWKV_CHUNK = 64
PROJ_TILE = 1024
MEMATTN_TILE = 2048
RWKV_PROMPT_GROUPS = 2
RWKV_SAMPLE_SEQS = 8
ATT_SAMPLE_SEQS = 2
NEG = -1e30
Q_SCALE = HEAD_DIM ** -0.5 * 1.4426950408889634
VMEM_LIMIT = 56 * 1024 * 1024


def _cparams(sem):
    return pltpu.CompilerParams(dimension_semantics=sem, vmem_limit_bytes=VMEM_LIMIT)


def _dot(a, b):
    return jnp.dot(a.astype(BF16), b.astype(BF16), preferred_element_type=F32)


def _dot_nt(a, b):
    return lax.dot_general(a.astype(BF16), b.astype(BF16), (((1,), (1,)), ((), ())),
                           preferred_element_type=F32)


def _dot_tn(a, b):
    return lax.dot_general(a.astype(BF16), b.astype(BF16), (((0,), (0,)), ((), ())),
                           preferred_element_type=F32)


def _dot_f32(a, b):
    return jnp.dot(a, b, preferred_element_type=F32, precision=lax.Precision.HIGHEST)


def _half0(shape):
    return (lax.broadcasted_iota(jnp.int32, shape, len(shape) - 1) & 64) == 0


def _rope_table_kernel(pos_row_ref, pos_base_ref, invf_ref, cr_ref, sr_ref, cb_ref, sb_ref):
    ang_r = pos_row_ref[...] * invf_ref[...]
    cr_ref[...] = jnp.cos(ang_r)
    sr_ref[...] = jnp.sin(ang_r)
    ang_b = pos_base_ref[...] * invf_ref[...]
    cb_ref[...] = jnp.cos(ang_b)
    sb_ref[...] = jnp.sin(ang_b)


def _rope_tables(pos_row, pos_base, invf):
    tab = lambda n: jax.ShapeDtypeStruct((n, LANES), F32)
    return pl.pallas_call(
        _rope_table_kernel,
        out_shape=[tab(pos_row.shape[0]), tab(pos_row.shape[0]), tab(pos_base.shape[0]), tab(pos_base.shape[0])],
        name="rope_tables",
    )(pos_row, pos_base, invf)


def _proj_kernel(x_ref, cr_ref, sr_ref, cb_ref, sb_ref, g_ref, w_ref, q_ref, k_ref, v_ref, zb_ref, qm_ref, gate_ref,
                 *, split_pairs, tiles_per_seq, base_offset):
    tm = x_ref.shape[0]
    base = base_offset + lax.rem(pl.program_id(0), tiles_per_seq)
    cb = cb_ref[pl.ds(base, 1), :]
    sb = sb_ref[pl.ds(base, 1), :]
    second = (lax.broadcasted_iota(jnp.int32, (1, LANES), 1) & 8) != 0

    def put(ref, p, rows, val):
        if split_pairs:
            ref[p, rows, :] = val
        else:
            ref[rows, p * LANES:(p + 1) * LANES] = val

    nsplit = 2 if tm % 16 == 0 else 1
    hm = tm // nsplit
    for part in range(nsplit):
        rows = pl.ds(part * hm, hm)
        x = x_ref[rows, :]
        ms = jnp.mean(x * x, axis=-1, keepdims=True)
        h = ((x * lax.rsqrt(ms + NORM_EPS)) * g_ref[...]).astype(BF16)
        cr = cr_ref[rows, :]
        sr = sr_ref[rows, :]
        cos = cb * cr - sb * sr
        sin = sb * cr + cb * sr
        sin_up = jnp.where(second, sin, 0.0)
        sin_dn = jnp.where(second, 0.0, -sin)

        def rope(t):
            return t * cos + pltpu.roll(t, 8, 1) * sin_up + pltpu.roll(t, LANES - 8, 1) * sin_dn

        qkv = jnp.dot(h, w_ref[:, :3 * ATT_W], preferred_element_type=F32)
        for p in range(ATT_W // LANES):
            put(q_ref, p, rows, rope(qkv[:, p * LANES:(p + 1) * LANES]) * Q_SCALE)
            put(k_ref, p, rows, rope(qkv[:, ATT_W + p * LANES:ATT_W + (p + 1) * LANES]))
            put(v_ref, p, rows, qkv[:, 2 * ATT_W + p * LANES:2 * ATT_W + (p + 1) * LANES])
        c0 = 3 * ATT_W
        zb_ref[rows, :] = jnp.dot(h, w_ref[:, c0:c0 + SHIFT_W], preferred_element_type=F32)
        c0 += SHIFT_W
        qm_ref[rows, :] = (jnp.dot(h, w_ref[:, c0:c0 + MEM_W], preferred_element_type=F32) * Q_SCALE).astype(
            qm_ref.dtype)
        c0 += MEM_W
        gate_ref[rows, :] = jnp.dot(h, w_ref[:, c0:c0 + MIX_W], preferred_element_type=F32).astype(gate_ref.dtype)


def _proj(x2d, cr, sr, cb, sb, g, w_bf16, tm, split_pairs, tiles_per_seq, base_offset, act):
    rows = x2d.shape[0]
    in_w = w_bf16.shape[1]
    if split_pairs:
        qkv_shape = jax.ShapeDtypeStruct((3, rows, LANES), F32)
        qkv_spec = pl.BlockSpec((3, tm, LANES), lambda i: (0, i, 0))
    else:
        qkv_shape = jax.ShapeDtypeStruct((rows, ATT_W), F32)
        qkv_spec = pl.BlockSpec((tm, ATT_W), lambda i: (i, 0))
    row = lambda w: pl.BlockSpec((tm, w), lambda i: (i, 0))
    full = lambda a: pl.BlockSpec(a.shape, lambda i: (0, 0))
    return pl.pallas_call(
        functools.partial(_proj_kernel, split_pairs=split_pairs, tiles_per_seq=tiles_per_seq,
                          base_offset=base_offset),
        grid=(rows // tm,),
        in_specs=[row(D_MODEL), full(cr), full(sr), full(cb), full(sb), full(g), full(w_bf16)],
        out_specs=[qkv_spec, qkv_spec, qkv_spec, row(SHIFT_W), row(MEM_W), row(MIX_W)],
        out_shape=[qkv_shape, qkv_shape, qkv_shape,
                   jax.ShapeDtypeStruct((rows, SHIFT_W), F32),
                   jax.ShapeDtypeStruct((rows, MEM_W), act),
                   jax.ShapeDtypeStruct((rows, MIX_W), act)],
        compiler_params=_cparams(("arbitrary",)),
        name="proj",
    )(x2d, cr, sr, cb, sb, g, w_bf16)


def _attn_blocks(npairs, blocks):
    h0 = _half0((BAND, LANES))
    h0k = _half0((2 * BAND, LANES))
    units = [(b, p) for b in range(len(blocks)) for p in range(npairs)]
    heads = [(i, hh) for i in range(len(units)) for hh in range(2)]
    q = [blocks[b][0](p) for b, p in units]
    kcat = [jnp.concatenate([blocks[b][1](p), blocks[b][2](p)], axis=0).astype(BF16) for b, p in units]
    vcat = [jnp.concatenate([blocks[b][3](p), blocks[b][4](p)], axis=0) for b, p in units]
    s = [_dot_nt(jnp.where(h0 if hh == 0 else jnp.logical_not(h0), q[i], 0.0), kcat[i]) + blocks[units[i][0]][5]
         for i, hh in heads]
    m = [jnp.max(s[j], axis=-1, keepdims=True) for j in range(len(heads))]
    pexp = [jnp.exp2(s[j] - m[j]).astype(BF16) for j in range(len(heads))]
    res = [jnp.dot(pexp[j], jnp.where(h0k if hh == 0 else jnp.logical_not(h0k), vcat[i], 1.0).astype(BF16),
                   preferred_element_type=F32) for j, (i, hh) in enumerate(heads)]
    lsum = [pltpu.roll(jnp.where(h0, res[2 * i + 1], res[2 * i]), HEAD_DIM, 1) for i in range(len(units))]
    outs = [[] for _ in blocks]
    for i, (b, p) in enumerate(units):
        outs[b].append((jnp.where(h0, res[2 * i], res[2 * i + 1]) / lsum[i],
                        jnp.where(h0, m[2 * i], m[2 * i + 1]) + jnp.log2(lsum[i])))
    return outs


def _attn_kernel(q_ref, kc_ref, kp_ref, vc_ref, vp_ref, o_ref, o_sc, lse_sc, yq_sc, ykv_sc):
    step = pl.program_id(1)
    has_prev = step > 0
    slot = step & 1
    prev_slot = 1 - slot
    qi = lax.broadcasted_iota(jnp.int32, (BAND, 2 * BAND), 0)
    kj = lax.broadcasted_iota(jnp.int32, (BAND, 2 * BAND), 1)
    in_cur = (kj >= BAND) & (kj - BAND <= qi)
    in_prev = (kj < BAND) & (kj >= qi)
    bias_full = jnp.where(in_cur | in_prev, 0.0, NEG).astype(F32)
    bias_first = jnp.where(in_cur | (in_prev & has_prev), 0.0, NEG).astype(F32)
    npairs = q_ref.shape[0]
    nblk = ATT_TILE // BAND
    per_iter = 2
    seg = ATT_TILE // REGROUP

    @pl.when(step == 0)
    def _():
        ykv_sc[prev_slot] = jnp.zeros(ykv_sc.shape[1:], F32)

    def regroup(c, carry):
        per_res = seg // BAND
        res, part = c >> (per_res.bit_length() - 1), c & (per_res - 1)
        src = pl.ds(res + part * (BAND * REGROUP), BAND, stride=REGROUP)
        dst = pl.ds(pl.multiple_of(c * BAND, BAND), BAND)
        for p in range(npairs):
            yq_sc[p, dst, :] = q_ref[p, src, :]
            ykv_sc[slot, 0, p, dst, :] = kc_ref[p, src, :]
            ykv_sc[slot, 1, p, dst, :] = vc_ref[p, src, :]
        return carry

    lax.fori_loop(0, nblk, regroup, 0)

    def desc(d, u, first):
        span = BAND * d
        bias = bias_first if first else bias_full
        if d == 1:
            qs = u * BAND if isinstance(u, int) else pl.multiple_of(u * BAND, BAND)
            rows = pl.ds(qs, BAND)
            prev = (lambda ref_c, ref_p: (lambda p: ref_p[p])) if first else (
                lambda ref_c, ref_p: (lambda p: ref_c[p, pl.ds(qs - BAND, BAND), :]))
            return (lambda p: q_ref[p, rows, :], prev(kc_ref, kp_ref), lambda p: kc_ref[p, rows, :],
                    prev(vc_ref, vp_ref), lambda p: vc_ref[p, rows, :], bias), qs
        assert d % REGROUP == 0
        sub = d // REGROUP
        sp, r = u >> (d.bit_length() - 1), u & (d - 1)
        qs = sp * span + r
        start = (r & (REGROUP - 1)) * seg + sp * (span // REGROUP) + (r >> (REGROUP.bit_length() - 1))
        rows = pl.ds(start, BAND, stride=sub)
        if first:
            prow, pslot = pl.ds(start + seg - span // REGROUP, BAND, stride=sub), prev_slot
        else:
            prow, pslot = pl.ds(start - span // REGROUP, BAND, stride=sub), slot
        return (lambda p: yq_sc[p, rows, :],
                lambda p: ykv_sc[pslot, 0, p, prow, :], lambda p: ykv_sc[slot, 0, p, rows, :],
                lambda p: ykv_sc[pslot, 1, p, prow, :], lambda p: ykv_sc[slot, 1, p, rows, :], bias), qs

    def merge(outs, start):
        row = pl.ds(start, BAND)
        for p, (o1, lse1) in enumerate(outs):
            os_ = [o1] + [o_sc[j, p, row, :] for j in range(len(DILATIONS) - 1)]
            ls_ = [lse1] + [lse_sc[j, p, row, :] for j in range(len(DILATIONS) - 1)]
            top = functools.reduce(jnp.maximum, ls_)
            ws = [jnp.exp2(l - top) for l in ls_]
            num = functools.reduce(lambda a, b: a + b, [w * o for w, o in zip(ws, os_)])
            o_ref[p, row, :] = (num / functools.reduce(lambda a, b: a + b, ws)).astype(o_ref.dtype)

    def run(j, d, descs):
        for (_, qs), outs in zip(descs, _attn_blocks(npairs, [blk for blk, _ in descs])):
            if j < 0:
                merge(outs, qs)
            else:
                for p, (o, lse) in enumerate(outs):
                    o_sc[j, p, pl.ds(qs, BAND, stride=d), :] = o
                    lse_sc[j, p, pl.ds(qs, BAND, stride=d), :] = lse

    def sweep(j, d):
        n_first = d
        several = lambda first: lambda i, c: (run(j, d, [desc(d, i * per_iter + k, first) for k in range(per_iter)]),
                                              c)[1]
        if n_first % per_iter == 0:
            lax.fori_loop(0, n_first // per_iter, several(True), 0)
            lo = n_first
        else:
            assert n_first == 1 and per_iter == 2
            run(j, d, [desc(d, 0, True), desc(d, 1, False)])
            lo = per_iter
        if lo < nblk:
            lax.fori_loop(lo // per_iter, nblk // per_iter, several(False), 0)

    for j, d in enumerate(DILATIONS[1:]):
        sweep(j, d)
    sweep(-1, DILATIONS[0])


def _attn_prompt(q3, k3, v3, batch, seq):
    nt = seq // ATT_TILE
    per_tile = ATT_TILE // BAND
    cur = pl.BlockSpec((3, ATT_TILE, LANES), lambda b, i: (0, b * nt + i, 0))
    prev = pl.BlockSpec((3, BAND, LANES), lambda b, i: (0, jnp.maximum((b * nt + i) * per_tile - 1, 0), 0))
    return pl.pallas_call(
        _attn_kernel,
        grid=(batch, nt),
        in_specs=[cur, cur, prev, cur, prev],
        out_specs=cur,
        out_shape=jax.ShapeDtypeStruct(q3.shape, BF16),
        scratch_shapes=[pltpu.VMEM((len(DILATIONS) - 1, 3, ATT_TILE, LANES), F32),
                        pltpu.VMEM((len(DILATIONS) - 1, 3, ATT_TILE, LANES), F32),
                        pltpu.VMEM((3, ATT_TILE, LANES), F32),
                        pltpu.VMEM((2, 2, 3, ATT_TILE, LANES), F32)],
        compiler_params=_cparams(("arbitrary", "arbitrary")),
        name="attn_prompt",
    )(q3, k3, k3, v3, v3)


def _tails_kernel(k_ref, v_ref, kt_ref, vt_ref):
    kt_ref[0] = k_ref[0].T
    vt_ref[0] = v_ref[0].T


def _tails(k3, v3, batch, seq, win):
    rows = win
    assert seq % win == 0 and win % rows == 0
    per_win, last = win // rows, seq // win - 1
    src = pl.BlockSpec((1, rows, LANES), lambda b, p, c: (p, (b * (last + 1) + last) * per_win + c, 0))
    dst = pl.BlockSpec((1, LANES, rows), lambda b, p, c: (b, p, c))
    out = jax.ShapeDtypeStruct((batch, ATT_W, win), F32)
    return pl.pallas_call(
        _tails_kernel,
        grid=(batch, ATT_W // LANES, per_win),
        in_specs=[src, src],
        out_specs=[dst, dst],
        out_shape=[out, out],
        compiler_params=_cparams(("arbitrary", "arbitrary", "arbitrary")),
        name="tails",
    )(k3, v3)


def _attn_sample_kernel(q_ref, kn_ref, vn_ref, kc_ref, vc_ref, o_ref, *, n_past, t_new, nb):
    nh = ATT_W // HEAD_DIM
    head_of_lane = lax.broadcasted_iota(jnp.int32, (t_new, ATT_W), 1) >> 6
    seqs = range(nb)
    qst = [jnp.concatenate([jnp.where(head_of_lane == h, q_ref[b], 0.0) for h in range(nh)], axis=0) for b in seqs]

    def count(delta):
        c = jnp.zeros(delta.shape, F32)
        for d in DILATIONS:
            ok = (delta >= 0) & (delta <= BAND * d) & ((delta & (d - 1)) == 0)
            c = c + jnp.where(ok, 1.0, 0.0)
        return c

    rows = nh * t_new
    t_past = lax.broadcasted_iota(jnp.int32, (rows, n_past), 0) & (t_new - 1)
    cnt_past = count(n_past + t_past - lax.broadcasted_iota(jnp.int32, (rows, n_past), 1))
    t_n = lax.broadcasted_iota(jnp.int32, (rows, t_new), 0) & (t_new - 1)
    cnt_new = count(t_n - lax.broadcasted_iota(jnp.int32, (rows, t_new), 1))

    s_past = [jnp.where(cnt_past > 0, _dot(qst[b], kc_ref[b]), NEG) for b in seqs]
    s_new = [jnp.where(cnt_new > 0, _dot_nt(qst[b], kn_ref[b]), NEG) for b in seqs]
    m = [jnp.maximum(jnp.max(s_past[b], axis=-1, keepdims=True), jnp.max(s_new[b], axis=-1, keepdims=True))
         for b in seqs]
    p_past = [cnt_past * jnp.exp2(s_past[b] - m[b]) for b in seqs]
    p_new = [cnt_new * jnp.exp2(s_new[b] - m[b]) for b in seqs]
    l = [jnp.sum(p_past[b], axis=-1, keepdims=True) + jnp.sum(p_new[b], axis=-1, keepdims=True) for b in seqs]
    o = [(_dot_nt(p_past[b], vc_ref[b]) + _dot(p_new[b], vn_ref[b])) / l[b] for b in seqs]
    for b in seqs:
        out = jnp.zeros((t_new, ATT_W), F32)
        for h in range(nh):
            out = jnp.where(head_of_lane == h, o[b][h * t_new:(h + 1) * t_new, :], out)
        o_ref[b] = out.astype(o_ref.dtype)


def _attn_sample(q, k_new, v_new, k_cache, v_cache, nb):
    db, t_new, _ = q.shape
    n_past = k_cache.shape[2]
    new = pl.BlockSpec((nb, t_new, ATT_W), lambda b: (b, 0, 0))
    cache = pl.BlockSpec((nb, ATT_W, n_past), lambda b: (b, 0, 0))
    return pl.pallas_call(
        functools.partial(_attn_sample_kernel, n_past=n_past, t_new=t_new, nb=nb),
        grid=(db // nb,),
        in_specs=[new, new, new, cache, cache],
        out_specs=new,
        out_shape=jax.ShapeDtypeStruct(q.shape, F32),
        compiler_params=_cparams(("arbitrary",)),
        name="attn_sample",
    )(q, k_new, v_new, k_cache, v_cache)


def _seg_sum(x, h0):
    s0 = jnp.sum(jnp.where(h0, x, 0.0), axis=-1, keepdims=True)
    s1 = jnp.sum(jnp.where(h0, 0.0, x), axis=-1, keepdims=True)
    return jnp.where(h0, s0, s1)


def _split_bf16(x):
    hi = x.astype(BF16)
    return hi, (x - hi.astype(F32)).astype(BF16)


def _rwkv_kernel(z_ref, sh0_ref, st0_ref, mu_ref, w0_ref, a0_ref, whi_ref, wlo_ref, kk_ref, ka_ref, rk_ref,
                 lng_ref, lnb_ref, o_ref, st_ref, st_sc, prev_sc, *, nb, groups, n_valid):
    C = RWKV_CHUNK
    R = groups * C
    npairs = RWKV_W // LANES

    @pl.when(pl.program_id(1) == 0)
    def _():
        zero = jnp.zeros((HEAD_DIM, HEAD_DIM), F32)
        for b in range(nb):
            for p in range(npairs):
                st_sc[b, p] = jnp.concatenate(
                    [jnp.concatenate([st0_ref[b, 2 * p], zero], axis=1),
                     jnp.concatenate([zero, st0_ref[b, 2 * p + 1]], axis=1)], axis=0)
        prev_sc[...] = sh0_ref[...]

    si = lax.broadcasted_iota(jnp.int32, (2 * C, 2 * C), 0)
    sj = lax.broadcasted_iota(jnp.int32, (2 * C, 2 * C), 1)
    same_head = (si >= C) == (sj >= C)
    ti2 = si & (C - 1)
    tj2 = sj & (C - 1)
    incl = same_head & (tj2 <= ti2)
    strict = same_head & (tj2 < ti2)
    eye = jnp.where(si == sj, 1.0, 0.0).astype(F32)
    levels = []
    s = 2
    while s < min(C, n_valid):
        sh = s.bit_length() - 1
        levels.append(((ti2 >> (sh + 1)) == (tj2 >> (sh + 1))) & (((ti2 >> sh) & 1) == 1) & (((tj2 >> sh) & 1) == 0))
        s *= 2
    first_level = (ti2 >> 1) == (tj2 >> 1)

    ri = lax.broadcasted_iota(jnp.int32, (C, C), 0)
    rj = lax.broadcasted_iota(jnp.int32, (C, C), 1)
    tri = jnp.where(rj <= ri, 1.0, 0.0).astype(BF16)
    rowid = lax.broadcasted_iota(jnp.int32, (C, 1), 0)
    h0 = _half0((C, LANES))
    zero_blk = jnp.zeros((2 * C, LANES), BF16)
    dot = functools.partial(jnp.dot, preferred_element_type=F32)
    pre = {}

    def stack(x):
        return jnp.concatenate([jnp.where(h0, x, 0.0), jnp.where(h0, 0.0, x)], axis=0)

    def prep(b, g):
        if n_valid < C:
            z = jnp.concatenate([z_ref[b], jnp.zeros((C - n_valid, SHIFT_W), F32)], axis=0)
        else:
            z = z_ref[b, g * C:(g + 1) * C, :]
        before = prev_sc[b] if g == 0 else z_ref[b, pl.ds(g * C - 1, 1), :]
        z_prev = jnp.where(rowid == 0, before, pltpu.roll(z, 1, 0))
        zs = z + (z_prev - z) * mu_ref[...]
        valid = rowid < (n_valid - g * C)
        if n_valid < (g + 1) * C:
            zs = jnp.where(valid, zs, 0.0)
        lat = zs[:, 3 * RWKV_W:]
        lat_hi, lat_lo = _split_bf16(jnp.where(_half0(lat.shape), jnp.tanh(lat), lat))
        lora = dot(lat_hi, whi_ref[...]) + dot(lat_lo, whi_ref[...]) + dot(lat_hi, wlo_ref[...])
        w = -jax.nn.softplus(-(w0_ref[...] + lora[:, :RWKV_W])) - 0.5
        lw = -jnp.exp(w)
        if n_valid < (g + 1) * C:
            lw = jnp.where(valid, lw, 0.0)
        a = jax.nn.sigmoid(a0_ref[...] + lora[:, RWKV_W:])
        lw_hi, lw_lo = _split_bf16(lw)
        cum = dot(tri, lw_hi) + dot(tri, lw_lo)
        for p in range(npairs):
            sl = slice(p * LANES, (p + 1) * LANES)
            r = zs[:, sl]
            k = zs[:, RWKV_W + p * LANES:RWKV_W + (p + 1) * LANES]
            v = zs[:, 2 * RWKV_W + p * LANES:2 * RWKV_W + (p + 1) * LANES]
            ap = a[:, sl]
            kk = k * kk_ref[:, sl]
            kk = kk * jnp.minimum(lax.rsqrt(_seg_sum(kk * kk, h0)), 1e12)
            k = k * (1.0 + (ap - 1.0) * ka_ref[:, sl])
            bb = kk * ap
            L = cum[:, sl]
            l_end = L[C - 1:C, :]
            e_out = jnp.exp(-L)
            e_end = jnp.exp(l_end - L)
            rt = stack(r * jnp.exp(L))
            kkh = stack(kk * jnp.exp(L - lw[:, sl])).astype(BF16)
            left = jnp.concatenate([rt.astype(BF16), kkh], axis=0)
            right = jnp.concatenate([stack(k * e_out), stack(-(bb * e_out))], axis=0).astype(BF16)
            ends = jnp.concatenate([stack(k * e_end), stack(-(bb * e_end))], axis=0).astype(BF16)
            pre[(b, g, p)] = dict(left=left, right=right, ends=ends, rt=rt, kkh=kkh, v2=stack(v).astype(BF16),
                                  decay=jnp.exp(l_end), bonus=_seg_sum(r * k * rk_ref[:, sl], h0) * v)

    def transition_stages(chains):
        def scores():
            for c in chains:
                d = pre[c]
                aa = lax.dot_general(d["left"], d["right"], (((1,), (1,)), ((), ())), preferred_element_type=F32)
                d["a_r"] = jnp.concatenate([jnp.where(incl, aa[:2 * C, :2 * C], 0.0),
                                            jnp.where(incl, aa[:2 * C, 2 * C:], 0.0)], axis=1).astype(BF16)
                d["a_kk"] = jnp.where(strict, aa[2 * C:, :2 * C], 0.0).astype(BF16)
                n_kb = jnp.where(strict, aa[2 * C:, 2 * C:], 0.0)
                d["n_kb"] = n_kb.astype(BF16)
                d["t"] = (eye + jnp.where(first_level, n_kb, 0.0)).astype(BF16)

        def odd_rows(x, s):
            return jnp.concatenate([x[i:i + s] for i in range(s, 2 * C, 2 * s)], axis=0) if s >= 16 else x

        def level_a(off, s):
            for c in chains:
                pre[c]["tn"] = dot(odd_rows(pre[c]["t"], s), jnp.where(off, pre[c]["n_kb"], 0.0)).astype(BF16)

        def level_b(s):
            for c in chains:
                t = pre[c]["t"]
                new = odd_rows(t, s) + dot(pre[c]["tn"], t).astype(BF16)
                if s >= 16:
                    parts = []
                    for j, i in enumerate(range(0, 2 * C, 2 * s)):
                        parts += [t[i:i + s], new[j * s:(j + 1) * s]]
                    new = jnp.concatenate(parts, axis=0)
                pre[c]["t"] = new

        def akv():
            for c in chains:
                pre[c]["akv"] = dot(pre[c]["a_kk"], pre[c]["v2"]).astype(BF16)

        def solve():
            for c in chains:
                d = pre[c]
                tr = dot(d["t"], jnp.concatenate([d["kkh"], d["akv"]], axis=1)).astype(BF16)
                kq, u0 = tr[:, :LANES], tr[:, LANES:]
                d["big"] = jnp.concatenate([jnp.concatenate([d["v2"], zero_blk], axis=1),
                                            jnp.concatenate([u0, kq], axis=1)], axis=0)

        def readout():
            for c in chains:
                d = pre[c]
                yr = dot(d["a_r"], d["big"])
                d["y0"] = yr[:, :LANES]
                d["rq"] = (d["rt"] + yr[:, LANES:]).astype(BF16)

        def update():
            for c in chains:
                d = pre[c]
                dg = lax.dot_general(d["big"], d["ends"], (((0,), (0,)), ((), ())), preferred_element_type=F32)
                d["dd"] = dg[:LANES]
                d["gm"] = dg[LANES:].astype(BF16)

        stages = [scores]
        for i, off in enumerate(levels):
            stages += [functools.partial(level_a, off, 2 << i), functools.partial(level_b, 2 << i)]
        return stages + [akv, solve, readout, update]

    state = {(b, p): st_sc[b, p] for b in range(nb) for p in range(npairs)}

    def carried(chains):
        for (b, g, p) in chains:
            d = pre[(b, g, p)]
            st = state[(b, p)]
            st_b = st.astype(BF16)
            d["y2"] = lax.dot_general(d["rq"], st_b, (((1,), (1,)), ((), ())), preferred_element_type=F32) + d["y0"]
            state[(b, p)] = st * d["decay"] + dot(st_b, d["gm"]) + d["dd"]
        for (b, g, p) in chains:
            d = pre.pop((b, g, p))
            y2 = d["y2"]
            y = y2[:C] + y2[C:]
            sl = slice(p * LANES, (p + 1) * LANES)
            mean = _seg_sum(y, h0) * (1.0 / HEAD_DIM)
            yc = y - mean
            var = _seg_sum(yc * yc, h0) * (1.0 / HEAD_DIM)
            yn = yc * lax.rsqrt(var + LNX_EPS) * lng_ref[:, sl] + lnb_ref[:, sl]
            rows_out = min(C, n_valid - g * C)
            o_ref[b, g * C:g * C + rows_out, sl] = (yn + d["bonus"])[:rows_out].astype(o_ref.dtype)

    items = [(b, g) for g in range(groups) for b in range(nb)]
    for item in items:
        prep(*item)
    chains = [(b, g, p) for (b, g) in items for p in range(npairs)]
    for stage in transition_stages(chains):
        stage()
    carried(chains)
    for b in range(nb):
        prev_sc[b] = z_ref[b, pl.ds(min(R, n_valid) - 1, 1), :]
    for (b, p), st in state.items():
        st_sc[b, p] = st

    @pl.when(pl.program_id(1) == pl.num_programs(1) - 1)
    def _():
        for b in range(nb):
            for p in range(npairs):
                st_ref[b, 2 * p] = st_sc[b, p, :HEAD_DIM, :HEAD_DIM]
                st_ref[b, 2 * p + 1] = st_sc[b, p, HEAD_DIM:, HEAD_DIM:]


def _rwkv(z, shift0, st0, mu, w0, a0, w_hi, w_lo, k_k, k_a, r_k, ln_g, ln_b, nb, groups, act):
    b, s, _ = z.shape
    rows = groups * RWKV_CHUNK
    if s % rows:
        assert groups == 1 and s < rows and s % 8 == 0
        rows = s
    n_valid = rows
    vec = lambda w: pl.BlockSpec((1, w), lambda i, j: (0, 0))
    st_spec = pl.BlockSpec((nb, 2 * 3, HEAD_DIM, HEAD_DIM), lambda i, j: (i, 0, 0, 0))
    wspec = pl.BlockSpec((LANES, 2 * RWKV_W), lambda i, j: (0, 0))
    return pl.pallas_call(
        functools.partial(_rwkv_kernel, nb=nb, groups=groups, n_valid=n_valid),
        grid=(b // nb, s // rows),
        in_specs=[pl.BlockSpec((nb, rows, SHIFT_W), lambda i, j: (i, j, 0)),
                  pl.BlockSpec((nb, 1, SHIFT_W), lambda i, j: (i, 0, 0)),
                  st_spec,
                  vec(SHIFT_W), vec(RWKV_W), vec(RWKV_W), wspec, wspec,
                  vec(RWKV_W), vec(RWKV_W), vec(RWKV_W), vec(RWKV_W), vec(RWKV_W)],
        out_specs=[pl.BlockSpec((nb, rows, RWKV_W), lambda i, j: (i, j, 0)), st_spec],
        out_shape=[jax.ShapeDtypeStruct((b, s, RWKV_W), act),
                   jax.ShapeDtypeStruct((b, 2 * 3, HEAD_DIM, HEAD_DIM), F32)],
        scratch_shapes=[pltpu.VMEM((nb, 3, LANES, LANES), F32), pltpu.VMEM((nb, 1, SHIFT_W), F32)],
        compiler_params=_cparams(("arbitrary", "arbitrary")),
        name="rwkv",
    )(z, shift0, st0, mu, w0, a0, w_hi, w_lo, k_k, k_a, r_k, ln_g, ln_b)


def _memkv_kernel(x_ref, g_ref, w_ref, k_ref, v_ref):
    x = x_ref[...]
    ms = jnp.mean(x * x, axis=-1, keepdims=True)
    h = ((x * lax.rsqrt(ms + NORM_EPS)) * g_ref[...]).astype(BF16)
    k = jnp.dot(h, w_ref[:, :MEM_W], preferred_element_type=F32)
    v = jnp.dot(h, w_ref[:, MEM_W:], preferred_element_type=F32)
    for b in range(k_ref.shape[0]):
        k_ref[b] = k[b * N_MEM:(b + 1) * N_MEM].T
        v_ref[b] = v[b * N_MEM:(b + 1) * N_MEM].T


def _memkv(mem2d, g, w_bf16):
    rows = mem2d.shape[0]
    out = jax.ShapeDtypeStruct((rows // N_MEM, MEM_W, N_MEM), F32)
    return pl.pallas_call(
        _memkv_kernel,
        out_shape=[out, out],
        compiler_params=pltpu.CompilerParams(vmem_limit_bytes=VMEM_LIMIT),
        name="memkv",
    )(mem2d, g, w_bf16)


def _memattn_kernel(q_ref, mk_ref, mv_ref, o_ref, *, nb):
    head_of_lane = lax.broadcasted_iota(jnp.int32, q_ref.shape[1:], 1) >> 6
    qs = [q_ref[b] for b in range(nb)]
    mks = [mk_ref[b].astype(BF16) for b in range(nb)]
    mvs = [mv_ref[b].astype(BF16) for b in range(nb)]
    outs = [jnp.zeros(q_ref.shape[1:], F32) for _ in range(nb)]
    for h in range(MEM_W // HEAD_DIM):
        mine = head_of_lane == h
        s = [_dot(jnp.where(mine, qs[b], 0.0), mks[b]) for b in range(nb)]
        pexp = [jnp.exp2(s[b] - jnp.max(s[b], axis=-1, keepdims=True)) for b in range(nb)]
        l = [jnp.sum(pexp[b], axis=-1, keepdims=True) for b in range(nb)]
        outs = [jnp.where(mine, _dot_nt(pexp[b], mvs[b]) / l[b], outs[b]) for b in range(nb)]
    for b in range(nb):
        o_ref[b] = outs[b].astype(o_ref.dtype)


def _memattn(q, mk, mv, tm, nb, act):
    b, s, _ = q.shape
    qspec = pl.BlockSpec((nb, tm, MEM_W), lambda i, j: (i, j, 0))
    mspec = pl.BlockSpec((nb,) + mk.shape[1:], lambda i, j: (i, 0, 0))
    return pl.pallas_call(
        functools.partial(_memattn_kernel, nb=nb),
        grid=(b // nb, s // tm),
        in_specs=[qspec, mspec, mspec],
        out_specs=qspec,
        out_shape=jax.ShapeDtypeStruct(q.shape, act),
        compiler_params=_cparams(("arbitrary", "arbitrary")),
        name="memattn",
    )(q, mk, mv)


def _out_kernel(oa_ref, ob_ref, om_ref, gate_ref, x_ref, w_ref, g_ref, y_ref, *, split_pairs):
    acc = x_ref[...]

    def add(acc, o, c0):
        width = o.shape[1]
        gate = gate_ref[:, c0:c0 + width]
        return acc + _dot(o * (gate * jax.nn.sigmoid(gate)), w_ref[c0:c0 + width, :])

    if split_pairs:
        for p in range(ATT_W // LANES):
            acc = add(acc, oa_ref[p], p * LANES)
    else:
        acc = add(acc, oa_ref[...], 0)
    acc = add(acc, ob_ref[...], ATT_W)
    acc = add(acc, om_ref[...], ATT_W + RWKV_W)
    ms = jnp.mean(acc * acc, axis=-1, keepdims=True)
    y_ref[...] = (acc * lax.rsqrt(ms + NORM_EPS)) * g_ref[...]


def _out(oa, ob, om, gate, x2d, w_bf16, g, tm, split_pairs):
    rows = x2d.shape[0]
    row = lambda w: pl.BlockSpec((tm, w), lambda i: (i, 0))
    oa_spec = pl.BlockSpec((3, tm, LANES), lambda i: (0, i, 0)) if split_pairs else row(ATT_W)
    return pl.pallas_call(
        functools.partial(_out_kernel, split_pairs=split_pairs),
        grid=(rows // tm,),
        in_specs=[oa_spec, row(RWKV_W), row(MEM_W), row(MIX_W), row(D_MODEL),
                  pl.BlockSpec((MIX_W, D_MODEL), lambda i: (0, 0)),
                  pl.BlockSpec((1, D_MODEL), lambda i: (0, 0))],
        out_specs=row(D_MODEL),
        out_shape=jax.ShapeDtypeStruct((rows, D_MODEL), F32),
        compiler_params=_cparams(("arbitrary",)),
        name="outproj",
    )(oa, ob, om, gate, x2d, w_bf16, g)


def kernel(x_prompt, x_sample, cache_win_k, cache_win_v, state_rwkv, state_rwkv_shift, cache_mem_k, cache_mem_v, mem_prompt, norm_in, w_in, rwkv_mu, rwkv_w0, rwkv_w2, rwkv_a0, rwkv_a2, rwkv_k_k, rwkv_k_a, rwkv_r_k, rwkv_lnx_g, rwkv_lnx_b, norm_mem, w_mem_kv, w_out, norm_final):
    B, S, _ = x_prompt.shape
    DB, T, _ = x_sample.shape
    depth = w_in.shape[0]
    assert depth == 1 and S % ATT_TILE == 0 and cache_win_k.shape[2] == max(DILATIONS) * BAND
    l = 0
    past_len = S

    half = ROPE_DIM // 2
    lane = jnp.arange(LANES)
    inv_freq = ROPE_THETA ** (-(lane % half).astype(F32) / half)
    invf = jnp.where((lane % HEAD_DIM) < ROPE_DIM, inv_freq, 0.0).reshape(1, LANES)

    row = lambda t: t.reshape(1, -1)
    w_in_b = w_in[l].astype(BF16)
    w_out_b = w_out[l].astype(BF16)
    w_kv_b = w_mem_kv[l].astype(BF16)
    zero = jnp.zeros((LORA_W, RWKV_W), F32)
    w2a2 = jnp.concatenate([jnp.concatenate([rwkv_w2[l], zero], axis=1),
                            jnp.concatenate([zero, rwkv_a2[l]], axis=1)], axis=0)
    w2a2_hi = w2a2.astype(BF16)
    w2a2_lo = (w2a2 - w2a2_hi.astype(F32)).astype(BF16)
    rw = (row(rwkv_mu[l]), row(rwkv_w0[l]), row(rwkv_a0[l]), w2a2_hi, w2a2_lo, row(rwkv_k_k[l]),
          row(rwkv_k_a[l]), row(rwkv_r_k[l]), row(rwkv_lnx_g[l]), row(rwkv_lnx_b[l]))

    tiles = S // PROJ_TILE
    n_base = -(-(tiles + 1) // 8) * 8
    pos_base = jnp.where(jnp.arange(n_base) < tiles, jnp.arange(n_base) * PROJ_TILE, past_len)
    cr, sr, cb, sb = _rope_tables(jnp.arange(PROJ_TILE, dtype=F32).reshape(-1, 1),
                                  pos_base.astype(F32).reshape(-1, 1), invf)

    xp = x_prompt.reshape(B * S, D_MODEL)
    q3, k3, v3, zb, qm, gate = _proj(xp, cr, sr, cb, sb, row(norm_in[l]), w_in_b, PROJ_TILE, True, tiles, 0, BF16)
    oa3 = _attn_prompt(q3, k3, v3, B, S)
    ob, st_p = _rwkv(zb.reshape(B, S, SHIFT_W), jnp.zeros((B, 1, SHIFT_W), F32),
                     jnp.zeros((B, RWKV_W // HEAD_DIM, HEAD_DIM, HEAD_DIM), F32), *rw, nb=B, groups=RWKV_PROMPT_GROUPS,
                     act=BF16)
    mk, mv = _memkv(mem_prompt.reshape(B * N_MEM, D_MODEL), row(norm_mem[l]), w_kv_b)
    om = _memattn(qm.reshape(B, S, MEM_W), mk, mv, MEMATTN_TILE, 1, BF16)
    y_p = _out(oa3, ob.reshape(B * S, RWKV_W), om.reshape(B * S, MEM_W), gate, xp, w_out_b,
               row(norm_final), PROJ_TILE, True)

    win = min(max(DILATIONS) * BAND, S)
    kt, vt = _tails(k3, v3, B, S, win)
    tail = lambda t: jnp.transpose(t.reshape(B, ATT_W // HEAD_DIM, HEAD_DIM, win), (0, 3, 1, 2))[None]
    heads = lambda t, n: t.reshape(1, t.shape[0], t.shape[1], n, HEAD_DIM)
    mem_heads = lambda t: jnp.transpose(t.reshape(B, MEM_W // HEAD_DIM, HEAD_DIM, N_MEM), (0, 3, 1, 2))[None]

    xs = x_sample.reshape(DB * T, D_MODEL)
    qs, ks, vs, zbs, qms, gates = _proj(xs, jnp.tile(cr[:T], (DB, 1)), jnp.tile(sr[:T], (DB, 1)), cb, sb,
                                        row(norm_in[l]), w_in_b, DB * T, False, 1, tiles, F32)
    n_past = cache_win_k.shape[2]
    minor = lambda c: jnp.transpose(c, (0, 2, 3, 1)).reshape(c.shape[0], c.shape[2] * c.shape[3], c.shape[1])
    oas = _attn_sample(qs.reshape(DB, T, ATT_W), ks.reshape(DB, T, ATT_W), vs.reshape(DB, T, ATT_W),
                       minor(cache_win_k[l]), minor(cache_win_v[l]), ATT_SAMPLE_SEQS)
    zbs3 = zbs.reshape(DB, T, SHIFT_W)
    obs, st_s = _rwkv(zbs3, state_rwkv_shift[l].reshape(DB, 1, SHIFT_W), state_rwkv[l],
                      *rw, nb=RWKV_SAMPLE_SEQS, groups=1, act=F32)
    oms = _memattn(qms.reshape(DB, T, MEM_W), minor(cache_mem_k[l]), minor(cache_mem_v[l]), T, RWKV_SAMPLE_SEQS,
                   F32)
    y_s = _out(oas.reshape(DB * T, ATT_W), obs.reshape(DB * T, RWKV_W), oms.reshape(DB * T, MEM_W), gates, xs,
               w_out_b, row(norm_final), DB * T, False)

    return (y_p.reshape(B, S, D_MODEL), y_s.reshape(DB, T, D_MODEL),
            tail(kt), tail(vt),
            st_p[None], zb.reshape(B, S, SHIFT_W)[:, -1][None],
            mem_heads(mk), mem_heads(mv),
            heads(ks.reshape(DB, T, ATT_W), ATT_W // HEAD_DIM), heads(vs.reshape(DB, T, ATT_W), ATT_W // HEAD_DIM),
            st_s[None], zbs3[:, -1][None])
```

```python
import functools

import jax
import jax.numpy as jnp
from jax import lax
from jax.experimental import pallas as pl
from jax.experimental.pallas import tpu as pltpu

F32 = jnp.float32
BF16 = jnp.bfloat16

D_MODEL = 1024
HEAD_DIM = 64
ATT_W = 384
RWKV_W = 384
MEM_W = 256
MIX_W = 1024
LORA_W = 64
SHIFT_W = 3 * RWKV_W + 2 * LORA_W
N_MEM = 256
ROPE_DIM = 16
ROPE_THETA = 500000.0
NORM_EPS = 1e-6
LNX_EPS = 64e-5
DILATIONS = (1, 4, 16)
BAND = 128
ATT_TILE = BAND * max(DILATIONS)
REGROUP = 4
LANES = 128
RWKV_CHUNK = 64
PROJ_TILE = 1024
MEMATTN_TILE = 2048
RWKV_PROMPT_GROUPS = 4
RWKV_SAMPLE_SEQS = 8
ATT_SAMPLE_SEQS = 2
NEG = -1e30
Q_SCALE = HEAD_DIM ** -0.5 * 1.4426950408889634
VMEM_LIMIT = 56 * 1024 * 1024


def _cparams(sem):
    return pltpu.CompilerParams(dimension_semantics=sem, vmem_limit_bytes=VMEM_LIMIT)


def _dot(a, b):
    return jnp.dot(a.astype(BF16), b.astype(BF16), preferred_element_type=F32)


def _dot_nt(a, b):
    return lax.dot_general(a.astype(BF16), b.astype(BF16), (((1,), (1,)), ((), ())),
                           preferred_element_type=F32)


def _dot_tn(a, b):
    return lax.dot_general(a.astype(BF16), b.astype(BF16), (((0,), (0,)), ((), ())),
                           preferred_element_type=F32)


def _dot_f32(a, b):
    return jnp.dot(a, b, preferred_element_type=F32, precision=lax.Precision.HIGHEST)


def _half0(shape):
    return (lax.broadcasted_iota(jnp.int32, shape, len(shape) - 1) & 64) == 0


def _rope_table_kernel(pos_row_ref, pos_base_ref, invf_ref, cr_ref, sr_ref, cb_ref, sb_ref):
    ang_r = pos_row_ref[...] * invf_ref[...]
    cr_ref[...] = jnp.cos(ang_r)
    sr_ref[...] = jnp.sin(ang_r)
    ang_b = pos_base_ref[...] * invf_ref[...]
    cb_ref[...] = jnp.cos(ang_b)
    sb_ref[...] = jnp.sin(ang_b)


def _rope_tables(pos_row, pos_base, invf):
    tab = lambda n: jax.ShapeDtypeStruct((n, LANES), F32)
    return pl.pallas_call(
        _rope_table_kernel,
        out_shape=[tab(pos_row.shape[0]), tab(pos_row.shape[0]), tab(pos_base.shape[0]), tab(pos_base.shape[0])],
        name="rope_tables",
    )(pos_row, pos_base, invf)


def _proj_kernel(x_ref, cr_ref, sr_ref, cb_ref, sb_ref, g_ref, w_ref, q_ref, k_ref, v_ref, zb_ref, qm_ref, gate_ref,
                 *, split_pairs, tiles_per_seq, base_offset):
    tm = x_ref.shape[0]
    base = base_offset + lax.rem(pl.program_id(0), tiles_per_seq)
    cb = cb_ref[pl.ds(base, 1), :]
    sb = sb_ref[pl.ds(base, 1), :]
    second = (lax.broadcasted_iota(jnp.int32, (1, LANES), 1) & 8) != 0

    def put(ref, p, rows, val):
        if split_pairs:
            ref[p, rows, :] = val
        else:
            ref[rows, p * LANES:(p + 1) * LANES] = val

    nsplit = 2 if tm % 16 == 0 else 1
    hm = tm // nsplit
    for part in range(nsplit):
        rows = pl.ds(part * hm, hm)
        x = x_ref[rows, :]
        ms = jnp.mean(x * x, axis=-1, keepdims=True)
        h = ((x * lax.rsqrt(ms + NORM_EPS)) * g_ref[...]).astype(BF16)
        cr = cr_ref[rows, :]
        sr = sr_ref[rows, :]
        cos = cb * cr - sb * sr
        sin = sb * cr + cb * sr
        sin_up = jnp.where(second, sin, 0.0)
        sin_dn = jnp.where(second, 0.0, -sin)

        def rope(t):
            return t * cos + pltpu.roll(t, 8, 1) * sin_up + pltpu.roll(t, LANES - 8, 1) * sin_dn

        qkv = jnp.dot(h, w_ref[:, :3 * ATT_W], preferred_element_type=F32)
        for p in range(ATT_W // LANES):
            put(q_ref, p, rows, rope(qkv[:, p * LANES:(p + 1) * LANES]) * Q_SCALE)
            put(k_ref, p, rows, rope(qkv[:, ATT_W + p * LANES:ATT_W + (p + 1) * LANES]))
            put(v_ref, p, rows, qkv[:, 2 * ATT_W + p * LANES:2 * ATT_W + (p + 1) * LANES])
        c0 = 3 * ATT_W
        zb_ref[rows, :] = jnp.dot(h, w_ref[:, c0:c0 + SHIFT_W], preferred_element_type=F32)
        c0 += SHIFT_W
        qm_ref[rows, :] = (jnp.dot(h, w_ref[:, c0:c0 + MEM_W], preferred_element_type=F32) * Q_SCALE).astype(
            qm_ref.dtype)
        c0 += MEM_W
        gate_ref[rows, :] = jnp.dot(h, w_ref[:, c0:c0 + MIX_W], preferred_element_type=F32).astype(gate_ref.dtype)


def _proj(x2d, cr, sr, cb, sb, g, w_bf16, tm, split_pairs, tiles_per_seq, base_offset, act):
    rows = x2d.shape[0]
    in_w = w_bf16.shape[1]
    if split_pairs:
        qkv_shape = jax.ShapeDtypeStruct((3, rows, LANES), F32)
        qkv_spec = pl.BlockSpec((3, tm, LANES), lambda i: (0, i, 0))
    else:
        qkv_shape = jax.ShapeDtypeStruct((rows, ATT_W), F32)
        qkv_spec = pl.BlockSpec((tm, ATT_W), lambda i: (i, 0))
    row = lambda w: pl.BlockSpec((tm, w), lambda i: (i, 0))
    full = lambda a: pl.BlockSpec(a.shape, lambda i: (0, 0))
    return pl.pallas_call(
        functools.partial(_proj_kernel, split_pairs=split_pairs, tiles_per_seq=tiles_per_seq,
                          base_offset=base_offset),
        grid=(rows // tm,),
        in_specs=[row(D_MODEL), full(cr), full(sr), full(cb), full(sb), full(g), full(w_bf16)],
        out_specs=[qkv_spec, qkv_spec, qkv_spec, row(SHIFT_W), row(MEM_W), row(MIX_W)],
        out_shape=[qkv_shape, qkv_shape, qkv_shape,
                   jax.ShapeDtypeStruct((rows, SHIFT_W), F32),
                   jax.ShapeDtypeStruct((rows, MEM_W), act),
                   jax.ShapeDtypeStruct((rows, MIX_W), act)],
        compiler_params=_cparams(("arbitrary",)),
        name="proj",
    )(x2d, cr, sr, cb, sb, g, w_bf16)


def _attn_blocks(npairs, blocks):
    h0 = _half0((BAND, LANES))
    h0k = _half0((2 * BAND, LANES))
    units = [(b, p) for b in range(len(blocks)) for p in range(npairs)]
    heads = [(i, hh) for i in range(len(units)) for hh in range(2)]
    q = [blocks[b][0](p) for b, p in units]
    kcat = [jnp.concatenate([blocks[b][1](p), blocks[b][2](p)], axis=0).astype(BF16) for b, p in units]
    vcat = [jnp.concatenate([blocks[b][3](p), blocks[b][4](p)], axis=0) for b, p in units]
    s = [_dot_nt(jnp.where(h0 if hh == 0 else jnp.logical_not(h0), q[i], 0.0), kcat[i]) + blocks[units[i][0]][5]
         for i, hh in heads]
    m = [jnp.max(s[j], axis=-1, keepdims=True) for j in range(len(heads))]
    pexp = [jnp.exp2(s[j] - m[j]).astype(BF16) for j in range(len(heads))]
    res = [jnp.dot(pexp[j], jnp.where(h0k if hh == 0 else jnp.logical_not(h0k), vcat[i], 1.0).astype(BF16),
                   preferred_element_type=F32) for j, (i, hh) in enumerate(heads)]
    lsum = [pltpu.roll(jnp.where(h0, res[2 * i + 1], res[2 * i]), HEAD_DIM, 1) for i in range(len(units))]
    outs = [[] for _ in blocks]
    for i, (b, p) in enumerate(units):
        outs[b].append((jnp.where(h0, res[2 * i], res[2 * i + 1]) / lsum[i],
                        jnp.where(h0, m[2 * i], m[2 * i + 1]) + jnp.log2(lsum[i])))
    return outs


def _attn_kernel(q_ref, kc_ref, kp_ref, vc_ref, vp_ref, o_ref, o_sc, lse_sc, yq_sc, ykv_sc):
    step = pl.program_id(1)
    has_prev = step > 0
    slot = step & 1
    prev_slot = 1 - slot
    qi = lax.broadcasted_iota(jnp.int32, (BAND, 2 * BAND), 0)
    kj = lax.broadcasted_iota(jnp.int32, (BAND, 2 * BAND), 1)
    in_cur = (kj >= BAND) & (kj - BAND <= qi)
    in_prev = (kj < BAND) & (kj >= qi)
    bias_full = jnp.where(in_cur | in_prev, 0.0, NEG).astype(F32)
    bias_first = jnp.where(in_cur | (in_prev & has_prev), 0.0, NEG).astype(F32)
    npairs = q_ref.shape[0]
    nblk = ATT_TILE // BAND
    per_iter = 2
    seg = ATT_TILE // REGROUP

    @pl.when(step == 0)
    def _():
        ykv_sc[prev_slot] = jnp.zeros(ykv_sc.shape[1:], F32)

    def regroup(c, carry):
        per_res = seg // BAND
        res, part = c >> (per_res.bit_length() - 1), c & (per_res - 1)
        src = pl.ds(res + part * (BAND * REGROUP), BAND, stride=REGROUP)
        dst = pl.ds(pl.multiple_of(c * BAND, BAND), BAND)
        for p in range(npairs):
            yq_sc[p, dst, :] = q_ref[p, src, :]
            ykv_sc[slot, 0, p, dst, :] = kc_ref[p, src, :]
            ykv_sc[slot, 1, p, dst, :] = vc_ref[p, src, :]
        return carry

    lax.fori_loop(0, nblk, regroup, 0)

    def desc(d, u, first):
        span = BAND * d
        bias = bias_first if first else bias_full
        if d == 1:
            qs = u * BAND if isinstance(u, int) else pl.multiple_of(u * BAND, BAND)
            rows = pl.ds(qs, BAND)
            prev = (lambda ref_c, ref_p: (lambda p: ref_p[p])) if first else (
                lambda ref_c, ref_p: (lambda p: ref_c[p, pl.ds(qs - BAND, BAND), :]))
            return (lambda p: q_ref[p, rows, :], prev(kc_ref, kp_ref), lambda p: kc_ref[p, rows, :],
                    prev(vc_ref, vp_ref), lambda p: vc_ref[p, rows, :], bias), qs
        assert d % REGROUP == 0
        sub = d // REGROUP
        sp, r = u >> (d.bit_length() - 1), u & (d - 1)
        qs = sp * span + r
        start = (r & (REGROUP - 1)) * seg + sp * (span // REGROUP) + (r >> (REGROUP.bit_length() - 1))
        rows = pl.ds(start, BAND, stride=sub)
        if first:
            prow, pslot = pl.ds(start + seg - span // REGROUP, BAND, stride=sub), prev_slot
        else:
            prow, pslot = pl.ds(start - span // REGROUP, BAND, stride=sub), slot
        return (lambda p: yq_sc[p, rows, :],
                lambda p: ykv_sc[pslot, 0, p, prow, :], lambda p: ykv_sc[slot, 0, p, rows, :],
                lambda p: ykv_sc[pslot, 1, p, prow, :], lambda p: ykv_sc[slot, 1, p, rows, :], bias), qs

    def merge(outs, start):
        row = pl.ds(start, BAND)
        for p, (o1, lse1) in enumerate(outs):
            os_ = [o1] + [o_sc[j, p, row, :] for j in range(len(DILATIONS) - 1)]
            ls_ = [lse1] + [lse_sc[j, p, row, :] for j in range(len(DILATIONS) - 1)]
            top = functools.reduce(jnp.maximum, ls_)
            ws = [jnp.exp2(l - top) for l in ls_]
            num = functools.reduce(lambda a, b: a + b, [w * o for w, o in zip(ws, os_)])
            o_ref[p, row, :] = (num / functools.reduce(lambda a, b: a + b, ws)).astype(o_ref.dtype)

    def run(j, d, descs):
        for (_, qs), outs in zip(descs, _attn_blocks(npairs, [blk for blk, _ in descs])):
            if j < 0:
                merge(outs, qs)
            else:
                for p, (o, lse) in enumerate(outs):
                    o_sc[j, p, pl.ds(qs, BAND, stride=d), :] = o
                    lse_sc[j, p, pl.ds(qs, BAND, stride=d), :] = lse

    def sweep(j, d):
        n_first = d
        several = lambda first: lambda i, c: (run(j, d, [desc(d, i * per_iter + k, first) for k in range(per_iter)]),
                                              c)[1]
        if n_first % per_iter == 0:
            lax.fori_loop(0, n_first // per_iter, several(True), 0)
            lo = n_first
        else:
            assert n_first == 1 and per_iter == 2
            run(j, d, [desc(d, 0, True), desc(d, 1, False)])
            lo = per_iter
        if lo < nblk:
            lax.fori_loop(lo // per_iter, nblk // per_iter, several(False), 0)

    for j, d in enumerate(DILATIONS[1:]):
        sweep(j, d)
    sweep(-1, DILATIONS[0])


def _attn_prompt(q3, k3, v3, batch, seq):
    nt = seq // ATT_TILE
    per_tile = ATT_TILE // BAND
    cur = pl.BlockSpec((3, ATT_TILE, LANES), lambda b, i: (0, b * nt + i, 0))
    prev = pl.BlockSpec((3, BAND, LANES), lambda b, i: (0, jnp.maximum((b * nt + i) * per_tile - 1, 0), 0))
    return pl.pallas_call(
        _attn_kernel,
        grid=(batch, nt),
        in_specs=[cur, cur, prev, cur, prev],
        out_specs=cur,
        out_shape=jax.ShapeDtypeStruct(q3.shape, BF16),
        scratch_shapes=[pltpu.VMEM((len(DILATIONS) - 1, 3, ATT_TILE, LANES), F32),
                        pltpu.VMEM((len(DILATIONS) - 1, 3, ATT_TILE, LANES), F32),
                        pltpu.VMEM((3, ATT_TILE, LANES), F32),
                        pltpu.VMEM((2, 2, 3, ATT_TILE, LANES), F32)],
        compiler_params=_cparams(("arbitrary", "arbitrary")),
        name="attn_prompt",
    )(q3, k3, k3, v3, v3)


def _tails_kernel(k_ref, v_ref, kt_ref, vt_ref):
    kt_ref[0] = k_ref[0].T
    vt_ref[0] = v_ref[0].T


def _tails(k3, v3, batch, seq, win):
    rows = win
    assert seq % win == 0 and win % rows == 0
    per_win, last = win // rows, seq // win - 1
    src = pl.BlockSpec((1, rows, LANES), lambda b, p, c: (p, (b * (last + 1) + last) * per_win + c, 0))
    dst = pl.BlockSpec((1, LANES, rows), lambda b, p, c: (b, p, c))
    out = jax.ShapeDtypeStruct((batch, ATT_W, win), F32)
    return pl.pallas_call(
        _tails_kernel,
        grid=(batch, ATT_W // LANES, per_win),
        in_specs=[src, src],
        out_specs=[dst, dst],
        out_shape=[out, out],
        compiler_params=_cparams(("arbitrary", "arbitrary", "arbitrary")),
        name="tails",
    )(k3, v3)


def _attn_sample_kernel(q_ref, kn_ref, vn_ref, kc_ref, vc_ref, o_ref, *, n_past, t_new, nb):
    nh = ATT_W // HEAD_DIM
    head_of_lane = lax.broadcasted_iota(jnp.int32, (t_new, ATT_W), 1) >> 6
    seqs = range(nb)
    qst = [jnp.concatenate([jnp.where(head_of_lane == h, q_ref[b], 0.0) for h in range(nh)], axis=0) for b in seqs]

    def count(delta):
        c = jnp.zeros(delta.shape, F32)
        for d in DILATIONS:
            ok = (delta >= 0) & (delta <= BAND * d) & ((delta & (d - 1)) == 0)
            c = c + jnp.where(ok, 1.0, 0.0)
        return c

    rows = nh * t_new
    t_past = lax.broadcasted_iota(jnp.int32, (rows, n_past), 0) & (t_new - 1)
    cnt_past = count(n_past + t_past - lax.broadcasted_iota(jnp.int32, (rows, n_past), 1))
    t_n = lax.broadcasted_iota(jnp.int32, (rows, t_new), 0) & (t_new - 1)
    cnt_new = count(t_n - lax.broadcasted_iota(jnp.int32, (rows, t_new), 1))

    s_past = [jnp.where(cnt_past > 0, _dot(qst[b], kc_ref[b]), NEG) for b in seqs]
    s_new = [jnp.where(cnt_new > 0, _dot_nt(qst[b], kn_ref[b]), NEG) for b in seqs]
    m = [jnp.maximum(jnp.max(s_past[b], axis=-1, keepdims=True), jnp.max(s_new[b], axis=-1, keepdims=True))
         for b in seqs]
    p_past = [cnt_past * jnp.exp2(s_past[b] - m[b]) for b in seqs]
    p_new = [cnt_new * jnp.exp2(s_new[b] - m[b]) for b in seqs]
    l = [jnp.sum(p_past[b], axis=-1, keepdims=True) + jnp.sum(p_new[b], axis=-1, keepdims=True) for b in seqs]
    o = [(_dot_nt(p_past[b], vc_ref[b]) + _dot(p_new[b], vn_ref[b])) / l[b] for b in seqs]
    for b in seqs:
        out = jnp.zeros((t_new, ATT_W), F32)
        for h in range(nh):
            out = jnp.where(head_of_lane == h, o[b][h * t_new:(h + 1) * t_new, :], out)
        o_ref[b] = out.astype(o_ref.dtype)


def _attn_sample(q, k_new, v_new, k_cache, v_cache, nb):
    db, t_new, _ = q.shape
    n_past = k_cache.shape[2]
    new = pl.BlockSpec((nb, t_new, ATT_W), lambda b: (b, 0, 0))
    cache = pl.BlockSpec((nb, ATT_W, n_past), lambda b: (b, 0, 0))
    return pl.pallas_call(
        functools.partial(_attn_sample_kernel, n_past=n_past, t_new=t_new, nb=nb),
        grid=(db // nb,),
        in_specs=[new, new, new, cache, cache],
        out_specs=new,
        out_shape=jax.ShapeDtypeStruct(q.shape, F32),
        compiler_params=_cparams(("arbitrary",)),
        name="attn_sample",
    )(q, k_new, v_new, k_cache, v_cache)


def _seg_sum(x, h0):
    s0 = jnp.sum(jnp.where(h0, x, 0.0), axis=-1, keepdims=True)
    s1 = jnp.sum(jnp.where(h0, 0.0, x), axis=-1, keepdims=True)
    return jnp.where(h0, s0, s1)


def _split_bf16(x):
    hi = x.astype(BF16)
    return hi, (x - hi.astype(F32)).astype(BF16)


def _rwkv_kernel(z_ref, sh0_ref, st0_ref, mu_ref, w0_ref, a0_ref, whi_ref, wlo_ref, kk_ref, ka_ref, rk_ref,
                 lng_ref, lnb_ref, o_ref, st_ref, st_sc, prev_sc, *, nb, groups, n_valid):
    C = RWKV_CHUNK
    R = groups * C
    npairs = RWKV_W // LANES

    @pl.when(pl.program_id(1) == 0)
    def _():
        zero = jnp.zeros((HEAD_DIM, HEAD_DIM), F32)
        for b in range(nb):
            for p in range(npairs):
                st_sc[b, p] = jnp.concatenate(
                    [jnp.concatenate([st0_ref[b, 2 * p], zero], axis=1),
                     jnp.concatenate([zero, st0_ref[b, 2 * p + 1]], axis=1)], axis=0)
        prev_sc[...] = sh0_ref[...]

    si = lax.broadcasted_iota(jnp.int32, (2 * C, 2 * C), 0)
    sj = lax.broadcasted_iota(jnp.int32, (2 * C, 2 * C), 1)
    same_head = (si >= C) == (sj >= C)
    ti2 = si & (C - 1)
    tj2 = sj & (C - 1)
    incl = same_head & (tj2 <= ti2)
    strict = same_head & (tj2 < ti2)
    eye = jnp.where(si == sj, 1.0, 0.0).astype(F32)
    levels = []
    s = 2
    while s < min(C, n_valid):
        sh = s.bit_length() - 1
        levels.append(((ti2 >> (sh + 1)) == (tj2 >> (sh + 1))) & (((ti2 >> sh) & 1) == 1) & (((tj2 >> sh) & 1) == 0))
        s *= 2
    first_level = (ti2 >> 1) == (tj2 >> 1)

    ri = lax.broadcasted_iota(jnp.int32, (C, C), 0)
    rj = lax.broadcasted_iota(jnp.int32, (C, C), 1)
    tri = jnp.where(rj <= ri, 1.0, 0.0).astype(BF16)
    rowid = lax.broadcasted_iota(jnp.int32, (C, 1), 0)
    h0 = _half0((C, LANES))
    zero_blk = jnp.zeros((2 * C, LANES), BF16)
    dot = functools.partial(jnp.dot, preferred_element_type=F32)
    pre = {}

    def stack(x):
        return jnp.concatenate([jnp.where(h0, x, 0.0), jnp.where(h0, 0.0, x)], axis=0)

    def prep(b, g):
        if n_valid < C:
            z = jnp.concatenate([z_ref[b], jnp.zeros((C - n_valid, SHIFT_W), F32)], axis=0)
        else:
            z = z_ref[b, g * C:(g + 1) * C, :]
        before = prev_sc[b] if g == 0 else z_ref[b, pl.ds(g * C - 1, 1), :]
        z_prev = jnp.where(rowid == 0, before, pltpu.roll(z, 1, 0))
        zs = z + (z_prev - z) * mu_ref[...]
        valid = rowid < (n_valid - g * C)
        if n_valid < (g + 1) * C:
            zs = jnp.where(valid, zs, 0.0)
        lat = zs[:, 3 * RWKV_W:]
        lat_hi, lat_lo = _split_bf16(jnp.where(_half0(lat.shape), jnp.tanh(lat), lat))
        lora = dot(lat_hi, whi_ref[...]) + dot(lat_lo, whi_ref[...]) + dot(lat_hi, wlo_ref[...])
        w = -jax.nn.softplus(-(w0_ref[...] + lora[:, :RWKV_W])) - 0.5
        lw = -jnp.exp(w)
        if n_valid < (g + 1) * C:
            lw = jnp.where(valid, lw, 0.0)
        a = jax.nn.sigmoid(a0_ref[...] + lora[:, RWKV_W:])
        lw_hi, lw_lo = _split_bf16(lw)
        cum = dot(tri, lw_hi) + dot(tri, lw_lo)
        for p in range(npairs):
            sl = slice(p * LANES, (p + 1) * LANES)
            r = zs[:, sl]
            k = zs[:, RWKV_W + p * LANES:RWKV_W + (p + 1) * LANES]
            v = zs[:, 2 * RWKV_W + p * LANES:2 * RWKV_W + (p + 1) * LANES]
            ap = a[:, sl]
            kk = k * kk_ref[:, sl]
            kk = kk * jnp.minimum(lax.rsqrt(_seg_sum(kk * kk, h0)), 1e12)
            k = k * (1.0 + (ap - 1.0) * ka_ref[:, sl])
            bb = kk * ap
            L = cum[:, sl]
            l_end = L[C - 1:C, :]
            e_out = jnp.exp(-L)
            e_end = jnp.exp(l_end - L)
            rt = stack(r * jnp.exp(L))
            kkh = stack(kk * jnp.exp(L - lw[:, sl])).astype(BF16)
            left = jnp.concatenate([rt.astype(BF16), kkh], axis=0)
            right = jnp.concatenate([stack(k * e_out), stack(-(bb * e_out))], axis=0).astype(BF16)
            ends = jnp.concatenate([stack(k * e_end), stack(-(bb * e_end))], axis=0).astype(BF16)
            pre[(b, g, p)] = dict(left=left, right=right, ends=ends, rt=rt, kkh=kkh, v2=stack(v).astype(BF16),
                                  decay=jnp.exp(l_end), bonus=_seg_sum(r * k * rk_ref[:, sl], h0) * v)

    def transition_stages(chains):
        def scores():
            for c in chains:
                d = pre[c]
                aa = lax.dot_general(d["left"], d["right"], (((1,), (1,)), ((), ())), preferred_element_type=F32)
                d["a_r"] = jnp.concatenate([jnp.where(incl, aa[:2 * C, :2 * C], 0.0),
                                            jnp.where(incl, aa[:2 * C, 2 * C:], 0.0)], axis=1).astype(BF16)
                d["a_kk"] = jnp.where(strict, aa[2 * C:, :2 * C], 0.0).astype(BF16)
                n_kb = jnp.where(strict, aa[2 * C:, 2 * C:], 0.0)
                d["n_kb"] = n_kb.astype(BF16)
                d["t"] = (eye + jnp.where(first_level, n_kb, 0.0)).astype(BF16)

        def odd_rows(x, s):
            return jnp.concatenate([x[i:i + s] for i in range(s, 2 * C, 2 * s)], axis=0) if s >= 16 else x

        def level_a(off, s):
            for c in chains:
                pre[c]["tn"] = dot(odd_rows(pre[c]["t"], s), jnp.where(off, pre[c]["n_kb"], 0.0)).astype(BF16)

        def level_b(s):
            for c in chains:
                t = pre[c]["t"]
                new = odd_rows(t, s) + dot(pre[c]["tn"], t).astype(BF16)
                if s >= 16:
                    parts = []
                    for j, i in enumerate(range(0, 2 * C, 2 * s)):
                        parts += [t[i:i + s], new[j * s:(j + 1) * s]]
                    new = jnp.concatenate(parts, axis=0)
                pre[c]["t"] = new

        def akv():
            for c in chains:
                pre[c]["akv"] = dot(pre[c]["a_kk"], pre[c]["v2"]).astype(BF16)

        def solve():
            for c in chains:
                d = pre[c]
                tr = dot(d["t"], jnp.concatenate([d["kkh"], d["akv"]], axis=1)).astype(BF16)
                kq, u0 = tr[:, :LANES], tr[:, LANES:]
                d["big"] = jnp.concatenate([jnp.concatenate([d["v2"], zero_blk], axis=1),
                                            jnp.concatenate([u0, kq], axis=1)], axis=0)

        def readout():
            for c in chains:
                d = pre[c]
                yr = dot(d["a_r"], d["big"])
                d["y0"] = yr[:, :LANES]
                d["rq"] = (d["rt"] + yr[:, LANES:]).astype(BF16)

        def update():
            for c in chains:
                d = pre[c]
                dg = lax.dot_general(d["big"], d["ends"], (((0,), (0,)), ((), ())), preferred_element_type=F32)
                d["dd"] = dg[:LANES]
                d["gm"] = dg[LANES:].astype(BF16)

        stages = [scores]
        for i, off in enumerate(levels):
            stages += [functools.partial(level_a, off, 2 << i), functools.partial(level_b, 2 << i)]
        return stages + [akv, solve, readout, update]

    state = {(b, p): st_sc[b, p] for b in range(nb) for p in range(npairs)}

    def carried(chains):
        for (b, g, p) in chains:
            d = pre[(b, g, p)]
            st = state[(b, p)]
            st_b = st.astype(BF16)
            d["y2"] = lax.dot_general(d["rq"], st_b, (((1,), (1,)), ((), ())), preferred_element_type=F32) + d["y0"]
            state[(b, p)] = st * d["decay"] + dot(st_b, d["gm"]) + d["dd"]
        for (b, g, p) in chains:
            d = pre.pop((b, g, p))
            y2 = d["y2"]
            y = y2[:C] + y2[C:]
            sl = slice(p * LANES, (p + 1) * LANES)
            mean = _seg_sum(y, h0) * (1.0 / HEAD_DIM)
            yc = y - mean
            var = _seg_sum(yc * yc, h0) * (1.0 / HEAD_DIM)
            yn = yc * lax.rsqrt(var + LNX_EPS) * lng_ref[:, sl] + lnb_ref[:, sl]
            rows_out = min(C, n_valid - g * C)
            o_ref[b, g * C:g * C + rows_out, sl] = (yn + d["bonus"])[:rows_out].astype(o_ref.dtype)

    items = [(b, g) for g in range(groups) for b in range(nb)]
    for item in items:
        prep(*item)
    chains = [(b, g, p) for (b, g) in items for p in range(npairs)]
    for stage in transition_stages(chains):
        stage()
    carried(chains)
    for b in range(nb):
        prev_sc[b] = z_ref[b, pl.ds(min(R, n_valid) - 1, 1), :]
    for (b, p), st in state.items():
        st_sc[b, p] = st

    @pl.when(pl.program_id(1) == pl.num_programs(1) - 1)
    def _():
        for b in range(nb):
            for p in range(npairs):
                st_ref[b, 2 * p] = st_sc[b, p, :HEAD_DIM, :HEAD_DIM]
                st_ref[b, 2 * p + 1] = st_sc[b, p, HEAD_DIM:, HEAD_DIM:]


def _rwkv(z, shift0, st0, mu, w0, a0, w_hi, w_lo, k_k, k_a, r_k, ln_g, ln_b, nb, groups, act):
    b, s, _ = z.shape
    rows = groups * RWKV_CHUNK
    if s % rows:
        assert groups == 1 and s < rows and s % 8 == 0
        rows = s
    n_valid = rows
    vec = lambda w: pl.BlockSpec((1, w), lambda i, j: (0, 0))
    st_spec = pl.BlockSpec((nb, 2 * 3, HEAD_DIM, HEAD_DIM), lambda i, j: (i, 0, 0, 0))
    wspec = pl.BlockSpec((LANES, 2 * RWKV_W), lambda i, j: (0, 0))
    return pl.pallas_call(
        functools.partial(_rwkv_kernel, nb=nb, groups=groups, n_valid=n_valid),
        grid=(b // nb, s // rows),
        in_specs=[pl.BlockSpec((nb, rows, SHIFT_W), lambda i, j: (i, j, 0)),
                  pl.BlockSpec((nb, 1, SHIFT_W), lambda i, j: (i, 0, 0)),
                  st_spec,
                  vec(SHIFT_W), vec(RWKV_W), vec(RWKV_W), wspec, wspec,
                  vec(RWKV_W), vec(RWKV_W), vec(RWKV_W), vec(RWKV_W), vec(RWKV_W)],
        out_specs=[pl.BlockSpec((nb, rows, RWKV_W), lambda i, j: (i, j, 0)), st_spec],
        out_shape=[jax.ShapeDtypeStruct((b, s, RWKV_W), act),
                   jax.ShapeDtypeStruct((b, 2 * 3, HEAD_DIM, HEAD_DIM), F32)],
        scratch_shapes=[pltpu.VMEM((nb, 3, LANES, LANES), F32), pltpu.VMEM((nb, 1, SHIFT_W), F32)],
        compiler_params=_cparams(("arbitrary", "arbitrary")),
        name="rwkv",
    )(z, shift0, st0, mu, w0, a0, w_hi, w_lo, k_k, k_a, r_k, ln_g, ln_b)


def _memkv_kernel(x_ref, g_ref, w_ref, k_ref, v_ref):
    x = x_ref[...]
    ms = jnp.mean(x * x, axis=-1, keepdims=True)
    h = ((x * lax.rsqrt(ms + NORM_EPS)) * g_ref[...]).astype(BF16)
    k = jnp.dot(h, w_ref[:, :MEM_W], preferred_element_type=F32)
    v = jnp.dot(h, w_ref[:, MEM_W:], preferred_element_type=F32)
    for b in range(k_ref.shape[0]):
        k_ref[b] = k[b * N_MEM:(b + 1) * N_MEM].T
        v_ref[b] = v[b * N_MEM:(b + 1) * N_MEM].T


def _memkv(mem2d, g, w_bf16):
    rows = mem2d.shape[0]
    out = jax.ShapeDtypeStruct((rows // N_MEM, MEM_W, N_MEM), F32)
    return pl.pallas_call(
        _memkv_kernel,
        out_shape=[out, out],
        compiler_params=pltpu.CompilerParams(vmem_limit_bytes=VMEM_LIMIT),
        name="memkv",
    )(mem2d, g, w_bf16)


def _memattn_kernel(q_ref, mk_ref, mv_ref, o_ref, *, nb):
    head_of_lane = lax.broadcasted_iota(jnp.int32, q_ref.shape[1:], 1) >> 6
    qs = [q_ref[b] for b in range(nb)]
    mks = [mk_ref[b].astype(BF16) for b in range(nb)]
    mvs = [mv_ref[b].astype(BF16) for b in range(nb)]
    outs = [jnp.zeros(q_ref.shape[1:], F32) for _ in range(nb)]
    for h in range(MEM_W // HEAD_DIM):
        mine = head_of_lane == h
        s = [_dot(jnp.where(mine, qs[b], 0.0), mks[b]) for b in range(nb)]
        pexp = [jnp.exp2(s[b] - jnp.max(s[b], axis=-1, keepdims=True)) for b in range(nb)]
        l = [jnp.sum(pexp[b], axis=-1, keepdims=True) for b in range(nb)]
        outs = [jnp.where(mine, _dot_nt(pexp[b], mvs[b]) / l[b], outs[b]) for b in range(nb)]
    for b in range(nb):
        o_ref[b] = outs[b].astype(o_ref.dtype)


def _memattn(q, mk, mv, tm, nb, act):
    b, s, _ = q.shape
    qspec = pl.BlockSpec((nb, tm, MEM_W), lambda i, j: (i, j, 0))
    mspec = pl.BlockSpec((nb,) + mk.shape[1:], lambda i, j: (i, 0, 0))
    return pl.pallas_call(
        functools.partial(_memattn_kernel, nb=nb),
        grid=(b // nb, s // tm),
        in_specs=[qspec, mspec, mspec],
        out_specs=qspec,
        out_shape=jax.ShapeDtypeStruct(q.shape, act),
        compiler_params=_cparams(("arbitrary", "arbitrary")),
        name="memattn",
    )(q, mk, mv)


def _out_kernel(oa_ref, ob_ref, om_ref, gate_ref, x_ref, w_ref, g_ref, y_ref, *, split_pairs):
    acc = x_ref[...]

    def add(acc, o, c0):
        width = o.shape[1]
        gate = gate_ref[:, c0:c0 + width]
        return acc + _dot(o * (gate * jax.nn.sigmoid(gate)), w_ref[c0:c0 + width, :])

    if split_pairs:
        for p in range(ATT_W // LANES):
            acc = add(acc, oa_ref[p], p * LANES)
    else:
        acc = add(acc, oa_ref[...], 0)
    acc = add(acc, ob_ref[...], ATT_W)
    acc = add(acc, om_ref[...], ATT_W + RWKV_W)
    ms = jnp.mean(acc * acc, axis=-1, keepdims=True)
    y_ref[...] = (acc * lax.rsqrt(ms + NORM_EPS)) * g_ref[...]


def _out(oa, ob, om, gate, x2d, w_bf16, g, tm, split_pairs):
    rows = x2d.shape[0]
    row = lambda w: pl.BlockSpec((tm, w), lambda i: (i, 0))
    oa_spec = pl.BlockSpec((3, tm, LANES), lambda i: (0, i, 0)) if split_pairs else row(ATT_W)
    return pl.pallas_call(
        functools.partial(_out_kernel, split_pairs=split_pairs),
        grid=(rows // tm,),
        in_specs=[oa_spec, row(RWKV_W), row(MEM_W), row(MIX_W), row(D_MODEL),
                  pl.BlockSpec((MIX_W, D_MODEL), lambda i: (0, 0)),
                  pl.BlockSpec((1, D_MODEL), lambda i: (0, 0))],
        out_specs=row(D_MODEL),
        out_shape=jax.ShapeDtypeStruct((rows, D_MODEL), F32),
        compiler_params=_cparams(("arbitrary",)),
        name="outproj",
    )(oa, ob, om, gate, x2d, w_bf16, g)


def kernel(x_prompt, x_sample, cache_win_k, cache_win_v, state_rwkv, state_rwkv_shift, cache_mem_k, cache_mem_v, mem_prompt, norm_in, w_in, rwkv_mu, rwkv_w0, rwkv_w2, rwkv_a0, rwkv_a2, rwkv_k_k, rwkv_k_a, rwkv_r_k, rwkv_lnx_g, rwkv_lnx_b, norm_mem, w_mem_kv, w_out, norm_final):
    B, S, _ = x_prompt.shape
    DB, T, _ = x_sample.shape
    depth = w_in.shape[0]
    assert depth == 1 and S % ATT_TILE == 0 and cache_win_k.shape[2] == max(DILATIONS) * BAND
    l = 0
    past_len = S

    half = ROPE_DIM // 2
    lane = jnp.arange(LANES)
    inv_freq = ROPE_THETA ** (-(lane % half).astype(F32) / half)
    invf = jnp.where((lane % HEAD_DIM) < ROPE_DIM, inv_freq, 0.0).reshape(1, LANES)

    row = lambda t: t.reshape(1, -1)
    w_in_b = w_in[l].astype(BF16)
    w_out_b = w_out[l].astype(BF16)
    w_kv_b = w_mem_kv[l].astype(BF16)
    zero = jnp.zeros((LORA_W, RWKV_W), F32)
    w2a2 = jnp.concatenate([jnp.concatenate([rwkv_w2[l], zero], axis=1),
                            jnp.concatenate([zero, rwkv_a2[l]], axis=1)], axis=0)
    w2a2_hi = w2a2.astype(BF16)
    w2a2_lo = (w2a2 - w2a2_hi.astype(F32)).astype(BF16)
    rw = (row(rwkv_mu[l]), row(rwkv_w0[l]), row(rwkv_a0[l]), w2a2_hi, w2a2_lo, row(rwkv_k_k[l]),
          row(rwkv_k_a[l]), row(rwkv_r_k[l]), row(rwkv_lnx_g[l]), row(rwkv_lnx_b[l]))

    tiles = S // PROJ_TILE
    n_base = -(-(tiles + 1) // 8) * 8
    pos_base = jnp.where(jnp.arange(n_base) < tiles, jnp.arange(n_base) * PROJ_TILE, past_len)
    cr, sr, cb, sb = _rope_tables(jnp.arange(PROJ_TILE, dtype=F32).reshape(-1, 1),
                                  pos_base.astype(F32).reshape(-1, 1), invf)

    xp = x_prompt.reshape(B * S, D_MODEL)
    q3, k3, v3, zb, qm, gate = _proj(xp, cr, sr, cb, sb, row(norm_in[l]), w_in_b, PROJ_TILE, True, tiles, 0, BF16)
    oa3 = _attn_prompt(q3, k3, v3, B, S)
    ob, st_p = _rwkv(zb.reshape(B, S, SHIFT_W), jnp.zeros((B, 1, SHIFT_W), F32),
                     jnp.zeros((B, RWKV_W // HEAD_DIM, HEAD_DIM, HEAD_DIM), F32), *rw, nb=B, groups=RWKV_PROMPT_GROUPS,
                     act=BF16)
    mk, mv = _memkv(mem_prompt.reshape(B * N_MEM, D_MODEL), row(norm_mem[l]), w_kv_b)
    om = _memattn(qm.reshape(B, S, MEM_W), mk, mv, MEMATTN_TILE, 1, BF16)
    y_p = _out(oa3, ob.reshape(B * S, RWKV_W), om.reshape(B * S, MEM_W), gate, xp, w_out_b,
               row(norm_final), PROJ_TILE, True)

    win = min(max(DILATIONS) * BAND, S)
    kt, vt = _tails(k3, v3, B, S, win)
    tail = lambda t: jnp.transpose(t.reshape(B, ATT_W // HEAD_DIM, HEAD_DIM, win), (0, 3, 1, 2))[None]
    heads = lambda t, n: t.reshape(1, t.shape[0], t.shape[1], n, HEAD_DIM)
    mem_heads = lambda t: jnp.transpose(t.reshape(B, MEM_W // HEAD_DIM, HEAD_DIM, N_MEM), (0, 3, 1, 2))[None]

    xs = x_sample.reshape(DB * T, D_MODEL)
    qs, ks, vs, zbs, qms, gates = _proj(xs, jnp.tile(cr[:T], (DB, 1)), jnp.tile(sr[:T], (DB, 1)), cb, sb,
                                        row(norm_in[l]), w_in_b, DB * T, False, 1, tiles, F32)
    n_past = cache_win_k.shape[2]
    minor = lambda c: jnp.transpose(c, (0, 2, 3, 1)).reshape(c.shape[0], c.shape[2] * c.shape[3], c.shape[1])
    oas = _attn_sample(qs.reshape(DB, T, ATT_W), ks.reshape(DB, T, ATT_W), vs.reshape(DB, T, ATT_W),
                       minor(cache_win_k[l]), minor(cache_win_v[l]), ATT_SAMPLE_SEQS)
    zbs3 = zbs.reshape(DB, T, SHIFT_W)
    obs, st_s = _rwkv(zbs3, state_rwkv_shift[l].reshape(DB, 1, SHIFT_W), state_rwkv[l],
                      *rw, nb=RWKV_SAMPLE_SEQS, groups=1, act=F32)
    oms = _memattn(qms.reshape(DB, T, MEM_W), minor(cache_mem_k[l]), minor(cache_mem_v[l]), T, RWKV_SAMPLE_SEQS,
                   F32)
    y_s = _out(oas.reshape(DB * T, ATT_W), obs.reshape(DB * T, RWKV_W), oms.reshape(DB * T, MEM_W), gates, xs,
               w_out_b, row(norm_final), DB * T, False)

    return (y_p.reshape(B, S, D_MODEL), y_s.reshape(DB, T, D_MODEL),
            tail(kt), tail(vt),
            st_p[None], zb.reshape(B, S, SHIFT_W)[:, -1][None],
            mem_heads(mk), mem_heads(mv),
            heads(ks.reshape(DB, T, ATT_W), ATT_W // HEAD_DIM), heads(vs.reshape(DB, T, ATT_W), ATT_W // HEAD_DIM),
            st_s[None], zbs3[:, -1][None])
```
